```python
import jax
import jax.numpy as jnp
from jax import lax
import numpy as np

D_MODEL = 1024
BATCH = 4
SEQ = 8192
DEPTH = 1
DEC_BATCH = 32
DEC_SEQ = 4
PAST_LEN = 16384
PAGE_SIZE = 128

HEAD_DIM = 64
MIX_W = D_MODEL
W_A = MIX_W // 2
W_B = MIX_W - W_A
H_A = W_A // HEAD_DIM
H_B = W_B // HEAD_DIM
DECAY_LORA = 64
ICLR_LORA = 64
GATE_LORA = 128
SHIFT_W = 3 * W_B + DECAY_LORA + ICLR_LORA + GATE_LORA
IN_W = 3 * W_A + SHIFT_W
N_EXPERTS = 32
TOP_K = 4
D_FF = D_MODEL
SWIGLU_ALPHA = 1.702
SWIGLU_LIMIT = 7.0
Q_BLOCK = 128
ROW_BLOCK = 128
RMS_EPS = 1e-6
LNX_EPS = 64e-5

kernel_name = 'hymba_stickbreak_rwkv7_moe_step'


def rms_norm(x, g):
    xf = x.astype(jnp.float32)
    y = xf * lax.rsqrt(jnp.mean(xf * xf, axis=-1, keepdims=True) + RMS_EPS)
    return (y * g.astype(jnp.float32)).astype(x.dtype)


def stick_breaking_block(q, k, v, q_pos, k_pos, bias):
    z = jnp.einsum('bqhd,bshd->bhqs', q, k).astype(jnp.float32) * (HEAD_DIM ** -0.5)
    z = z + bias.astype(jnp.float32)[None, :, None, None]
    causal = k_pos[None, :] < q_pos[:, None]
    neg_log_keep = jnp.where(causal, jax.nn.softplus(z), 0.0)
    later = lax.cumsum(neg_log_keep, axis=3, reverse=True) - neg_log_keep
    weight = jnp.where(causal, jnp.exp(jax.nn.log_sigmoid(z) - later), 0.0)
    return jnp.einsum('bhqs,bshd->bqhd', weight.astype(v.dtype), v)


def stick_breaking_prompt(q, k, v, bias):
    b, t = q.shape[0], q.shape[1]
    n_blk = t // Q_BLOCK
    k_pos = jnp.arange(t)
    q_blocks = jnp.moveaxis(q.reshape(b, n_blk, Q_BLOCK, H_A, HEAD_DIM), 1, 0)

    def one_block(args):
        q_blk, i = args
        return stick_breaking_block(q_blk, k, v, i * Q_BLOCK + jnp.arange(Q_BLOCK), k_pos, bias)

    o = lax.map(one_block, (q_blocks, jnp.arange(n_blk)))
    return jnp.moveaxis(o, 0, 1).reshape(b, t, H_A, HEAD_DIM)


def gather_past(cache, page_table):
    pages = cache[page_table]
    return pages.reshape(page_table.shape[0], page_table.shape[1] * PAGE_SIZE, H_A, HEAD_DIM)


def wkv7_scan(s0, r, w, k, v, a, b):
    def step(s, inp):
        r_t, w_t, k_t, v_t, a_t, b_t = inp
        sa = jnp.einsum('bhij,bhj->bhi', s, a_t)
        s = s * w_t[:, :, None, :] + sa[..., None] * b_t[:, :, None, :] + v_t[..., None] * k_t[:, :, None, :]
        return s, jnp.einsum('bhij,bhj->bhi', s, r_t)

    xs = tuple(jnp.moveaxis(u, 1, 0) for u in (r, w, k, v, a, b))
    s_final, ys = lax.scan(step, s0, xs)
    return s_final, jnp.moveaxis(ys, 0, 1)


def rwkv7_branch(pb, wkv0, decay_base, decay_up, iclr_base, iclr_up, gate_up, k_k, k_a, r_k, lnx_g, lnx_b):
    b, t = pb.shape[0], pb.shape[1]
    r = pb[..., :W_B]
    k = pb[..., W_B:2 * W_B]
    v = pb[..., 2 * W_B:3 * W_B]
    o = 3 * W_B
    xw = pb[..., o:o + DECAY_LORA]
    xa = pb[..., o + DECAY_LORA:o + DECAY_LORA + ICLR_LORA]
    xg = pb[..., o + DECAY_LORA + ICLR_LORA:]
    log_w = -jax.nn.softplus(-(decay_base + jnp.tanh(xw) @ decay_up)) - 0.5
    decay = jnp.exp(-jnp.exp(log_w))
    a = jax.nn.sigmoid(iclr_base + xa @ iclr_up)
    g = jax.nn.sigmoid(xg) @ gate_up
    heads = lambda u: u.reshape(b, t, H_B, HEAD_DIM)
    kk_f = heads(k * k_k).astype(jnp.float32)
    kk = (kk_f / jnp.maximum(jnp.sqrt(jnp.sum(kk_f * kk_f, -1, keepdims=True)), 1e-12)).astype(pb.dtype)
    k = k * (1 + (a - 1) * k_a)
    r_h, k_h, v_h, a_h = heads(r), heads(k), heads(v), heads(a)
    s_final, y = wkv7_scan(wkv0, r_h, heads(decay), k_h, v_h, -kk, kk * a_h)
    yf = y.astype(jnp.float32)
    mu = jnp.mean(yf, -1, keepdims=True)
    var = jnp.mean(jnp.square(yf - mu), -1, keepdims=True)
    yn = ((yf - mu) * lax.rsqrt(var + LNX_EPS)).reshape(b, t, W_B) * lnx_g + lnx_b
    bonus = jnp.sum(r_h * k_h * r_k, -1, keepdims=True) * v_h
    out = (yn.astype(pb.dtype) + bonus.reshape(b, t, W_B)) * g
    return out, s_final


def moe_ffn(h, router_w, router_b, e_w1, e_b1, e_w2, e_b2):
    m = h.shape[0]
    logits = (h @ router_w + router_b).astype(jnp.float32)
    top_val, top_idx = lax.top_k(logits, TOP_K)
    gate = jax.nn.softmax(top_val, axis=-1).astype(h.dtype)
    n_rows = m * TOP_K
    flat_e = top_idx.reshape(-1)
    flat_tok = jnp.repeat(jnp.arange(m, dtype=jnp.int32), TOP_K)
    flat_gate = gate.reshape(-1)
    order = jnp.argsort(flat_e)
    e_sorted = flat_e[order]
    counts = jnp.bincount(flat_e, length=N_EXPERTS)
    padded = (counts + ROW_BLOCK - 1) // ROW_BLOCK * ROW_BLOCK
    pad_end = jnp.cumsum(padded)
    pad_start = pad_end - padded
    grp_start = jnp.cumsum(counts) - counts
    dest = pad_start[e_sorted] + jnp.arange(n_rows) - grp_start[e_sorted]
    n_blocks = (n_rows + N_EXPERTS * (ROW_BLOCK - 1) + ROW_BLOCK - 1) // ROW_BLOCK
    n_slots = n_blocks * ROW_BLOCK
    slot_tok = jnp.full((n_slots,), m, jnp.int32).at[dest].set(flat_tok[order])
    slot_gate = jnp.zeros((n_slots,), h.dtype).at[dest].set(flat_gate[order])
    blk_expert = jnp.minimum(jnp.searchsorted(pad_end, jnp.arange(n_blocks) * ROW_BLOCK, side='right'), N_EXPERTS - 1)

    def expert_block(args):
        e, tok, gt = args
        xb = h.at[tok].get(mode='fill', fill_value=0)
        u = xb @ e_w1[e] + e_b1[e]
        u_glu = jnp.minimum(u[:, :D_FF], SWIGLU_LIMIT)
        u_lin = jnp.clip(u[:, D_FF:], -SWIGLU_LIMIT, SWIGLU_LIMIT)
        act = u_glu * jax.nn.sigmoid(SWIGLU_ALPHA * u_glu) * (u_lin + 1)
        return (act @ e_w2[e] + e_b2[e]) * gt[:, None]

    out = lax.map(expert_block, (blk_expert, slot_tok.reshape(n_blocks, ROW_BLOCK), slot_gate.reshape(n_blocks, ROW_BLOCK)))
    return jnp.zeros_like(h).at[slot_tok].add(out.reshape(n_slots, -1), mode='drop')


def trunk_layer(x, c, attend, shift_prev, wkv_prev, p):
    b, t = x.shape[0], x.shape[1]
    mod = jax.nn.silu(c) @ p['w_ada'] + p['b_ada']
    sh1, sc1, gt1, sh2, sc2, gt2 = [u[:, None, :] for u in jnp.split(mod, 6, axis=-1)]
    h = rms_norm(x, p['g_mix_pre']) * (1 + sc1) + sh1
    proj = h @ p['w_in']
    pa, pb = proj[..., :3 * W_A], proj[..., 3 * W_A:]
    prev = jnp.concatenate([shift_prev[:, None, :], pb[:, :-1]], axis=1)
    pb_mixed = pb + (prev - pb) * p['mu_shift']
    q = pa[..., :W_A].reshape(b, t, H_A, HEAD_DIM)
    k = pa[..., W_A:2 * W_A].reshape(b, t, H_A, HEAD_DIM)
    v = pa[..., 2 * W_A:].reshape(b, t, H_A, HEAD_DIM)
    o_a = rms_norm(attend(q, k, v, p['sb_bias']).reshape(b, t, W_A), p['g_attn_out'])
    o_b, wkv_new = rwkv7_branch(pb_mixed, wkv_prev, p['decay_base'], p['decay_up'], p['iclr_base'], p['iclr_up'],
                                p['gate_up'], p['k_k'], p['k_a'], p['r_k'], p['lnx_g'], p['lnx_b'])
    mixed = jnp.concatenate([o_a, o_b], axis=-1) @ p['w_out']
    x = x + gt1 * rms_norm(mixed, p['g_mix_post'])
    h2 = rms_norm(x, p['g_ffn_pre']) * (1 + sc2) + sh2
    f = moe_ffn(h2.reshape(b * t, D_MODEL), p['router_w'], p['router_b'], p['e_w1'], p['e_b1'],
                p['e_w2'], p['e_b2']).reshape(b, t, D_MODEL)
    x = x + gt2 * rms_norm(f, p['g_ffn_post'])
    return x, k, v, wkv_new, pb[:, -1]


def setup_inputs(seed: int = 0) -> dict:
    key = jax.random.key(seed)
    ks = jax.random.split(key, 37)
    n_pages = PAST_LEN // PAGE_SIZE
    n_phys = (DEC_BATCH * n_pages * 5) // 4
    L = DEPTH

    def nrm(k, shape, scale):
        return jax.random.normal(k, shape, jnp.float32) * scale

    def gain(k, shape):
        return 1.0 + nrm(k, shape, 0.02)

    page_table = jax.random.permutation(ks[6], n_phys)[:DEC_BATCH * n_pages].reshape(DEC_BATCH, n_pages).astype(jnp.int32)
    return {
        'x_prompt': nrm(ks[0], (BATCH, SEQ, D_MODEL), 1.0),
        'x_sample': nrm(ks[1], (DEC_BATCH, DEC_SEQ, D_MODEL), 1.0),
        'cache_k': nrm(ks[2], (L, n_phys, PAGE_SIZE, H_A, HEAD_DIM), 1.0),
        'cache_v': nrm(ks[3], (L, n_phys, PAGE_SIZE, H_A, HEAD_DIM), 1.0),
        'state_wkv': nrm(ks[4], (L, DEC_BATCH, H_B, HEAD_DIM, HEAD_DIM), 0.5),
        'state_shift': nrm(ks[5], (L, DEC_BATCH, SHIFT_W), 1.0),
        'page_table': page_table,
        'c_prompt': nrm(ks[7], (BATCH, D_MODEL), 1.0),
        'c_sample': nrm(ks[8], (DEC_BATCH, D_MODEL), 1.0),
        'w_ada': nrm(ks[9], (L, D_MODEL, 6 * D_MODEL), D_MODEL ** -0.5),
        'b_ada': nrm(ks[10], (L, 6 * D_MODEL), 0.02),
        'g_mix_pre': gain(ks[11], (L, D_MODEL)),
        'g_mix_post': gain(ks[12], (L, D_MODEL)),
        'g_ffn_pre': gain(ks[13], (L, D_MODEL)),
        'g_ffn_post': gain(ks[14], (L, D_MODEL)),
        'w_in': nrm(ks[15], (L, D_MODEL, IN_W), D_MODEL ** -0.5),
        'mu_shift': jax.random.uniform(ks[16], (L, SHIFT_W), jnp.float32),
        'decay_base': jax.random.uniform(ks[17], (L, W_B), jnp.float32, -5.0, 0.0),
        'decay_up': nrm(ks[18], (L, DECAY_LORA, W_B), 0.1 * DECAY_LORA ** -0.5),
        'iclr_base': nrm(ks[19], (L, W_B), 0.1),
        'iclr_up': nrm(ks[20], (L, ICLR_LORA, W_B), 0.1 * ICLR_LORA ** -0.5),
        'gate_up': nrm(ks[21], (L, GATE_LORA, W_B), GATE_LORA ** -0.5),
        'k_k': 0.85 + nrm(ks[22], (L, W_B), 0.05),
        'k_a': 1.0 + nrm(ks[23], (L, W_B), 0.05),
        'r_k': nrm(ks[24], (L, H_B, HEAD_DIM), 0.1),
        'lnx_g': gain(ks[25], (L, W_B)),
        'lnx_b': nrm(ks[26], (L, W_B), 0.02),
        'g_attn_out': gain(ks[27], (L, W_A)),
        'sb_bias': jax.random.uniform(ks[35], (L, H_A), jnp.float32, -7.0, -4.0),
        'w_out': nrm(ks[28], (L, MIX_W, D_MODEL), MIX_W ** -0.5),
        'router_w': nrm(ks[29], (L, D_MODEL, N_EXPERTS), D_MODEL ** -0.5),
        'router_b': nrm(ks[30], (L, N_EXPERTS), 0.01),
        'e_w1': nrm(ks[31], (L, N_EXPERTS, D_MODEL, 2 * D_FF), D_MODEL ** -0.5),
        'e_b1': nrm(ks[32], (L, N_EXPERTS, 2 * D_FF), 0.01),
        'e_w2': nrm(ks[33], (L, N_EXPERTS, D_FF, D_MODEL), D_FF ** -0.5),
        'e_b2': nrm(ks[34], (L, N_EXPERTS, D_MODEL), 0.01),
    }


def reference(x_prompt, x_sample, cache_k, cache_v, state_wkv, state_shift, page_table, c_prompt, c_sample,
              w_ada, b_ada, g_mix_pre, g_mix_post, g_ffn_pre, g_ffn_post, w_in, mu_shift,
              decay_base, decay_up, iclr_base, iclr_up, gate_up, k_k, k_a, r_k, lnx_g, lnx_b,
              g_attn_out, sb_bias, w_out, router_w, router_b, e_w1, e_b1, e_w2, e_b2):
    weights = dict(w_ada=w_ada, b_ada=b_ada, g_mix_pre=g_mix_pre, g_mix_post=g_mix_post,
                   g_ffn_pre=g_ffn_pre, g_ffn_post=g_ffn_post, w_in=w_in, mu_shift=mu_shift,
                   decay_base=decay_base, decay_up=decay_up, iclr_base=iclr_base, iclr_up=iclr_up,
                   gate_up=gate_up, k_k=k_k, k_a=k_a, r_k=r_k, lnx_g=lnx_g, lnx_b=lnx_b,
                   g_attn_out=g_attn_out, sb_bias=sb_bias, w_out=w_out, router_w=router_w, router_b=router_b,
                   e_w1=e_w1, e_b1=e_b1, e_w2=e_w2, e_b2=e_b2)
    past_len = page_table.shape[1] * PAGE_SIZE
    bp = x_prompt.shape[0]
    hp, hs = x_prompt, x_sample
    kp_l, vp_l, ks_l, vs_l, wp_l, ws_l, shp_l, shs_l = [], [], [], [], [], [], [], []
    for l in range(DEPTH):
        p = {name: w[l] for name, w in weights.items()}
        shift0 = jnp.zeros((bp, SHIFT_W), hp.dtype)
        wkv0 = jnp.zeros((bp, H_B, HEAD_DIM, HEAD_DIM), hp.dtype)
        hp, kp, vp, wp, shp = trunk_layer(hp, c_prompt, stick_breaking_prompt, shift0, wkv0, p)
        k_past = gather_past(cache_k[l], page_table)
        v_past = gather_past(cache_v[l], page_table)

        def attend_sample(q, k, v, bias, k_past=k_past, v_past=v_past):
            k_all = jnp.concatenate([k_past, k], axis=1)
            v_all = jnp.concatenate([v_past, v], axis=1)
            q_pos = past_len + jnp.arange(q.shape[1])
            return stick_breaking_block(q, k_all, v_all, q_pos, jnp.arange(k_all.shape[1]), bias)

        hs, ks_, vs_, ws, shs = trunk_layer(hs, c_sample, attend_sample, state_shift[l], state_wkv[l], p)
        kp_l.append(kp); vp_l.append(vp); ks_l.append(ks_); vs_l.append(vs_)
        wp_l.append(wp); ws_l.append(ws); shp_l.append(shp); shs_l.append(shs)
    return (hp, hs, jnp.stack(kp_l), jnp.stack(vp_l), jnp.stack(ks_l), jnp.stack(vs_l),
            jnp.stack(wp_l), jnp.stack(ws_l), jnp.stack(shp_l), jnp.stack(shs_l))
```

```python
import functools

import jax
import jax.numpy as jnp
from jax import lax
from jax.experimental import pallas as pl
from jax.experimental.pallas import tpu as pltpu

f32 = jnp.float32
bf16 = jnp.bfloat16
i32 = jnp.int32
SDS = jax.ShapeDtypeStruct

D_MODEL = 1024
HEAD_DIM = 64
W_A = 512
W_B = 512
H_A = W_A // HEAD_DIM
H_B = W_B // HEAD_DIM
DECAY_LORA = 64
ICLR_LORA = 64
GATE_LORA = 128
SHIFT_W = 3 * W_B + DECAY_LORA + ICLR_LORA + GATE_LORA
N_EXPERTS = 32
TOP_K = 4
D_FF = D_MODEL
SWIGLU_ALPHA = 1.702
SWIGLU_LIMIT = 7.0
PAGE_SIZE = 128
RMS_EPS = 1e-6
LNX_EPS = 64e-5
LOG2E = 1.4426950408889634

V7X_SUBLANES = 8
V7X_LANES = 128
VMEM_LIMIT = 56 * 1024 * 1024

ROW_TILE = 256
WKV_CHUNK = 64
ATTN_TQ = 256
ATTN_TK = 128
PAGES_PER_STEP = 4
MOE_TILE = 256


def _cparams(sem):
    return pltpu.CompilerParams(dimension_semantics=sem, vmem_limit_bytes=VMEM_LIMIT)


def _bdot(a, b):
    return jnp.dot(a.astype(bf16), b.astype(bf16), preferred_element_type=f32)


def _fdot(a, b, dims=(((1,), (0,)), ((), ()))):
    return lax.dot_general(a, b, dims, precision=lax.Precision.HIGHEST, preferred_element_type=f32)


_NT = (((1,), (1,)), ((), ()))
_TN = (((0,), (0,)), ((), ()))


def _split_dot(x, m_bf16):
    hi = x.astype(bf16)
    lo = (x - hi.astype(f32)).astype(bf16)
    return (jnp.dot(hi, m_bf16, preferred_element_type=f32)
            + jnp.dot(lo, m_bf16, preferred_element_type=f32))


def _sigmoid(x):
    return 1.0 / (1.0 + jnp.exp(-x))


def _softplus(x):
    return jnp.maximum(x, 0.0) + jnp.log(1.0 + jnp.exp(-jnp.abs(x)))


def _softplus2(z):
    return jnp.maximum(z, 0.0) + jnp.log2(1.0 + jnp.exp2(-jnp.abs(z)))


def _rms(x, g):
    return x * lax.rsqrt(jnp.mean(x * x, axis=-1, keepdims=True) + RMS_EPS) * g


def _ada_kernel(c_ref, w_ref, b_ref, o_ref):
    c = c_ref[...]
    o_ref[...] = _bdot(c * _sigmoid(c), w_ref[...]) + b_ref[...]


def _ada_mod(c, w_ada, b_ada):
    n, d = c.shape
    nout = w_ada.shape[1]
    tn = 1536
    return pl.pallas_call(
        _ada_kernel,
        grid=(nout // tn,),
        in_specs=[pl.BlockSpec((n, d), lambda j: (0, 0)),
                  pl.BlockSpec((d, tn), lambda j: (0, j)),
                  pl.BlockSpec((1, tn), lambda j: (0, j))],
        out_specs=pl.BlockSpec((n, tn), lambda j: (0, j)),
        out_shape=SDS((n, nout), f32),
        compiler_params=_cparams(("arbitrary",)),
        name="ada_mod",
    )(c, w_ada, b_ada.reshape(1, nout))


def _inproj_kernel(x_ref, sh_ref, sc_ref, g_ref, w_ref, q_ref, k_ref, v_ref, pb_ref):
    nb, tt, d = x_ref.shape
    h = _rms(x_ref[...], g_ref[...]) * (1.0 + sc_ref[...]) + sh_ref[...]
    hb = h.reshape(nb * tt, d).astype(bf16)
    q_ref[...] = jnp.dot(hb, w_ref[:, 0:W_A], preferred_element_type=f32).reshape(nb, tt, W_A)
    k_ref[...] = jnp.dot(hb, w_ref[:, W_A:2 * W_A], preferred_element_type=f32).reshape(nb, tt, W_A)
    v_ref[...] = jnp.dot(hb, w_ref[:, 2 * W_A:3 * W_A], preferred_element_type=f32).reshape(nb, tt, W_A)
    pb_ref[...] = jnp.dot(hb, w_ref[:, 3 * W_A:], preferred_element_type=f32).reshape(nb, tt, SHIFT_W)


def _in_proj(x, sh1, sc1, g, w_in_bf16, nb, tt):
    b, t, d = x.shape
    row = lambda w: pl.BlockSpec((nb, tt, w), lambda i, j: (i, j, 0))
    mod = pl.BlockSpec((nb, 1, d), lambda i, j: (i, 0, 0))
    return pl.pallas_call(
        _inproj_kernel,
        grid=(b // nb, t // tt),
        in_specs=[row(d), mod, mod,
                  pl.BlockSpec((1, d), lambda i, j: (0, 0)),
                  pl.BlockSpec(w_in_bf16.shape, lambda i, j: (0, 0))],
        out_specs=[row(W_A), row(W_A), row(W_A), row(SHIFT_W)],
        out_shape=[SDS((b, t, W_A), f32)] * 3 + [SDS((b, t, SHIFT_W), f32)],
        compiler_params=_cparams(("arbitrary", "arbitrary")),
        name="in_proj",
    )(x, sh1, sc1, g.reshape(1, d), w_in_bf16)


def _attn_prompt_kernel(bias_ref, q_ref, k_ref, v_ref, uo_ref, o_ref, acc_ref, *, tq, tk):
    hp = pl.program_id(1)
    qi = pl.program_id(2)
    ratio = tq // tk
    lane = lax.broadcasted_iota(i32, (tq, 2 * HEAD_DIM), 1)
    q = q_ref[0] * (LOG2E * HEAD_DIM ** -0.5)
    in_head = (lane < HEAD_DIM, lane >= HEAD_DIM)
    qm = [jnp.where(m, q, 0.0).astype(bf16) for m in in_head]
    bias = [bias_ref[2 * hp + h] * LOG2E for h in range(2)]
    uo = uo_ref[...]
    acc_ref[...] = jnp.zeros_like(acc_ref)
    row = lax.broadcasted_iota(i32, (tq, tk), 0)
    col = lax.broadcasted_iota(i32, (tq, tk), 1)

    def tile(k_start, mask, carries):
        kb = k_ref[0, pl.ds(k_start, tk), :].astype(bf16)
        vb = v_ref[0, pl.ds(k_start, tk), :].astype(bf16)
        new = []
        for h in range(2):
            z = lax.dot_general(qm[h], kb, _NT, preferred_element_type=f32) + bias[h]
            sp = _softplus2(z)
            if mask is not None:
                sp = jnp.where(mask, sp, 0.0)
            cr = jnp.dot(sp.astype(bf16), uo, preferred_element_type=f32)
            w = jnp.exp2(z - cr[:, :tk] - carries[h])
            if mask is not None:
                w = jnp.where(mask, w, 0.0)
            acc_ref[h] += jnp.dot(w.astype(bf16), vb, preferred_element_type=f32)
            new.append(carries[h] + cr[:, tk:])
        return tuple(new)

    carries = (jnp.zeros((tq, tk), f32), jnp.zeros((tq, tk), f32))
    for dgl in range(ratio - 1, -1, -1):
        k_start = pl.multiple_of(qi * tq + dgl * tk, tk)
        carries = tile(k_start, (col + dgl * tk) < row, carries)

    def body(n, carries):
        k_start = pl.multiple_of((qi * ratio - 1 - n) * tk, tk)
        return tile(k_start, None, carries)

    lax.fori_loop(0, qi * ratio, body, carries)
    o_ref[0] = jnp.where(in_head[0], acc_ref[0], acc_ref[1])


def _attn_prompt(q, k, v, sb_bias):
    b, t, _ = q.shape
    tq, tk = ATTN_TQ, ATTN_TK
    j = lax.broadcasted_iota(i32, (tk, tk), 0)
    s = lax.broadcasted_iota(i32, (tk, tk), 1)
    uo = jnp.concatenate([(j >= s).astype(bf16), jnp.ones((tk, tk), bf16)], axis=1)
    seq = pl.BlockSpec((1, t, 2 * HEAD_DIM), lambda bi, hp, qi: (bi, 0, hp))
    return pl.pallas_call(
        functools.partial(_attn_prompt_kernel, tq=tq, tk=tk),
        grid=(b, H_A // 2, t // tq),
        in_specs=[pl.BlockSpec(memory_space=pltpu.SMEM),
                  pl.BlockSpec((1, tq, 2 * HEAD_DIM), lambda bi, hp, qi: (bi, qi, hp)),
                  seq, seq,
                  pl.BlockSpec((tk, 2 * tk), lambda bi, hp, qi: (0, 0))],
        out_specs=pl.BlockSpec((1, tq, 2 * HEAD_DIM), lambda bi, hp, qi: (bi, qi, hp)),
        out_shape=SDS((b, t, W_A), f32),
        scratch_shapes=[pltpu.VMEM((2, tq, 2 * HEAD_DIM), f32)],
        compiler_params=_cparams(("arbitrary", "arbitrary", "arbitrary")),
        name="attn_prompt",
    )(sb_bias, q, k, v, uo)


def _attn_sample_kernel(pt_ref, bias_ref, qbd_ref, knew_ref, vnew_ref, lt_ref, *rest, n_new, pages):
    kp_refs = rest[:pages]
    vp_refs = rest[pages:2 * pages]
    o_ref, acc_ref, carry_ref = rest[2 * pages:]
    step = pl.program_id(1)
    n_col = H_A * n_new
    qbd = qbd_ref[0]
    bias = bias_ref[...]

    @pl.when(step == 0)
    def _():
        pad = knew_ref.shape[1]
        z = jnp.dot(knew_ref[0].astype(bf16), qbd, preferred_element_type=f32) + bias
        s_idx = lax.broadcasted_iota(i32, (pad, n_col), 0)
        t_idx = lax.broadcasted_iota(i32, (pad, n_col), 1) % n_new
        mask = s_idx < t_idx
        sp = jnp.where(mask, _softplus2(z), 0.0)
        incl = jnp.zeros_like(sp)
        for j in range(n_new):
            incl = incl + jnp.where(s_idx <= j, sp[j:j + 1, :], 0.0)
        w = jnp.where(mask, jnp.exp2(z - incl), 0.0)
        acc_ref[...] = _fdot(w, vnew_ref[0], _TN)
        carry_ref[...] = incl[0:1, :]

    lt = lt_ref[...]
    for i in range(pages):
        z = jnp.dot(kp_refs[i][0].astype(bf16), qbd, preferred_element_type=f32) + bias
        sp = _softplus2(z)
        incl = jnp.dot(lt, sp.astype(bf16), preferred_element_type=f32)
        w = jnp.exp2(z - incl - carry_ref[...])
        acc_ref[...] += lax.dot_general(w.astype(bf16), vp_refs[i][0].astype(bf16), _TN,
                                        preferred_element_type=f32)
        carry_ref[...] += incl[0:1, :]

    @pl.when(step == pl.num_programs(1) - 1)
    def _():
        r = lax.broadcasted_iota(i32, (n_col, W_A), 0) // n_new
        c = lax.broadcasted_iota(i32, (n_col, W_A), 1) // HEAD_DIM
        om = jnp.where(r == c, acc_ref[...], 0.0)
        pad = o_ref.shape[1]
        tt = lax.broadcasted_iota(i32, (pad, n_col), 0)
        cc = lax.broadcasted_iota(i32, (pad, n_col), 1) % n_new
        fold = (tt == cc).astype(bf16)
        hi = om.astype(bf16)
        lo = (om - hi.astype(f32)).astype(bf16)
        o_ref[0] = (jnp.dot(fold, hi, preferred_element_type=f32)
                    + jnp.dot(fold, lo, preferred_element_type=f32))


def _attn_sample(q, k_new, v_new, cache_k, cache_v, page_table, sb_bias, n_new):
    b, pad, _ = q.shape
    n_pages = page_table.shape[1]
    pages = PAGES_PER_STEP
    n_col = H_A * n_new
    scale = LOG2E * HEAD_DIM ** -0.5
    qh = q[:, :n_new].reshape(b, n_new, H_A, HEAD_DIM) * scale
    eye = jnp.eye(H_A, dtype=f32)
    qbd = (jnp.transpose(qh, (0, 2, 3, 1))[:, :, :, None, :] * eye[None, :, None, :, None])
    qbd = qbd.reshape(b, W_A, n_col).astype(bf16)
    bias = jnp.repeat(sb_bias * LOG2E, n_new).reshape(1, n_col)
    s = lax.broadcasted_iota(i32, (PAGE_SIZE, PAGE_SIZE), 0)
    j = lax.broadcasted_iota(i32, (PAGE_SIZE, PAGE_SIZE), 1)
    lt = (j >= s).astype(bf16)

    def page_spec(i):
        return pl.BlockSpec((1, PAGE_SIZE, W_A),
                            lambda bi, st, pt, i=i: (pt[bi, n_pages - 1 - (st * pages + i)], 0, 0))

    new_spec = pl.BlockSpec((1, pad, W_A), lambda bi, st, pt: (bi, 0, 0))
    grid_spec = pltpu.PrefetchScalarGridSpec(
        num_scalar_prefetch=1,
        grid=(b, n_pages // pages),
        in_specs=[pl.BlockSpec((1, n_col), lambda bi, st, pt: (0, 0)),
                  pl.BlockSpec((1, W_A, n_col), lambda bi, st, pt: (bi, 0, 0)),
                  new_spec, new_spec,
                  pl.BlockSpec((PAGE_SIZE, PAGE_SIZE), lambda bi, st, pt: (0, 0))]
                 + [page_spec(i) for i in range(pages)] * 2,
        out_specs=pl.BlockSpec((1, pad, W_A), lambda bi, st, pt: (bi, 0, 0)),
        scratch_shapes=[pltpu.VMEM((n_col, W_A), f32), pltpu.VMEM((1, n_col), f32)],
    )
    return pl.pallas_call(
        functools.partial(_attn_sample_kernel, n_new=n_new, pages=pages),
        grid_spec=grid_spec,
        out_shape=SDS((b, pad, W_A), f32),
        compiler_params=_cparams(("arbitrary", "arbitrary")),
        name="attn_sample",
    )(page_table, bias, qbd, k_new, v_new, lt,
      *([cache_k] * pages), *([cache_v] * pages))


def _prep_kernel(pb_ref, pf_ref, mu_ref, dbase_ref, dup_ref, ibase_ref, iup_ref, gup_ref,
                 kk_ref, ka_ref, rk_ref, hsum_ref,
                 r_out, lw_out, k_out, v_out, a_out, b_out, g_out, bonus_out, *, t_real):
    nb, tt, w = pb_ref.shape
    pb = pb_ref[...]
    tpos = lax.broadcasted_iota(i32, pb.shape, 1)
    prev = jnp.where(tpos == 0, pf_ref[:, 0], pltpu.roll(pb, 1, axis=1))
    x = (pb + (prev - pb) * mu_ref[...]).reshape(nb * tt, w)
    r = x[:, 0:W_B]
    k = x[:, W_B:2 * W_B]
    v = x[:, 2 * W_B:3 * W_B]
    o = 3 * W_B
    xw = x[:, o:o + DECAY_LORA]
    xa = x[:, o + DECAY_LORA:o + DECAY_LORA + ICLR_LORA]
    xg = x[:, o + DECAY_LORA + ICLR_LORA:]
    log_w = -_softplus(-(dbase_ref[...] + _bdot(jnp.tanh(xw), dup_ref[...]))) - 0.5
    lw = -jnp.exp(log_w)
    a = _sigmoid(ibase_ref[...] + _bdot(xa, iup_ref[...]))
    g = _bdot(_sigmoid(xg), gup_ref[...])
    hsum = hsum_ref[...]
    kkf = k * kk_ref[...]
    kk = kkf / jnp.maximum(jnp.sqrt(_split_dot(kkf * kkf, hsum)), 1e-12)
    k2 = k * (1.0 + (a - 1.0) * ka_ref[...])
    bonus = _split_dot(r * k2 * rk_ref[...], hsum) * v
    na = -kk
    kb = kk * a
    if t_real < tt:
        valid = (lax.broadcasted_iota(i32, (nb, tt, W_B), 1) < t_real).reshape(nb * tt, W_B)
        zero = lambda u: jnp.where(valid, u, 0.0)
        r, lw, k2, v, na, kb = zero(r), zero(lw), zero(k2), zero(v), zero(na), zero(kb)
    g_out[...] = g.reshape(nb, tt, W_B)
    bonus_out[...] = bonus.reshape(nb, tt, W_B)
    for val, ref in ((r, r_out), (lw, lw_out), (k2, k_out), (v, v_out), (na, a_out), (kb, b_out)):
        val = val.reshape(nb, tt, W_B)
        for h in range(H_B):
            ref[:, h, :, :] = val[:, :, h * HEAD_DIM:(h + 1) * HEAD_DIM]


def _rwkv_prep(pb, prev_first, p, nb, tt, t_real):
    b, t, w = pb.shape
    row = lambda i, j: (i, j, 0)
    const = lambda i, j: (0, 0)
    vec = lambda a: a.reshape(1, -1)
    head_of = jnp.arange(W_B) // HEAD_DIM
    hsum = (head_of[:, None] == head_of[None, :]).astype(bf16)
    params = [vec(p['mu_shift']), vec(p['decay_base']), p['decay_up'], vec(p['iclr_base']), p['iclr_up'],
              p['gate_up'], vec(p['k_k']), vec(p['k_a']), vec(p['r_k']), hsum]
    heads = pl.BlockSpec((nb, H_B, tt, HEAD_DIM), lambda i, j: (i, 0, j, 0))
    return pl.pallas_call(
        functools.partial(_prep_kernel, t_real=t_real),
        grid=(b // nb, t // tt),
        in_specs=[pl.BlockSpec((nb, tt, w), row),
                  pl.BlockSpec((nb, 1, 1, w), lambda i, j: (i, j, 0, 0))]
                 + [pl.BlockSpec(a.shape, const) for a in params],
        out_specs=[heads] * 6 + [pl.BlockSpec((nb, tt, W_B), row)] * 2,
        out_shape=[SDS((b, H_B, t, HEAD_DIM), f32)] * 6 + [SDS((b, t, W_B), f32)] * 2,
        compiler_params=_cparams(("arbitrary", "arbitrary")),
        name="rwkv_prep",
    )(pb, prev_first, *params)


def _wkv_kernel(r_ref, lw_ref, k_ref, v_ref, a_ref, b_ref, s0_ref, y_ref, s_out, s_ref, *, chunk):
    c = chunk
    step = pl.program_id(1)

    @pl.when(step == 0)
    def _():
        s_ref[...] = s0_ref[0]

    ti = lax.broadcasted_iota(i32, (c, c), 0)
    si = lax.broadcasted_iota(i32, (c, c), 1)
    incl = ti >= si
    strict = ti > si
    tri = incl.astype(f32)
    eye = (ti == si).astype(f32)
    levels = max(c.bit_length() - 2, 0)
    for h in range(H_B):
        lw = lw_ref[0, h]
        cum = _fdot(tri, lw)
        w_in = jnp.exp(cum)
        w_out = jnp.exp(-cum)
        at = a_ref[0, h] * jnp.exp(cum - lw)
        bt = b_ref[0, h] * w_out
        kt = k_ref[0, h] * w_out
        rt = r_ref[0, h] * w_in
        v = v_ref[0, h]
        s0 = s_ref[h]
        a_ab = jnp.where(strict, _fdot(at, bt, _NT), 0.0)
        a_ak = jnp.where(strict, _fdot(at, kt, _NT), 0.0)
        a_rb = jnp.where(incl, _fdot(rt, bt, _NT), 0.0)
        a_rk = jnp.where(incl, _fdot(rt, kt, _NT), 0.0)
        inv = eye + a_ab
        pw = a_ab
        for _ in range(levels):
            pw = _fdot(pw, pw)
            inv = inv + _fdot(inv, pw)
        u = _fdot(inv, _fdot(at, s0, _NT) + _fdot(a_ak, v))
        y = _fdot(rt, s0, _NT) + _fdot(a_rb, u) + _fdot(a_rk, v)
        s_ref[h] = (s0 + _fdot(u, bt, _TN) + _fdot(v, kt, _TN)) * w_in[c - 1:c, :]
        mu = jnp.mean(y, axis=-1, keepdims=True)
        yc = y - mu
        var = jnp.mean(yc * yc, axis=-1, keepdims=True)
        y_ref[0, :, h * HEAD_DIM:(h + 1) * HEAD_DIM] = yc * lax.rsqrt(var + LNX_EPS)

    @pl.when(step == pl.num_programs(1) - 1)
    def _():
        s_out[0] = s_ref[...]


def _wkv_scan(r, lw, k, v, a, b, s0, chunk):
    bsz, h, t, n = r.shape
    seq = pl.BlockSpec((1, h, chunk, n), lambda i, j: (i, 0, j, 0))
    state = pl.BlockSpec((1, h, n, n), lambda i, j: (i, 0, 0, 0))
    return pl.pallas_call(
        functools.partial(_wkv_kernel, chunk=chunk),
        grid=(bsz, t // chunk),
        in_specs=[seq] * 6 + [state],
        out_specs=[pl.BlockSpec((1, chunk, h * n), lambda i, j: (i, j, 0)), state],
        out_shape=[SDS((bsz, t, h * n), f32), SDS((bsz, h, n, n), f32)],
        scratch_shapes=[pltpu.VMEM((h, n, n), f32)],
        compiler_params=_cparams(("arbitrary", "arbitrary")),
        name="wkv_scan",
    )(r, lw, k, v, a, b, s0)


def _outproj_kernel(attn_ref, yn_ref, bonus_ref, g_ref, x_ref, gt1_ref, sh2_ref, sc2_ref,
                    gattn_ref, lnxg_ref, lnxb_ref, wout_ref, gpost_ref, gpre_ref, rw_ref, rb_ref,
                    x1_ref, h2_ref, logit_ref):
    nb, tt, d = x_ref.shape
    o_a = _rms(attn_ref[...], gattn_ref[...])
    o_b = (yn_ref[...] * lnxg_ref[...] + lnxb_ref[...] + bonus_ref[...]) * g_ref[...]
    cat = jnp.concatenate([o_a, o_b], axis=-1).reshape(nb * tt, d)
    mixed = _bdot(cat, wout_ref[...]).reshape(nb, tt, d)
    x1 = x_ref[...] + gt1_ref[...] * _rms(mixed, gpost_ref[...])
    x1_ref[...] = x1
    h2 = _rms(x1, gpre_ref[...]) * (1.0 + sc2_ref[...]) + sh2_ref[...]
    h2_ref[...] = h2.astype(bf16)
    logits = _fdot(h2.reshape(nb * tt, d), rw_ref[...]) + rb_ref[...]
    logit_ref[...] = logits.reshape(nb, tt, N_EXPERTS)


def _out_proj(attn, yn, bonus, g, x, gt1, sh2, sc2, p, w_out_bf16, nb, tt):
    b, t, d = x.shape
    row = lambda w: pl.BlockSpec((nb, tt, w), lambda i, j: (i, j, 0))
    mod = pl.BlockSpec((nb, 1, d), lambda i, j: (i, 0, 0))
    vec = lambda a: a.reshape(1, -1)
    params = [vec(p['g_attn_out']), vec(p['lnx_g']), vec(p['lnx_b']), w_out_bf16,
              vec(p['g_mix_post']), vec(p['g_ffn_pre']), p['router_w'], vec(p['router_b'])]
    return pl.pallas_call(
        _outproj_kernel,
        grid=(b // nb, t // tt),
        in_specs=[row(W_A), row(W_B), row(W_B), row(W_B), row(d), mod, mod, mod]
                 + [pl.BlockSpec(a.shape, lambda i, j: (0, 0)) for a in params],
        out_specs=[row(d), row(d), row(N_EXPERTS)],
        out_shape=[SDS((b, t, d), f32), SDS((b, t, d), bf16), SDS((b, t, N_EXPERTS), f32)],
        compiler_params=_cparams(("arbitrary", "arbitrary")),
        name="out_proj",
    )(attn, yn, bonus, g, x, gt1, sh2, sc2, *params)


def _expert_kernel(be_ref, nused_ref, x_ref, gate_ref, w1_ref, b1_ref, w2_ref, b2_ref, y_ref,
                   w1b_ref, w2b_ref):
    i = pl.program_id(0)
    prev = be_ref[jnp.maximum(i - 1, 0)]

    @pl.when((i == 0) | (be_ref[i] != prev))
    def _():
        w1b_ref[...] = w1_ref[0].astype(bf16)
        w2b_ref[...] = w2_ref[0].astype(bf16)

    @pl.when(i < nused_ref[0])
    def _():
        u = jnp.dot(x_ref[...], w1b_ref[...], preferred_element_type=f32) + b1_ref[0]
        u_glu = jnp.minimum(u[:, :D_FF], SWIGLU_LIMIT)
        u_lin = jnp.clip(u[:, D_FF:], -SWIGLU_LIMIT, SWIGLU_LIMIT)
        act = u_glu * _sigmoid(SWIGLU_ALPHA * u_glu) * (u_lin + 1.0)
        y = jnp.dot(act.astype(bf16), w2b_ref[...], preferred_element_type=f32) + b2_ref[0]
        y_ref[...] = y * gate_ref[...]

    @pl.when(i >= nused_ref[0])
    def _():
        y_ref[...] = jnp.zeros_like(y_ref)


def _expert_ffn(xs, gates, blk_expert, n_used, e_w1, e_b1, e_w2, e_b2, tm):
    n_slots, d = xs.shape
    n_blocks = n_slots // tm
    ex = lambda i, be, nu: (be[i], 0, 0)
    grid_spec = pltpu.PrefetchScalarGridSpec(
        num_scalar_prefetch=2,
        grid=(n_blocks,),
        in_specs=[pl.BlockSpec((tm, d), lambda i, be, nu: (i, 0)),
                  pl.BlockSpec((tm, 1), lambda i, be, nu: (i, 0)),
                  pl.BlockSpec((1, d, 2 * D_FF), ex),
                  pl.BlockSpec((1, 1, 2 * D_FF), ex),
                  pl.BlockSpec((1, D_FF, d), ex),
                  pl.BlockSpec((1, 1, d), ex)],
        out_specs=pl.BlockSpec((tm, d), lambda i, be, nu: (i, 0)),
        scratch_shapes=[pltpu.VMEM((d, 2 * D_FF), bf16), pltpu.VMEM((D_FF, d), bf16)],
    )
    return pl.pallas_call(
        _expert_kernel,
        grid_spec=grid_spec,
        out_shape=SDS((n_slots, d), f32),
        compiler_params=_cparams(("arbitrary",)),
        name="expert_ffn",
    )(blk_expert, n_used, xs, gates, e_w1, e_b1.reshape(N_EXPERTS, 1, -1), e_w2,
      e_b2.reshape(N_EXPERTS, 1, -1))


def _moe(h2, logits, e_w1, e_b1, e_w2, e_b2, tm):
    m = h2.shape[0]
    top_val, top_idx = lax.top_k(logits, TOP_K)
    gate = jax.nn.softmax(top_val, axis=-1)
    n_rows = m * TOP_K
    flat_e = top_idx.reshape(-1)
    order = jnp.argsort(flat_e)
    e_sorted = flat_e[order]
    counts = jnp.bincount(flat_e, length=N_EXPERTS)
    padded = (counts + tm - 1) // tm * tm
    pad_end = jnp.cumsum(padded)
    pad_start = pad_end - padded
    grp_start = jnp.cumsum(counts) - counts
    dest = (pad_start[e_sorted] + jnp.arange(n_rows) - grp_start[e_sorted]).astype(i32)
    n_blocks = (n_rows + N_EXPERTS * (tm - 1) + tm - 1) // tm
    n_slots = n_blocks * tm
    slot_tok = jnp.full((n_slots,), m, i32).at[dest].set((order // TOP_K).astype(i32))
    slot_gate = jnp.zeros((n_slots,), f32).at[dest].set(gate.reshape(-1)[order])
    blk_expert = jnp.minimum(jnp.searchsorted(pad_end, jnp.arange(n_blocks) * tm, side='right'),
                             N_EXPERTS - 1).astype(i32)
    n_used = (pad_end[-1] // tm).astype(i32).reshape(1)
    xs = h2.at[slot_tok].get(mode='fill', fill_value=0)
    ys = _expert_ffn(xs, slot_gate.reshape(n_slots, 1), blk_expert, n_used, e_w1, e_b1, e_w2, e_b2, tm)
    inv = jnp.zeros((n_rows,), i32).at[order].set(dest)
    return ys[inv].reshape(m, TOP_K, -1).sum(axis=1)


def _final_kernel(x1_ref, f_ref, gt2_ref, g_ref, o_ref):
    o_ref[...] = x1_ref[...] + gt2_ref[...] * _rms(f_ref[...], g_ref[...])


def _final(x1, f, gt2, g, nb, tt):
    b, t, d = x1.shape
    row = pl.BlockSpec((nb, tt, d), lambda i, j: (i, j, 0))
    return pl.pallas_call(
        _final_kernel,
        grid=(b // nb, t // tt),
        in_specs=[row, row, pl.BlockSpec((nb, 1, d), lambda i, j: (i, 0, 0)),
                  pl.BlockSpec((1, d), lambda i, j: (0, 0))],
        out_specs=row,
        out_shape=SDS((b, t, d), f32),
        compiler_params=_cparams(("arbitrary", "arbitrary")),
        name="final_residual",
    )(x1, f, gt2, g.reshape(1, d))


def _mix_layer(x, mod, attend, shift_prev, wkv_prev, p, w_in_bf16, w_out_bf16, nb, tt, t_real, chunk):
    b, t, d = x.shape
    sh1, sc1, gt1, sh2, sc2, gt2 = [mod[:, i:i + 1, :] for i in range(6)]
    q, k, v, pb = _in_proj(x, sh1, sc1, p['g_mix_pre'], w_in_bf16, nb, tt)
    attn = attend(q, k, v)
    last = pb[:, tt - 1::tt]
    prev_first = jnp.concatenate([shift_prev[:, None], last[:, :-1]], axis=1)[:, :, None, :]
    r, lw, k2, v2, a, kb, g, bonus = _rwkv_prep(pb, prev_first, p, nb, tt, t_real)
    yn, wkv_new = _wkv_scan(r, lw, k2, v2, a, kb, wkv_prev, chunk)
    x1, h2, logits = _out_proj(attn, yn, bonus, g, x, gt1, sh2, sc2, p, w_out_bf16, nb, tt)
    return x1, h2, logits, gt2, k, v, wkv_new, pb[:, t_real - 1]


def kernel(x_prompt, x_sample, cache_k, cache_v, state_wkv, state_shift, page_table, c_prompt, c_sample,
           w_ada, b_ada, g_mix_pre, g_mix_post, g_ffn_pre, g_ffn_post, w_in, mu_shift,
           decay_base, decay_up, iclr_base, iclr_up, gate_up, k_k, k_a, r_k, lnx_g, lnx_b,
           g_attn_out, sb_bias, w_out, router_w, router_b, e_w1, e_b1, e_w2, e_b2):
    weights = dict(w_ada=w_ada, b_ada=b_ada, g_mix_pre=g_mix_pre, g_mix_post=g_mix_post,
                   g_ffn_pre=g_ffn_pre, g_ffn_post=g_ffn_post, w_in=w_in, mu_shift=mu_shift,
                   decay_base=decay_base, decay_up=decay_up, iclr_base=iclr_base, iclr_up=iclr_up,
                   gate_up=gate_up, k_k=k_k, k_a=k_a, r_k=r_k, lnx_g=lnx_g, lnx_b=lnx_b,
                   g_attn_out=g_attn_out, sb_bias=sb_bias, w_out=w_out, router_w=router_w,
                   router_b=router_b, e_w1=e_w1, e_b1=e_b1, e_w2=e_w2, e_b2=e_b2)
    depth = w_ada.shape[0]
    bp, tp, d = x_prompt.shape
    bs, ts, _ = x_sample.shape
    ts_pad = -(-ts // V7X_SUBLANES) * V7X_SUBLANES
    hp = x_prompt
    hs = jnp.pad(x_sample, ((0, 0), (0, ts_pad - ts), (0, 0)))
    outs = [[] for _ in range(8)]
    for l in range(depth):
        p = {name: w[l] for name, w in weights.items()}
        w_in_bf16 = p['w_in'].astype(bf16)
        w_out_bf16 = p['w_out'].astype(bf16)
        mod = _ada_mod(jnp.concatenate([c_prompt, c_sample], axis=0), p['w_ada'], p['b_ada'])
        mod = mod.reshape(bp + bs, 6, d)

        attend_p = lambda q, k, v: _attn_prompt(q, k, v, p['sb_bias'])
        x1p, h2p, lgp, gt2p, kp, vp, wp, shp = _mix_layer(
            hp, mod[:bp], attend_p, jnp.zeros((bp, SHIFT_W), f32),
            jnp.zeros((bp, H_B, HEAD_DIM, HEAD_DIM), f32), p, w_in_bf16, w_out_bf16,
            nb=1, tt=ROW_TILE, t_real=tp, chunk=WKV_CHUNK)

        ck = cache_k[l].reshape(-1, PAGE_SIZE, W_A)
        cv = cache_v[l].reshape(-1, PAGE_SIZE, W_A)
        attend_s = lambda q, k, v: _attn_sample(q, k, v, ck, cv, page_table, p['sb_bias'], ts)
        x1s, h2s, lgs, gt2s, ks, vs, ws, shs = _mix_layer(
            hs, mod[bp:], attend_s, state_shift[l], state_wkv[l], p, w_in_bf16, w_out_bf16,
            nb=bs, tt=ts_pad, t_real=ts, chunk=ts_pad)

        mp = bp * tp
        h2_all = jnp.concatenate([h2p.reshape(mp, d), h2s[:, :ts].reshape(bs * ts, d)], axis=0)
        lg_all = jnp.concatenate([lgp.reshape(mp, N_EXPERTS), lgs[:, :ts].reshape(bs * ts, N_EXPERTS)], axis=0)
        f_all = _moe(h2_all, lg_all, p['e_w1'], p['e_b1'], p['e_w2'], p['e_b2'], MOE_TILE)
        fp = f_all[:mp].reshape(bp, tp, d)
        fs = jnp.pad(f_all[mp:].reshape(bs, ts, d), ((0, 0), (0, ts_pad - ts), (0, 0)))
        hp = _final(x1p, fp, gt2p, p['g_ffn_post'], 1, ROW_TILE)
        hs = _final(x1s, fs, gt2s, p['g_ffn_post'], bs, ts_pad)

        for lst, val in zip(outs, (kp.reshape(bp, tp, H_A, HEAD_DIM), vp.reshape(bp, tp, H_A, HEAD_DIM),
                                   ks[:, :ts].reshape(bs, ts, H_A, HEAD_DIM),
                                   vs[:, :ts].reshape(bs, ts, H_A, HEAD_DIM), wp, ws, shp, shs)):
            lst.append(val)
    return (hp, hs[:, :ts]) + tuple(jnp.stack(lst) for lst in outs)
```

```python
import functools

import jax
import jax.numpy as jnp
from jax import lax
from jax.experimental import pallas as pl
from jax.experimental.pallas import tpu as pltpu

f32 = jnp.float32
bf16 = jnp.bfloat16
i32 = jnp.int32
SDS = jax.ShapeDtypeStruct

D_MODEL = 1024
HEAD_DIM = 64
W_A = 512
W_B = 512
H_A = W_A // HEAD_DIM
H_B = W_B // HEAD_DIM
DECAY_LORA = 64
ICLR_LORA = 64
GATE_LORA = 128
SHIFT_W = 3 * W_B + DECAY_LORA + ICLR_LORA + GATE_LORA
N_EXPERTS = 32
TOP_K = 4
D_FF = D_MODEL
SWIGLU_ALPHA = 1.702
SWIGLU_LIMIT = 7.0
PAGE_SIZE = 128
RMS_EPS = 1e-6
LNX_EPS = 64e-5
LOG2E = 1.4426950408889634

V7X_SUBLANES = 8
V7X_LANES = 128
VMEM_LIMIT = 56 * 1024 * 1024

ROW_TILE = 256
WKV_CHUNK = 64
ATTN_TQ = 512
ATTN_TK = 128
ATTN_NSUB = 4
PAGES_PER_STEP = 8
MOE_TILE = 256


def _cparams(sem):
    return pltpu.CompilerParams(dimension_semantics=sem, vmem_limit_bytes=VMEM_LIMIT)


def _bdot(a, b):
    return jnp.dot(a.astype(bf16), b.astype(bf16), preferred_element_type=f32)


def _fdot(a, b, dims=(((1,), (0,)), ((), ()))):
    return lax.dot_general(a, b, dims, precision=lax.Precision.HIGHEST, preferred_element_type=f32)


_NT = (((1,), (1,)), ((), ()))
_TN = (((0,), (0,)), ((), ()))


def _split_dot(x, m_bf16):
    hi = x.astype(bf16)
    lo = (x - hi.astype(f32)).astype(bf16)
    return (jnp.dot(hi, m_bf16, preferred_element_type=f32)
            + jnp.dot(lo, m_bf16, preferred_element_type=f32))


def _sigmoid(x):
    return 1.0 / (1.0 + jnp.exp(-x))


def _softplus(x):
    return jnp.maximum(x, 0.0) + jnp.log(1.0 + jnp.exp(-jnp.abs(x)))


def _softplus2(z):
    neg_abs = pltpu.bitcast(pltpu.bitcast(z, jnp.uint32) | jnp.uint32(0x80000000), f32)
    return jnp.maximum(z, 0.0) + jnp.log2(1.0 + jnp.exp2(neg_abs))


def _split2(x):
    hi = x.astype(bf16)
    return hi, (x - hi.astype(f32)).astype(bf16)


def _split3(x):
    hi = x.astype(bf16)
    r = x - hi.astype(f32)
    mid = r.astype(bf16)
    return hi, mid, (r - mid.astype(f32)).astype(bf16)


def _dg(a, b, dims):
    return lax.dot_general(a, b, dims, preferred_element_type=f32)


_NN = (((1,), (0,)), ((), ()))


def _dot3(a, b, dims=_NN):
    ah, al = _split2(a)
    bh, bl = _split2(b)
    return _dg(ah, bh, dims) + _dg(ah, bl, dims) + _dg(al, bh, dims)


def _dot_exact_rhs(x3, m_bf16, dims=_NN):
    return _dg(x3[0], m_bf16, dims) + _dg(x3[1], m_bf16, dims) + _dg(x3[2], m_bf16, dims)


def _rms(x, g):
    return x * lax.rsqrt(jnp.mean(x * x, axis=-1, keepdims=True) + RMS_EPS) * g


def _ada_kernel(c_ref, w_ref, b_ref, o_ref):
    c = c_ref[...]
    o_ref[...] = _bdot(c * _sigmoid(c), w_ref[...]) + b_ref[...]


def _ada_mod(c, w_ada, b_ada):
    n, d = c.shape
    nout = w_ada.shape[1]
    tn = 1536
    return pl.pallas_call(
        _ada_kernel,
        grid=(nout // tn,),
        in_specs=[pl.BlockSpec((n, d), lambda j: (0, 0)),
                  pl.BlockSpec((d, tn), lambda j: (0, j)),
                  pl.BlockSpec((1, tn), lambda j: (0, j))],
        out_specs=pl.BlockSpec((n, tn), lambda j: (0, j)),
        out_shape=SDS((n, nout), f32),
        compiler_params=_cparams(("arbitrary",)),
        name="ada_mod",
    )(c, w_ada, b_ada.reshape(1, nout))


def _inproj_kernel(x_ref, sh_ref, sc_ref, g_ref, w_ref, q_ref, k_ref, v_ref, pb_ref):
    nb, tt, d = x_ref.shape
    h = _rms(x_ref[...], g_ref[...]) * (1.0 + sc_ref[...]) + sh_ref[...]
    hb = h.reshape(nb * tt, d).astype(bf16)
    q_ref[...] = jnp.dot(hb, w_ref[:, 0:W_A], preferred_element_type=f32).reshape(nb, tt, W_A)
    k_ref[...] = jnp.dot(hb, w_ref[:, W_A:2 * W_A], preferred_element_type=f32).reshape(nb, tt, W_A)
    v_ref[...] = jnp.dot(hb, w_ref[:, 2 * W_A:3 * W_A], preferred_element_type=f32).reshape(nb, tt, W_A)
    pb_ref[...] = jnp.dot(hb, w_ref[:, 3 * W_A:], preferred_element_type=f32).reshape(nb, tt, SHIFT_W)


def _in_proj(x, sh1, sc1, g, w_in_bf16, nb, tt):
    b, t, d = x.shape
    row = lambda w: pl.BlockSpec((nb, tt, w), lambda i, j: (i, j, 0))
    mod = pl.BlockSpec((nb, 1, d), lambda i, j: (i, 0, 0))
    return pl.pallas_call(
        _inproj_kernel,
        grid=(b // nb, t // tt),
        in_specs=[row(d), mod, mod,
                  pl.BlockSpec((1, d), lambda i, j: (0, 0)),
                  pl.BlockSpec(w_in_bf16.shape, lambda i, j: (0, 0))],
        out_specs=[row(W_A), row(W_A), row(W_A), row(SHIFT_W)],
        out_shape=[SDS((b, t, W_A), f32)] * 3 + [SDS((b, t, SHIFT_W), f32)],
        compiler_params=_cparams(("arbitrary", "arbitrary")),
        name="in_proj",
    )(x, sh1, sc1, g.reshape(1, d), w_in_bf16)


def _attn_prompt_kernel(bias_ref, q_ref, k_ref, v_ref, u2_ref, o_ref, kbd_ref, vbd_ref, acc_ref,
                        *, tq, tk, nsub):
    hp = pl.program_id(1)
    qi = pl.program_id(2)
    two = 2 * tk
    big = tk * nsub
    ratio = tq // big
    n_tiles = k_ref.shape[1] // tk

    @pl.when(qi == 0)
    def _():
        first = lax.broadcasted_iota(i32, (tk, 2 * HEAD_DIM), 1) < HEAD_DIM

        def build(j, _):
            start = pl.multiple_of(j * tk, tk)
            kt = k_ref[0, pl.ds(start, tk), :]
            vt = v_ref[0, pl.ds(start, tk), :]
            kbd_ref[j, 0:tk, :] = jnp.where(first, kt, 0.0).astype(bf16)
            kbd_ref[j, tk:two, :] = jnp.where(first, 0.0, kt).astype(bf16)
            vbd_ref[j, 0:tk, :] = jnp.where(first, vt, 0.0).astype(bf16)
            vbd_ref[j, tk:two, :] = jnp.where(first, 0.0, vt).astype(bf16)
            return 0

        lax.fori_loop(0, n_tiles, build, 0)

    qb = (q_ref[0] * (LOG2E * HEAD_DIM ** -0.5)).astype(bf16)
    lane2 = lax.broadcasted_iota(i32, (1, two), 1)
    bias2 = jnp.where(lane2 < tk, bias_ref[2 * hp], bias_ref[2 * hp + 1]) * LOG2E
    bias_row = jnp.concatenate([bias2] * nsub, axis=1)
    u2 = u2_ref[...]
    acc_ref[...] = jnp.zeros_like(acc_ref)
    row = lax.broadcasted_iota(i32, (tq, nsub * two), 0)
    col = lax.broadcasted_iota(i32, (tq, nsub * two), 1)
    key_off = (col // two) * tk + col % tk

    def chunk(tile0, mask, carry):
        kb = kbd_ref[pl.ds(tile0, nsub)].reshape(nsub * two, 2 * HEAD_DIM)
        vb = vbd_ref[pl.ds(tile0, nsub)].reshape(nsub * two, 2 * HEAD_DIM)
        z = _dg(qb, kb, _NT) + bias_row
        sp = _softplus2(z)
        if mask is not None:
            sp = jnp.where(mask, sp, 0.0)
        spb = sp.astype(bf16)
        ws = [None] * nsub
        for j in range(nsub - 1, -1, -1):
            incl = jnp.dot(spb[:, j * two:(j + 1) * two], u2, preferred_element_type=f32)
            ws[j] = jnp.exp2(z[:, j * two:(j + 1) * two] - incl - carry)
            carry = carry + jnp.concatenate([jnp.broadcast_to(incl[:, 0:1], (tq, tk)),
                                             jnp.broadcast_to(incl[:, tk:tk + 1], (tq, tk))], axis=1)
        w = jnp.concatenate(ws, axis=1)
        if mask is not None:
            w = jnp.where(mask, w, 0.0)
        acc_ref[...] += jnp.dot(w.astype(bf16), vb, preferred_element_type=f32)
        return carry

    carry = jnp.zeros((tq, two), f32)
    for dgl in range(ratio - 1, -1, -1):
        carry = chunk(qi * (tq // tk) + dgl * nsub, (key_off + dgl * big) < row, carry)

    def body(n, carry):
        return chunk((qi * ratio - 1 - n) * nsub, None, carry)

    lax.fori_loop(0, qi * ratio, body, carry)
    o_ref[0] = acc_ref[...]


def _attn_prompt(q, k, v, sb_bias):
    b, t, _ = q.shape
    tq, tk, nsub = ATTN_TQ, ATTN_TK, ATTN_NSUB
    j = lax.broadcasted_iota(i32, (2 * tk, 2 * tk), 0)
    s = lax.broadcasted_iota(i32, (2 * tk, 2 * tk), 1)
    u2 = ((j >= s) & ((j // tk) == (s // tk))).astype(bf16)
    seq = pl.BlockSpec((1, t, 2 * HEAD_DIM), lambda bi, hp, qi: (bi, 0, hp))
    return pl.pallas_call(
        functools.partial(_attn_prompt_kernel, tq=tq, tk=tk, nsub=nsub),
        grid=(b, H_A // 2, t // tq),
        in_specs=[pl.BlockSpec(memory_space=pltpu.SMEM),
                  pl.BlockSpec((1, tq, 2 * HEAD_DIM), lambda bi, hp, qi: (bi, qi, hp)),
                  seq, seq,
                  pl.BlockSpec((2 * tk, 2 * tk), lambda bi, hp, qi: (0, 0))],
        out_specs=pl.BlockSpec((1, tq, 2 * HEAD_DIM), lambda bi, hp, qi: (bi, qi, hp)),
        out_shape=SDS((b, t, W_A), f32),
        scratch_shapes=[pltpu.VMEM((t // tk, 2 * tk, 2 * HEAD_DIM), bf16),
                        pltpu.VMEM((t // tk, 2 * tk, 2 * HEAD_DIM), bf16),
                        pltpu.VMEM((tq, 2 * HEAD_DIM), f32)],
        compiler_params=_cparams(("arbitrary", "arbitrary", "arbitrary")),
        name="attn_prompt",
    )(sb_bias, q, k, v, u2)


def _attn_sample_kernel(pt_ref, bias_ref, qrow_ref, knew_ref, vnew_ref, uo_ref, *rest, n_new, pages):
    kp_refs = rest[:pages]
    vp_refs = rest[pages:2 * pages]
    o_ref, acc_ref, carry_ref = rest[2 * pages:]
    step = pl.program_id(1)
    n_row = H_A * n_new
    row_head = lax.broadcasted_iota(i32, (n_row, PAGE_SIZE), 0) // n_new
    bias = jnp.broadcast_to(bias_ref[...], (n_row, PAGE_SIZE))

    def add_values(w, value_of_head):
        for h in range(H_A):
            acc_ref[...] += jnp.dot(jnp.where(row_head[:, :w.shape[1]] == h, w, 0.0).astype(bf16),
                                    value_of_head(h), preferred_element_type=f32)

    @pl.when(step == 0)
    def _():
        pad = knew_ref.shape[1]
        knew = knew_ref[0].astype(bf16)
        vnew = vnew_ref[0].astype(bf16)
        head_cols = lambda x, h: x[:, h * HEAD_DIM:(h + 1) * HEAD_DIM]
        z = sum(_dg(qrow_ref[0, h], head_cols(knew, h), _NT) for h in range(H_A)) + bias[:, :pad]
        s_idx = lax.broadcasted_iota(i32, (n_row, pad), 1)
        t_idx = lax.broadcasted_iota(i32, (n_row, pad), 0) % n_new
        mask = s_idx < t_idx
        sp = jnp.where(mask, _softplus2(z), 0.0)
        incl = jnp.zeros_like(sp)
        for j in range(n_new):
            incl = incl + jnp.where(s_idx <= j, sp[:, j:j + 1], 0.0)
        w = jnp.where(mask, jnp.exp2(z - incl), 0.0)
        acc_ref[...] = jnp.zeros_like(acc_ref)
        add_values(w, lambda h: head_cols(vnew, h))
        carry_ref[...] = jnp.broadcast_to(incl[:, 0:1], carry_ref.shape)

    uo = uo_ref[...]
    head_rows = lambda ref, h: ref[0, 0, pl.ds(h, PAGE_SIZE, stride=H_A), :].astype(bf16)
    z = [sum(_dg(qrow_ref[0, h], head_rows(kp_refs[i], h), _NT) for h in range(H_A)) + bias
         for i in range(pages)]
    sp = [_softplus2(z[i]) for i in range(pages)]
    cr = [jnp.dot(sp[i].astype(bf16), uo, preferred_element_type=f32) for i in range(pages)]
    carry = carry_ref[...]
    for i in range(pages):
        w = jnp.exp2(z[i] - cr[i][:, :PAGE_SIZE] - carry)
        add_values(w, lambda h, i=i: head_rows(vp_refs[i], h))
        carry = carry + cr[i][:, PAGE_SIZE:]
    carry_ref[...] = carry

    @pl.when(step == pl.num_programs(1) - 1)
    def _():
        o_ref[0] = acc_ref[...]


def _attn_sample(q, k_new, v_new, cache_k, cache_v, layer, page_table, sb_bias, n_new):
    b, pad, _ = q.shape
    n_pages = page_table.shape[1]
    pages = PAGES_PER_STEP
    n_row = H_A * n_new
    scale = LOG2E * HEAD_DIM ** -0.5
    qh = jnp.transpose(q[:, :n_new].reshape(b, n_new, H_A, HEAD_DIM) * scale, (0, 2, 1, 3))
    eye = jnp.eye(H_A, dtype=f32)
    qrow = (qh[:, :, None, :, :] * eye[None, :, :, None, None]).reshape(b, H_A, n_row, HEAD_DIM).astype(bf16)
    bias = jnp.repeat(sb_bias * LOG2E, n_new).reshape(n_row, 1)
    j = lax.broadcasted_iota(i32, (PAGE_SIZE, PAGE_SIZE), 0)
    s = lax.broadcasted_iota(i32, (PAGE_SIZE, PAGE_SIZE), 1)
    uo = jnp.concatenate([(j >= s).astype(bf16), jnp.ones((PAGE_SIZE, PAGE_SIZE), bf16)], axis=1)
    n_layers, n_phys = cache_k.shape[:2]
    ck = cache_k.reshape(n_layers, n_phys, PAGE_SIZE * H_A, HEAD_DIM)
    cv = cache_v.reshape(n_layers, n_phys, PAGE_SIZE * H_A, HEAD_DIM)

    def page_spec(i):
        return pl.BlockSpec(
            (1, 1, PAGE_SIZE * H_A, HEAD_DIM),
            lambda bi, st, pt, i=i: (layer, pt[bi, n_pages - 1 - (st * pages + i)], 0, 0))

    new_spec = pl.BlockSpec((1, pad, W_A), lambda bi, st, pt: (bi, 0, 0))
    grid_spec = pltpu.PrefetchScalarGridSpec(
        num_scalar_prefetch=1,
        grid=(b, n_pages // pages),
        in_specs=[pl.BlockSpec((n_row, 1), lambda bi, st, pt: (0, 0)),
                  pl.BlockSpec((1, H_A, n_row, HEAD_DIM), lambda bi, st, pt: (bi, 0, 0, 0)),
                  new_spec, new_spec,
                  pl.BlockSpec((PAGE_SIZE, 2 * PAGE_SIZE), lambda bi, st, pt: (0, 0))]
                 + [page_spec(i) for i in range(pages)] * 2,
        out_specs=pl.BlockSpec((1, n_row, HEAD_DIM), lambda bi, st, pt: (bi, 0, 0)),
        scratch_shapes=[pltpu.VMEM((n_row, HEAD_DIM), f32), pltpu.VMEM((n_row, PAGE_SIZE), f32)],
    )
    out = pl.pallas_call(
        functools.partial(_attn_sample_kernel, n_new=n_new, pages=pages),
        grid_spec=grid_spec,
        out_shape=SDS((b, n_row, HEAD_DIM), f32),
        compiler_params=_cparams(("arbitrary", "arbitrary")),
        name="attn_sample",
    )(page_table, bias, qrow, k_new, v_new, uo, *([ck] * pages), *([cv] * pages))
    return jnp.transpose(out.reshape(b, H_A, n_new, HEAD_DIM), (0, 2, 1, 3)).reshape(b, n_new, W_A)


def _prep_kernel(pb_ref, pf_ref, mu_ref, dbase_ref, dup_ref, ibase_ref, iup_ref, gup_ref,
                 kk_ref, ka_ref, rk_ref, hsum_ref,
                 r_out, lw_out, k_out, v_out, a_out, b_out, g_out, bonus_out, *, t_real):
    nb, tt, w = pb_ref.shape
    pb = pb_ref[...]
    tpos = lax.broadcasted_iota(i32, pb.shape, 1)
    prev = jnp.where(tpos == 0, pf_ref[:, 0], pltpu.roll(pb, 1, axis=1))
    x = (pb + (prev - pb) * mu_ref[...]).reshape(nb * tt, w)
    r = x[:, 0:W_B]
    k = x[:, W_B:2 * W_B]
    v = x[:, 2 * W_B:3 * W_B]
    o = 3 * W_B
    xw = x[:, o:o + DECAY_LORA]
    xa = x[:, o + DECAY_LORA:o + DECAY_LORA + ICLR_LORA]
    xg = x[:, o + DECAY_LORA + ICLR_LORA:]
    log_w = -_softplus(-(dbase_ref[...] + _bdot(jnp.tanh(xw), dup_ref[...]))) - 0.5
    lw = -jnp.exp(log_w)
    a = _sigmoid(ibase_ref[...] + _bdot(xa, iup_ref[...]))
    g = _bdot(_sigmoid(xg), gup_ref[...])
    hsum = hsum_ref[...]
    kkf = k * kk_ref[...]
    kk = kkf / jnp.maximum(jnp.sqrt(_split_dot(kkf * kkf, hsum)), 1e-12)
    k2 = k * (1.0 + (a - 1.0) * ka_ref[...])
    bonus = _split_dot(r * k2 * rk_ref[...], hsum) * v
    na = -kk
    kb = kk * a
    if t_real < tt:
        valid = (lax.broadcasted_iota(i32, (nb, tt, W_B), 1) < t_real).reshape(nb * tt, W_B)
        zero = lambda u: jnp.where(valid, u, 0.0)
        r, lw, k2, v, na, kb = zero(r), zero(lw), zero(k2), zero(v), zero(na), zero(kb)
    g_out[...] = g.reshape(nb, tt, W_B)
    bonus_out[...] = bonus.reshape(nb, tt, W_B)
    for val, ref in ((r, r_out), (lw, lw_out), (k2, k_out), (v, v_out), (na, a_out), (kb, b_out)):
        val = val.reshape(nb, tt, W_B)
        for h in range(H_B):
            ref[:, h, :, :] = val[:, :, h * HEAD_DIM:(h + 1) * HEAD_DIM]


def _rwkv_prep(pb, prev_first, p, nb, tt, t_real):
    b, t, w = pb.shape
    row = lambda i, j: (i, j, 0)
    const = lambda i, j: (0, 0)
    vec = lambda a: a.reshape(1, -1)
    head_of = jnp.arange(W_B) // HEAD_DIM
    hsum = (head_of[:, None] == head_of[None, :]).astype(bf16)
    params = [vec(p['mu_shift']), vec(p['decay_base']), p['decay_up'], vec(p['iclr_base']), p['iclr_up'],
              p['gate_up'], vec(p['k_k']), vec(p['k_a']), vec(p['r_k']), hsum]
    heads = pl.BlockSpec((nb, H_B, tt, HEAD_DIM), lambda i, j: (i, 0, j, 0))
    return pl.pallas_call(
        functools.partial(_prep_kernel, t_real=t_real),
        grid=(b // nb, t // tt),
        in_specs=[pl.BlockSpec((nb, tt, w), row),
                  pl.BlockSpec((nb, 1, 1, w), lambda i, j: (i, j, 0, 0))]
                 + [pl.BlockSpec(a.shape, const) for a in params],
        out_specs=[heads] * 6 + [pl.BlockSpec((nb, tt, W_B), row)] * 2,
        out_shape=[SDS((b, H_B, t, HEAD_DIM), f32)] * 6 + [SDS((b, t, W_B), f32)] * 2,
        compiler_params=_cparams(("arbitrary", "arbitrary")),
        name="rwkv_prep",
    )(pb, prev_first, *params)


def _wkv_kernel(r_ref, lw_ref, k_ref, v_ref, a_ref, b_ref, s0_ref, y_ref, s_out, st_ref, *, chunk):
    c = chunk
    n = HEAD_DIM
    step = pl.program_id(1)

    @pl.when(step == 0)
    def _():
        for h in range(H_B):
            st_ref[h] = s0_ref[0, h].T

    ti = lax.broadcasted_iota(i32, (c, c), 0)
    si = lax.broadcasted_iota(i32, (c, c), 1)
    tri = (ti >= si).astype(bf16)
    eye = (ti == si).astype(f32)
    ones = jnp.ones((c, n), bf16)
    row2 = lax.broadcasted_iota(i32, (c, 2 * c), 0)
    col2 = lax.broadcasted_iota(i32, (c, 2 * c), 1)
    strict2 = (col2 % c) < row2
    incl2 = (col2 % c) <= row2
    right = col2 >= c
    levels = max(c.bit_length() - 2, 0)
    heads = range(H_B)
    each = lambda f: [f(h) for h in heads]
    lw = each(lambda h: lw_ref[0, h])
    lw3 = each(lambda h: _split3(lw[h]))
    cum = each(lambda h: sum(_dg(tri, t, _NN) for t in lw3[h]))
    wsum = each(lambda h: sum(_dg(t, ones, _TN) for t in lw3[h]))
    w_in = each(lambda h: jnp.exp(cum[h]))
    w_out = each(lambda h: jnp.exp(-cum[h]))
    at = each(lambda h: a_ref[0, h] * jnp.exp(cum[h] - lw[h]))
    rt = each(lambda h: r_ref[0, h] * w_in[h])
    bk = each(lambda h: jnp.concatenate([b_ref[0, h] * w_out[h], k_ref[0, h] * w_out[h]], axis=0))
    g = each(lambda h: _dot3(jnp.concatenate([at[h], rt[h]], axis=0), bk[h], _NT))
    top = each(lambda h: jnp.where(strict2, g[h][:c], 0.0))
    bot = each(lambda h: jnp.where(incl2, g[h][c:], 0.0))
    pw = each(lambda h: top[h][:, :c])
    inv = each(lambda h: eye + pw[h])
    for _ in range(levels):
        pw = each(lambda h: _dot3(pw[h], pw[h]))
        inv = each(lambda h: inv[h] + _dot3(inv[h], pw[h]))
    vv = each(lambda h: jnp.concatenate([v_ref[0, h], v_ref[0, h]], axis=0))
    xv = each(lambda h: _dot3(jnp.where(right, top[h], 0.0), vv[h]))
    x = each(lambda h: _dot3(at[h], st_ref[h]) + xv[h])
    u = each(lambda h: _dot3(inv[h], x[h]))
    uv = each(lambda h: jnp.concatenate([u[h], v_ref[0, h]], axis=0))
    y = each(lambda h: _bdot(bot[h], uv[h]) + _bdot(rt[h], st_ref[h]))
    st_new = each(lambda h: (st_ref[h] + _dot3(bk[h], uv[h], _TN)) * jnp.exp(wsum[h]))
    for h in heads:
        st_ref[h] = st_new[h]
        mu = jnp.mean(y[h], axis=-1, keepdims=True)
        yc = y[h] - mu
        var = jnp.mean(yc * yc, axis=-1, keepdims=True)
        y_ref[0, :, h * HEAD_DIM:(h + 1) * HEAD_DIM] = yc * lax.rsqrt(var + LNX_EPS)

    @pl.when(step == pl.num_programs(1) - 1)
    def _():
        for h in range(H_B):
            s_out[0, h] = st_ref[h].T


def _wkv_scan(r, lw, k, v, a, b, s0, chunk):
    bsz, h, t, n = r.shape
    seq = pl.BlockSpec((1, h, chunk, n), lambda i, j: (i, 0, j, 0))
    state = pl.BlockSpec((1, h, n, n), lambda i, j: (i, 0, 0, 0))
    return pl.pallas_call(
        functools.partial(_wkv_kernel, chunk=chunk),
        grid=(bsz, t // chunk),
        in_specs=[seq] * 6 + [state],
        out_specs=[pl.BlockSpec((1, chunk, h * n), lambda i, j: (i, j, 0)), state],
        out_shape=[SDS((bsz, t, h * n), f32), SDS((bsz, h, n, n), f32)],
        scratch_shapes=[pltpu.VMEM((h, n, n), f32)],
        compiler_params=_cparams(("arbitrary", "arbitrary")),
        name="wkv_scan",
    )(r, lw, k, v, a, b, s0)


def _outproj_kernel(attn_ref, yn_ref, bonus_ref, g_ref, x_ref, gt1_ref, sh2_ref, sc2_ref,
                    gattn_ref, lnxg_ref, lnxb_ref, wout_ref, gpost_ref, gpre_ref, rw_ref, rb_ref,
                    x1_ref, h2_ref, logit_ref):
    nb, tt, d = x_ref.shape
    o_a = _rms(attn_ref[...], gattn_ref[...])
    o_b = (yn_ref[...] * lnxg_ref[...] + lnxb_ref[...] + bonus_ref[...]) * g_ref[...]
    cat = jnp.concatenate([o_a, o_b], axis=-1).reshape(nb * tt, d)
    mixed = _bdot(cat, wout_ref[...]).reshape(nb, tt, d)
    x1 = x_ref[...] + gt1_ref[...] * _rms(mixed, gpost_ref[...])
    x1_ref[...] = x1
    h2 = _rms(x1, gpre_ref[...]) * (1.0 + sc2_ref[...]) + sh2_ref[...]
    h2_ref[...] = h2.astype(bf16)
    logits = _fdot(h2.reshape(nb * tt, d), rw_ref[...]) + rb_ref[...]
    logit_ref[...] = logits.reshape(nb, tt, N_EXPERTS)


def _out_proj(attn, yn, bonus, g, x, gt1, sh2, sc2, p, w_out_bf16, nb, tt):
    b, t, d = x.shape
    row = lambda w: pl.BlockSpec((nb, tt, w), lambda i, j: (i, j, 0))
    mod = pl.BlockSpec((nb, 1, d), lambda i, j: (i, 0, 0))
    vec = lambda a: a.reshape(1, -1)
    params = [vec(p['g_attn_out']), vec(p['lnx_g']), vec(p['lnx_b']), w_out_bf16,
              vec(p['g_mix_post']), vec(p['g_ffn_pre']), p['router_w'], vec(p['router_b'])]
    return pl.pallas_call(
        _outproj_kernel,
        grid=(b // nb, t // tt),
        in_specs=[row(W_A), row(W_B), row(W_B), row(W_B), row(d), mod, mod, mod]
                 + [pl.BlockSpec(a.shape, lambda i, j: (0, 0)) for a in params],
        out_specs=[row(d), row(d), row(N_EXPERTS)],
        out_shape=[SDS((b, t, d), f32), SDS((b, t, d), bf16), SDS((b, t, N_EXPERTS), f32)],
        compiler_params=_cparams(("arbitrary", "arbitrary")),
        name="out_proj",
    )(attn, yn, bonus, g, x, gt1, sh2, sc2, *params)


def _expert_kernel(be_ref, nused_ref, x_ref, gate_ref, w1_ref, b1_ref, w2_ref, b2_ref, y_ref,
                   w1b_ref, w2b_ref):
    i = pl.program_id(0)
    prev = be_ref[jnp.maximum(i - 1, 0)]

    @pl.when((i == 0) | (be_ref[i] != prev))
    def _():
        w1b_ref[...] = w1_ref[0].astype(bf16)
        w2b_ref[...] = w2_ref[0].astype(bf16)

    @pl.when(i < nused_ref[0])
    def _():
        u = jnp.dot(x_ref[...], w1b_ref[...], preferred_element_type=f32) + b1_ref[0]
        u_glu = jnp.minimum(u[:, :D_FF], SWIGLU_LIMIT)
        u_lin = jnp.clip(u[:, D_FF:], -SWIGLU_LIMIT, SWIGLU_LIMIT)
        act = u_glu * _sigmoid(SWIGLU_ALPHA * u_glu) * (u_lin + 1.0)
        y = jnp.dot(act.astype(bf16), w2b_ref[...], preferred_element_type=f32) + b2_ref[0]
        y_ref[...] = y * gate_ref[...]

    @pl.when(i >= nused_ref[0])
    def _():
        y_ref[...] = jnp.zeros_like(y_ref)


def _expert_ffn(xs, gates, blk_expert, n_used, e_w1, e_b1, e_w2, e_b2, tm):
    n_slots, d = xs.shape
    n_blocks = n_slots // tm
    ex = lambda i, be, nu: (be[i], 0, 0)
    grid_spec = pltpu.PrefetchScalarGridSpec(
        num_scalar_prefetch=2,
        grid=(n_blocks,),
        in_specs=[pl.BlockSpec((tm, d), lambda i, be, nu: (i, 0)),
                  pl.BlockSpec((tm, 1), lambda i, be, nu: (i, 0)),
                  pl.BlockSpec((1, d, 2 * D_FF), ex),
                  pl.BlockSpec((1, 1, 2 * D_FF), ex),
                  pl.BlockSpec((1, D_FF, d), ex),
                  pl.BlockSpec((1, 1, d), ex)],
        out_specs=pl.BlockSpec((tm, d), lambda i, be, nu: (i, 0)),
        scratch_shapes=[pltpu.VMEM((d, 2 * D_FF), bf16), pltpu.VMEM((D_FF, d), bf16)],
    )
    return pl.pallas_call(
        _expert_kernel,
        grid_spec=grid_spec,
        out_shape=SDS((n_slots, d), f32),
        compiler_params=_cparams(("arbitrary",)),
        name="expert_ffn",
    )(blk_expert, n_used, xs, gates, e_w1, e_b1.reshape(N_EXPERTS, 1, -1), e_w2,
      e_b2.reshape(N_EXPERTS, 1, -1))


def _moe(h2, logits, e_w1, e_b1, e_w2, e_b2, tm):
    m = h2.shape[0]
    top_val, top_idx = lax.top_k(logits, TOP_K)
    gate = jax.nn.softmax(top_val, axis=-1)
    n_rows = m * TOP_K
    flat_e = top_idx.reshape(-1)
    order = jnp.argsort(flat_e)
    e_sorted = flat_e[order]
    counts = jnp.bincount(flat_e, length=N_EXPERTS)
    padded = (counts + tm - 1) // tm * tm
    pad_end = jnp.cumsum(padded)
    pad_start = pad_end - padded
    grp_start = jnp.cumsum(counts) - counts
    dest = (pad_start[e_sorted] + jnp.arange(n_rows) - grp_start[e_sorted]).astype(i32)
    n_blocks = (n_rows + N_EXPERTS * (tm - 1) + tm - 1) // tm
    n_slots = n_blocks * tm
    slot_tok = jnp.full((n_slots,), m, i32).at[dest].set((order // TOP_K).astype(i32))
    slot_gate = jnp.zeros((n_slots,), f32).at[dest].set(gate.reshape(-1)[order])
    blk_expert = jnp.minimum(jnp.searchsorted(pad_end, jnp.arange(n_blocks) * tm, side='right'),
                             N_EXPERTS - 1).astype(i32)
    n_used = (pad_end[-1] // tm).astype(i32).reshape(1)
    xs = h2.at[slot_tok].get(mode='fill', fill_value=0)
    ys = _expert_ffn(xs, slot_gate.reshape(n_slots, 1), blk_expert, n_used, e_w1, e_b1, e_w2, e_b2, tm)
    inv = jnp.zeros((n_rows,), i32).at[order].set(dest)
    return ys[inv].reshape(m, TOP_K, -1).sum(axis=1)


def _final_kernel(x1_ref, f_ref, gt2_ref, g_ref, o_ref):
    o_ref[...] = x1_ref[...] + gt2_ref[...] * _rms(f_ref[...], g_ref[...])


def _final(x1, f, gt2, g, nb, tt):
    b, t, d = x1.shape
    row = pl.BlockSpec((nb, tt, d), lambda i, j: (i, j, 0))
    return pl.pallas_call(
        _final_kernel,
        grid=(b // nb, t // tt),
        in_specs=[row, row, pl.BlockSpec((nb, 1, d), lambda i, j: (i, 0, 0)),
                  pl.BlockSpec((1, d), lambda i, j: (0, 0))],
        out_specs=row,
        out_shape=SDS((b, t, d), f32),
        compiler_params=_cparams(("arbitrary", "arbitrary")),
        name="final_residual",
    )(x1, f, gt2, g.reshape(1, d))


def _mix_layer(x, mod, attend, shift_prev, wkv_prev, p, w_in_bf16, w_out_bf16, nb, tt, t_real, chunk):
    b, t, d = x.shape
    sh1, sc1, gt1, sh2, sc2, gt2 = [mod[:, i:i + 1, :] for i in range(6)]
    q, k, v, pb = _in_proj(x, sh1, sc1, p['g_mix_pre'], w_in_bf16, nb, tt)
    attn = attend(q, k, v)
    last = pb[:, tt - 1::tt]
    prev_first = jnp.concatenate([shift_prev[:, None], last[:, :-1]], axis=1)[:, :, None, :]
    r, lw, k2, v2, a, kb, g, bonus = _rwkv_prep(pb, prev_first, p, nb, tt, t_real)
    yn, wkv_new = _wkv_scan(r, lw, k2, v2, a, kb, wkv_prev, chunk)
    x1, h2, logits = _out_proj(attn, yn, bonus, g, x, gt1, sh2, sc2, p, w_out_bf16, nb, tt)
    return x1, h2, logits, gt2, k, v, wkv_new, pb[:, t_real - 1]


def kernel(x_prompt, x_sample, cache_k, cache_v, state_wkv, state_shift, page_table, c_prompt, c_sample,
           w_ada, b_ada, g_mix_pre, g_mix_post, g_ffn_pre, g_ffn_post, w_in, mu_shift,
           decay_base, decay_up, iclr_base, iclr_up, gate_up, k_k, k_a, r_k, lnx_g, lnx_b,
           g_attn_out, sb_bias, w_out, router_w, router_b, e_w1, e_b1, e_w2, e_b2):
    weights = dict(w_ada=w_ada, b_ada=b_ada, g_mix_pre=g_mix_pre, g_mix_post=g_mix_post,
                   g_ffn_pre=g_ffn_pre, g_ffn_post=g_ffn_post, w_in=w_in, mu_shift=mu_shift,
                   decay_base=decay_base, decay_up=decay_up, iclr_base=iclr_base, iclr_up=iclr_up,
                   gate_up=gate_up, k_k=k_k, k_a=k_a, r_k=r_k, lnx_g=lnx_g, lnx_b=lnx_b,
                   g_attn_out=g_attn_out, sb_bias=sb_bias, w_out=w_out, router_w=router_w,
                   router_b=router_b, e_w1=e_w1, e_b1=e_b1, e_w2=e_w2, e_b2=e_b2)
    depth = w_ada.shape[0]
    bp, tp, d = x_prompt.shape
    bs, ts, _ = x_sample.shape
    ts_pad = -(-ts // V7X_SUBLANES) * V7X_SUBLANES
    hp = x_prompt
    hs = jnp.pad(x_sample, ((0, 0), (0, ts_pad - ts), (0, 0)))
    outs = [[] for _ in range(8)]
    for l in range(depth):
        p = {name: w[l] for name, w in weights.items()}
        w_in_bf16 = p['w_in'].astype(bf16)
        w_out_bf16 = p['w_out'].astype(bf16)
        mod = _ada_mod(jnp.concatenate([c_prompt, c_sample], axis=0), p['w_ada'], p['b_ada'])
        mod = mod.reshape(bp + bs, 6, d)

        attend_p = lambda q, k, v: _attn_prompt(q, k, v, p['sb_bias'])
        x1p, h2p, lgp, gt2p, kp, vp, wp, shp = _mix_layer(
            hp, mod[:bp], attend_p, jnp.zeros((bp, SHIFT_W), f32),
            jnp.zeros((bp, H_B, HEAD_DIM, HEAD_DIM), f32), p, w_in_bf16, w_out_bf16,
            nb=1, tt=ROW_TILE, t_real=tp, chunk=WKV_CHUNK)

        attend_s = lambda q, k, v: jnp.pad(
            _attn_sample(q, k, v, cache_k, cache_v, l, page_table, p['sb_bias'], ts),
            ((0, 0), (0, ts_pad - ts), (0, 0)))
        x1s, h2s, lgs, gt2s, ks, vs, ws, shs = _mix_layer(
            hs, mod[bp:], attend_s, state_shift[l], state_wkv[l], p, w_in_bf16, w_out_bf16,
            nb=bs, tt=ts_pad, t_real=ts, chunk=ts_pad)

        mp = bp * tp
        h2_all = jnp.concatenate([h2p.reshape(mp, d), h2s[:, :ts].reshape(bs * ts, d)], axis=0)
        lg_all = jnp.concatenate([lgp.reshape(mp, N_EXPERTS), lgs[:, :ts].reshape(bs * ts, N_EXPERTS)], axis=0)
        f_all = _moe(h2_all, lg_all, p['e_w1'], p['e_b1'], p['e_w2'], p['e_b2'], MOE_TILE)
        fp = f_all[:mp].reshape(bp, tp, d)
        fs = jnp.pad(f_all[mp:].reshape(bs, ts, d), ((0, 0), (0, ts_pad - ts), (0, 0)))
        hp = _final(x1p, fp, gt2p, p['g_ffn_post'], 1, ROW_TILE)
        hs = _final(x1s, fs, gt2s, p['g_ffn_post'], bs, ts_pad)

        for lst, val in zip(outs, (kp.reshape(bp, tp, H_A, HEAD_DIM), vp.reshape(bp, tp, H_A, HEAD_DIM),
                                   ks[:, :ts].reshape(bs, ts, H_A, HEAD_DIM),
                                   vs[:, :ts].reshape(bs, ts, H_A, HEAD_DIM), wp, ws, shp, shs)):
            lst.append(val)
    return (hp, hs[:, :ts]) + tuple(jnp.stack(lst) for lst in outs)
```

```python
import functools

import jax
import jax.numpy as jnp
from jax import lax
from jax.experimental import pallas as pl
from jax.experimental.pallas import tpu as pltpu

f32 = jnp.float32
bf16 = jnp.bfloat16
i32 = jnp.int32
SDS = jax.ShapeDtypeStruct

D_MODEL = 1024
HEAD_DIM = 64
W_A = 512
W_B = 512
H_A = W_A // HEAD_DIM
H_B = W_B // HEAD_DIM
DECAY_LORA = 64
ICLR_LORA = 64
GATE_LORA = 128
SHIFT_W = 3 * W_B + DECAY_LORA + ICLR_LORA + GATE_LORA
N_EXPERTS = 32
TOP_K = 4
D_FF = D_MODEL
SWIGLU_ALPHA = 1.702
SWIGLU_LIMIT = 7.0
PAGE_SIZE = 128
RMS_EPS = 1e-6
LNX_EPS = 64e-5
LOG2E = 1.4426950408889634

V7X_SUBLANES = 8
V7X_LANES = 128
VMEM_LIMIT = 56 * 1024 * 1024

ROW_TILE = 256
WKV_CHUNK = 64
ATTN_TQ = 512
ATTN_TK = 128
ATTN_NSUB = 4
PAGES_PER_STEP = 8
MOE_TILE = 256
MOE_TOKEN_TILE = 128


def _cparams(sem):
    return pltpu.CompilerParams(dimension_semantics=sem, vmem_limit_bytes=VMEM_LIMIT)


def _bdot(a, b):
    return jnp.dot(a.astype(bf16), b.astype(bf16), preferred_element_type=f32)


def _fdot(a, b, dims=(((1,), (0,)), ((), ()))):
    return lax.dot_general(a, b, dims, precision=lax.Precision.HIGHEST, preferred_element_type=f32)


_NT = (((1,), (1,)), ((), ()))
_TN = (((0,), (0,)), ((), ()))


def _split_dot(x, m_bf16):
    hi = x.astype(bf16)
    lo = (x - hi.astype(f32)).astype(bf16)
    return (jnp.dot(hi, m_bf16, preferred_element_type=f32)
            + jnp.dot(lo, m_bf16, preferred_element_type=f32))


def _sigmoid(x):
    return 1.0 / (1.0 + jnp.exp(-x))


def _softplus(x):
    return jnp.maximum(x, 0.0) + jnp.log(1.0 + jnp.exp(-jnp.abs(x)))


def _softplus2(z):
    neg_abs = pltpu.bitcast(pltpu.bitcast(z, jnp.uint32) | jnp.uint32(0x80000000), f32)
    return jnp.maximum(z, 0.0) + jnp.log2(1.0 + jnp.exp2(neg_abs))


def _split2(x):
    hi = x.astype(bf16)
    return hi, (x - hi.astype(f32)).astype(bf16)


def _split3(x):
    hi = x.astype(bf16)
    r = x - hi.astype(f32)
    mid = r.astype(bf16)
    return hi, mid, (r - mid.astype(f32)).astype(bf16)


def _dg(a, b, dims):
    return lax.dot_general(a, b, dims, preferred_element_type=f32)


_NN = (((1,), (0,)), ((), ()))


def _dot3(a, b, dims=_NN):
    ah, al = _split2(a)
    bh, bl = _split2(b)
    return _dg(ah, bh, dims) + _dg(ah, bl, dims) + _dg(al, bh, dims)


def _dot_exact_rhs(x3, m_bf16, dims=_NN):
    return _dg(x3[0], m_bf16, dims) + _dg(x3[1], m_bf16, dims) + _dg(x3[2], m_bf16, dims)


def _rms(x, g):
    return x * lax.rsqrt(jnp.mean(x * x, axis=-1, keepdims=True) + RMS_EPS) * g


def _ada_kernel(c_ref, w_ref, b_ref, o_ref):
    c = c_ref[...]
    o_ref[...] = _bdot(c * _sigmoid(c), w_ref[...]) + b_ref[...]


def _ada_mod(c, w_ada, b_ada):
    n, d = c.shape
    nout = w_ada.shape[1]
    tn = 1536
    return pl.pallas_call(
        _ada_kernel,
        grid=(nout // tn,),
        in_specs=[pl.BlockSpec((n, d), lambda j: (0, 0)),
                  pl.BlockSpec((d, tn), lambda j: (0, j)),
                  pl.BlockSpec((1, tn), lambda j: (0, j))],
        out_specs=pl.BlockSpec((n, tn), lambda j: (0, j)),
        out_shape=SDS((n, nout), f32),
        compiler_params=_cparams(("arbitrary",)),
        name="ada_mod",
    )(c, w_ada, b_ada.reshape(1, nout))


def _inproj_kernel(x_ref, sh_ref, sc_ref, g_ref, w_ref, q_ref, k_ref, v_ref, pb_ref):
    nb, tt, d = x_ref.shape
    h = _rms(x_ref[...], g_ref[...]) * (1.0 + sc_ref[...]) + sh_ref[...]
    hb = h.reshape(nb * tt, d).astype(bf16)
    q_ref[...] = jnp.dot(hb, w_ref[:, 0:W_A], preferred_element_type=f32).reshape(nb, tt, W_A)
    k_ref[...] = jnp.dot(hb, w_ref[:, W_A:2 * W_A], preferred_element_type=f32).reshape(nb, tt, W_A)
    v_ref[...] = jnp.dot(hb, w_ref[:, 2 * W_A:3 * W_A], preferred_element_type=f32).reshape(nb, tt, W_A)
    pb_ref[...] = jnp.dot(hb, w_ref[:, 3 * W_A:], preferred_element_type=f32).reshape(nb, tt, SHIFT_W)


def _in_proj(x, sh1, sc1, g, w_in_bf16, nb, tt):
    b, t, d = x.shape
    row = lambda w: pl.BlockSpec((nb, tt, w), lambda i, j: (i, j, 0))
    mod = pl.BlockSpec((nb, 1, d), lambda i, j: (i, 0, 0))
    return pl.pallas_call(
        _inproj_kernel,
        grid=(b // nb, t // tt),
        in_specs=[row(d), mod, mod,
                  pl.BlockSpec((1, d), lambda i, j: (0, 0)),
                  pl.BlockSpec(w_in_bf16.shape, lambda i, j: (0, 0))],
        out_specs=[row(W_A), row(W_A), row(W_A), row(SHIFT_W)],
        out_shape=[SDS((b, t, W_A), f32)] * 3 + [SDS((b, t, SHIFT_W), f32)],
        compiler_params=_cparams(("arbitrary", "arbitrary")),
        name="in_proj",
    )(x, sh1, sc1, g.reshape(1, d), w_in_bf16)


def _attn_prompt_kernel(bias_ref, q_ref, k_ref, v_ref, u2_ref, o_ref, kbd_ref, vbd_ref, acc_ref,
                        *, tq, tk, nsub):
    hp = pl.program_id(1)
    qi = pl.program_id(2)
    two = 2 * tk
    big = tk * nsub
    ratio = tq // big
    n_tiles = k_ref.shape[1] // tk

    @pl.when(qi == 0)
    def _():
        first = lax.broadcasted_iota(i32, (tk, 2 * HEAD_DIM), 1) < HEAD_DIM

        def build(j, _):
            start = pl.multiple_of(j * tk, tk)
            kt = k_ref[0, pl.ds(start, tk), :]
            vt = v_ref[0, pl.ds(start, tk), :]
            kbd_ref[j, 0:tk, :] = jnp.where(first, kt, 0.0).astype(bf16)
            kbd_ref[j, tk:two, :] = jnp.where(first, 0.0, kt).astype(bf16)
            vbd_ref[j, 0:tk, :] = jnp.where(first, vt, 0.0).astype(bf16)
            vbd_ref[j, tk:two, :] = jnp.where(first, 0.0, vt).astype(bf16)
            return 0

        lax.fori_loop(0, n_tiles, build, 0)

    qb = (q_ref[0] * (LOG2E * HEAD_DIM ** -0.5)).astype(bf16)
    lane2 = lax.broadcasted_iota(i32, (1, two), 1)
    bias2 = jnp.where(lane2 < tk, bias_ref[2 * hp], bias_ref[2 * hp + 1]) * LOG2E
    bias_row = jnp.concatenate([bias2] * nsub, axis=1)
    u2 = u2_ref[...]
    acc_ref[...] = jnp.zeros_like(acc_ref)
    row = lax.broadcasted_iota(i32, (tq, nsub * two), 0)
    col = lax.broadcasted_iota(i32, (tq, nsub * two), 1)
    key_off = (col // two) * tk + col % tk

    def chunk(tile0, mask, carry):
        kb = kbd_ref[pl.ds(tile0, nsub)].reshape(nsub * two, 2 * HEAD_DIM)
        vb = vbd_ref[pl.ds(tile0, nsub)].reshape(nsub * two, 2 * HEAD_DIM)
        z = _dg(qb, kb, _NT) + bias_row
        sp = _softplus2(z)
        if mask is not None:
            sp = jnp.where(mask, sp, 0.0)
        spb = sp.astype(bf16)
        ws = [None] * nsub
        for j in range(nsub - 1, -1, -1):
            incl = jnp.dot(spb[:, j * two:(j + 1) * two], u2, preferred_element_type=f32)
            ws[j] = jnp.exp2(z[:, j * two:(j + 1) * two] - incl - carry)
            carry = carry + jnp.concatenate([jnp.broadcast_to(incl[:, 0:1], (tq, tk)),
                                             jnp.broadcast_to(incl[:, tk:tk + 1], (tq, tk))], axis=1)
        w = jnp.concatenate(ws, axis=1)
        if mask is not None:
            w = jnp.where(mask, w, 0.0)
        acc_ref[...] += jnp.dot(w.astype(bf16), vb, preferred_element_type=f32)
        return carry

    carry = jnp.zeros((tq, two), f32)
    for dgl in range(ratio - 1, -1, -1):
        carry = chunk(qi * (tq // tk) + dgl * nsub, (key_off + dgl * big) < row, carry)

    def body(n, carry):
        return chunk((qi * ratio - 1 - n) * nsub, None, carry)

    lax.fori_loop(0, qi * ratio, body, carry)
    o_ref[0] = acc_ref[...]


def _attn_prompt(q, k, v, sb_bias):
    b, t, _ = q.shape
    tq, tk, nsub = ATTN_TQ, ATTN_TK, ATTN_NSUB
    j = lax.broadcasted_iota(i32, (2 * tk, 2 * tk), 0)
    s = lax.broadcasted_iota(i32, (2 * tk, 2 * tk), 1)
    u2 = ((j >= s) & ((j // tk) == (s // tk))).astype(bf16)
    seq = pl.BlockSpec((1, t, 2 * HEAD_DIM), lambda bi, hp, qi: (bi, 0, hp))
    return pl.pallas_call(
        functools.partial(_attn_prompt_kernel, tq=tq, tk=tk, nsub=nsub),
        grid=(b, H_A // 2, t // tq),
        in_specs=[pl.BlockSpec(memory_space=pltpu.SMEM),
                  pl.BlockSpec((1, tq, 2 * HEAD_DIM), lambda bi, hp, qi: (bi, qi, hp)),
                  seq, seq,
                  pl.BlockSpec((2 * tk, 2 * tk), lambda bi, hp, qi: (0, 0))],
        out_specs=pl.BlockSpec((1, tq, 2 * HEAD_DIM), lambda bi, hp, qi: (bi, qi, hp)),
        out_shape=SDS((b, t, W_A), f32),
        scratch_shapes=[pltpu.VMEM((t // tk, 2 * tk, 2 * HEAD_DIM), bf16),
                        pltpu.VMEM((t // tk, 2 * tk, 2 * HEAD_DIM), bf16),
                        pltpu.VMEM((tq, 2 * HEAD_DIM), f32)],
        compiler_params=_cparams(("arbitrary", "arbitrary", "arbitrary")),
        name="attn_prompt",
    )(sb_bias, q, k, v, u2)


def _attn_sample_kernel(pt_ref, bias_ref, qrow_ref, knew_ref, vnew_ref, uo_ref, *rest, n_new, pages):
    kp_refs = rest[:pages]
    vp_refs = rest[pages:2 * pages]
    o_ref, acc_ref, carry_ref = rest[2 * pages:]
    step = pl.program_id(1)
    n_row = H_A * n_new
    row_head = lax.broadcasted_iota(i32, (n_row, PAGE_SIZE), 0) // n_new
    bias = jnp.broadcast_to(bias_ref[...], (n_row, PAGE_SIZE))

    def add_values(w, value_of_head):
        for h in range(H_A):
            acc_ref[...] += jnp.dot(jnp.where(row_head[:, :w.shape[1]] == h, w, 0.0).astype(bf16),
                                    value_of_head(h), preferred_element_type=f32)

    @pl.when(step == 0)
    def _():
        pad = knew_ref.shape[1]
        knew = knew_ref[0].astype(bf16)
        vnew = vnew_ref[0].astype(bf16)
        head_cols = lambda x, h: x[:, h * HEAD_DIM:(h + 1) * HEAD_DIM]
        z = sum(_dg(qrow_ref[0, h], head_cols(knew, h), _NT) for h in range(H_A)) + bias[:, :pad]
        s_idx = lax.broadcasted_iota(i32, (n_row, pad), 1)
        t_idx = lax.broadcasted_iota(i32, (n_row, pad), 0) % n_new
        mask = s_idx < t_idx
        sp = jnp.where(mask, _softplus2(z), 0.0)
        incl = jnp.zeros_like(sp)
        for j in range(n_new):
            incl = incl + jnp.where(s_idx <= j, sp[:, j:j + 1], 0.0)
        w = jnp.where(mask, jnp.exp2(z - incl), 0.0)
        acc_ref[...] = jnp.zeros_like(acc_ref)
        add_values(w, lambda h: head_cols(vnew, h))
        carry_ref[...] = jnp.broadcast_to(incl[:, 0:1], carry_ref.shape)

    uo = uo_ref[...]
    head_rows = lambda ref, h: ref.reshape(1, 1, PAGE_SIZE * H_A, HEAD_DIM)[
        0, 0, pl.ds(h, PAGE_SIZE, stride=H_A), :].astype(bf16)
    z = [sum(_dg(qrow_ref[0, h], head_rows(kp_refs[i], h), _NT) for h in range(H_A)) + bias
         for i in range(pages)]
    sp = [_softplus2(z[i]) for i in range(pages)]
    cr = [jnp.dot(sp[i].astype(bf16), uo, preferred_element_type=f32) for i in range(pages)]
    carry = carry_ref[...]
    for i in range(pages):
        w = jnp.exp2(z[i] - cr[i][:, :PAGE_SIZE] - carry)
        add_values(w, lambda h, i=i: head_rows(vp_refs[i], h))
        carry = carry + cr[i][:, PAGE_SIZE:]
    carry_ref[...] = carry

    @pl.when(step == pl.num_programs(1) - 1)
    def _():
        o_ref[0] = acc_ref[...]


def _attn_sample(q, k_new, v_new, cache_k, cache_v, layer, page_table, sb_bias, n_new):
    b, pad, _ = q.shape
    n_pages = page_table.shape[1]
    pages = PAGES_PER_STEP
    n_row = H_A * n_new
    scale = LOG2E * HEAD_DIM ** -0.5
    qh = jnp.transpose(q[:, :n_new].reshape(b, n_new, H_A, HEAD_DIM) * scale, (0, 2, 1, 3))
    eye = jnp.eye(H_A, dtype=f32)
    qrow = (qh[:, :, None, :, :] * eye[None, :, :, None, None]).reshape(b, H_A, n_row, HEAD_DIM).astype(bf16)
    bias = jnp.repeat(sb_bias * LOG2E, n_new).reshape(n_row, 1)
    j = lax.broadcasted_iota(i32, (PAGE_SIZE, PAGE_SIZE), 0)
    s = lax.broadcasted_iota(i32, (PAGE_SIZE, PAGE_SIZE), 1)
    uo = jnp.concatenate([(j >= s).astype(bf16), jnp.ones((PAGE_SIZE, PAGE_SIZE), bf16)], axis=1)

    def page_spec(i):
        return pl.BlockSpec(
            (1, 1, PAGE_SIZE, H_A, HEAD_DIM),
            lambda bi, st, pt, i=i: (layer, pt[bi, n_pages - 1 - (st * pages + i)], 0, 0, 0))

    new_spec = pl.BlockSpec((1, pad, W_A), lambda bi, st, pt: (bi, 0, 0))
    grid_spec = pltpu.PrefetchScalarGridSpec(
        num_scalar_prefetch=1,
        grid=(b, n_pages // pages),
        in_specs=[pl.BlockSpec((n_row, 1), lambda bi, st, pt: (0, 0)),
                  pl.BlockSpec((1, H_A, n_row, HEAD_DIM), lambda bi, st, pt: (bi, 0, 0, 0)),
                  new_spec, new_spec,
                  pl.BlockSpec((PAGE_SIZE, 2 * PAGE_SIZE), lambda bi, st, pt: (0, 0))]
                 + [page_spec(i) for i in range(pages)] * 2,
        out_specs=pl.BlockSpec((1, n_row, HEAD_DIM), lambda bi, st, pt: (bi, 0, 0)),
        scratch_shapes=[pltpu.VMEM((n_row, HEAD_DIM), f32), pltpu.VMEM((n_row, PAGE_SIZE), f32)],
    )
    out = pl.pallas_call(
        functools.partial(_attn_sample_kernel, n_new=n_new, pages=pages),
        grid_spec=grid_spec,
        out_shape=SDS((b, n_row, HEAD_DIM), f32),
        compiler_params=_cparams(("arbitrary", "arbitrary")),
        name="attn_sample",
    )(page_table, bias, qrow, k_new, v_new, uo, *([cache_k] * pages), *([cache_v] * pages))
    return jnp.transpose(out.reshape(b, H_A, n_new, HEAD_DIM), (0, 2, 1, 3)).reshape(b, n_new, W_A)


def _prep_kernel(pb_ref, pf_ref, mu_ref, dbase_ref, dup_ref, ibase_ref, iup_ref, gup_ref,
                 kk_ref, ka_ref, rk_ref, hsum_ref,
                 r_out, lw_out, k_out, v_out, a_out, b_out, g_out, bonus_out, *, t_real):
    nb, tt, w = pb_ref.shape
    pb = pb_ref[...]
    tpos = lax.broadcasted_iota(i32, pb.shape, 1)
    prev = jnp.where(tpos == 0, pf_ref[:, 0], pltpu.roll(pb, 1, axis=1))
    x = (pb + (prev - pb) * mu_ref[...]).reshape(nb * tt, w)
    r = x[:, 0:W_B]
    k = x[:, W_B:2 * W_B]
    v = x[:, 2 * W_B:3 * W_B]
    o = 3 * W_B
    xw = x[:, o:o + DECAY_LORA]
    xa = x[:, o + DECAY_LORA:o + DECAY_LORA + ICLR_LORA]
    xg = x[:, o + DECAY_LORA + ICLR_LORA:]
    log_w = -_softplus(-(dbase_ref[...] + _bdot(jnp.tanh(xw), dup_ref[...]))) - 0.5
    lw = -jnp.exp(log_w)
    a = _sigmoid(ibase_ref[...] + _bdot(xa, iup_ref[...]))
    g = _bdot(_sigmoid(xg), gup_ref[...])
    hsum = hsum_ref[...]
    kkf = k * kk_ref[...]
    kk = kkf / jnp.maximum(jnp.sqrt(_split_dot(kkf * kkf, hsum)), 1e-12)
    k2 = k * (1.0 + (a - 1.0) * ka_ref[...])
    bonus = _split_dot(r * k2 * rk_ref[...], hsum) * v
    na = -kk
    kb = kk * a
    if t_real < tt:
        valid = (lax.broadcasted_iota(i32, (nb, tt, W_B), 1) < t_real).reshape(nb * tt, W_B)
        zero = lambda u: jnp.where(valid, u, 0.0)
        r, lw, k2, v, na, kb = zero(r), zero(lw), zero(k2), zero(v), zero(na), zero(kb)
    g_out[...] = g.reshape(nb, tt, W_B)
    bonus_out[...] = bonus.reshape(nb, tt, W_B)
    for val, ref in ((r, r_out), (lw, lw_out), (k2, k_out), (v, v_out), (na, a_out), (kb, b_out)):
        val = val.reshape(nb, tt, W_B)
        for h in range(H_B):
            ref[:, h, :, :] = val[:, :, h * HEAD_DIM:(h + 1) * HEAD_DIM]


def _rwkv_prep(pb, prev_first, p, nb, tt, t_real):
    b, t, w = pb.shape
    row = lambda i, j: (i, j, 0)
    const = lambda i, j: (0, 0)
    vec = lambda a: a.reshape(1, -1)
    head_of = jnp.arange(W_B) // HEAD_DIM
    hsum = (head_of[:, None] == head_of[None, :]).astype(bf16)
    params = [vec(p['mu_shift']), vec(p['decay_base']), p['decay_up'], vec(p['iclr_base']), p['iclr_up'],
              p['gate_up'], vec(p['k_k']), vec(p['k_a']), vec(p['r_k']), hsum]
    heads = pl.BlockSpec((nb, H_B, tt, HEAD_DIM), lambda i, j: (i, 0, j, 0))
    return pl.pallas_call(
        functools.partial(_prep_kernel, t_real=t_real),
        grid=(b // nb, t // tt),
        in_specs=[pl.BlockSpec((nb, tt, w), row),
                  pl.BlockSpec((nb, 1, 1, w), lambda i, j: (i, j, 0, 0))]
                 + [pl.BlockSpec(a.shape, const) for a in params],
        out_specs=[heads] * 6 + [pl.BlockSpec((nb, tt, W_B), row)] * 2,
        out_shape=[SDS((b, H_B, t, HEAD_DIM), f32)] * 6 + [SDS((b, t, W_B), f32)] * 2,
        compiler_params=_cparams(("arbitrary", "arbitrary")),
        name="rwkv_prep",
    )(pb, prev_first, *params)


def _wkv_kernel(r_ref, lw_ref, k_ref, v_ref, a_ref, b_ref, s0_ref, y_ref, s_out, st_ref, *, chunk):
    c = chunk
    n = HEAD_DIM
    step = pl.program_id(1)

    @pl.when(step == 0)
    def _():
        for h in range(H_B):
            st_ref[h] = s0_ref[0, h].T

    ti = lax.broadcasted_iota(i32, (c, c), 0)
    si = lax.broadcasted_iota(i32, (c, c), 1)
    tri = (ti >= si).astype(bf16)
    eye = (ti == si).astype(f32)
    ones = jnp.ones((c, n), bf16)
    row2 = lax.broadcasted_iota(i32, (c, 2 * c), 0)
    col2 = lax.broadcasted_iota(i32, (c, 2 * c), 1)
    strict2 = (col2 % c) < row2
    incl2 = (col2 % c) <= row2
    right = col2 >= c
    levels = max(c.bit_length() - 2, 0)
    heads = range(H_B)
    each = lambda f: [f(h) for h in heads]
    lw = each(lambda h: lw_ref[0, h])
    lw3 = each(lambda h: _split3(lw[h]))
    cum = each(lambda h: sum(_dg(tri, t, _NN) for t in lw3[h]))
    wsum = each(lambda h: sum(_dg(t, ones, _TN) for t in lw3[h]))
    w_in = each(lambda h: jnp.exp(cum[h]))
    w_out = each(lambda h: jnp.exp(-cum[h]))
    at = each(lambda h: a_ref[0, h] * jnp.exp(cum[h] - lw[h]))
    rt = each(lambda h: r_ref[0, h] * w_in[h])
    bk = each(lambda h: jnp.concatenate([b_ref[0, h] * w_out[h], k_ref[0, h] * w_out[h]], axis=0))
    g = each(lambda h: _dot3(jnp.concatenate([at[h], rt[h]], axis=0), bk[h], _NT))
    top = each(lambda h: jnp.where(strict2, g[h][:c], 0.0))
    bot = each(lambda h: jnp.where(incl2, g[h][c:], 0.0))
    pw = each(lambda h: top[h][:, :c])
    inv = each(lambda h: eye + pw[h])
    for _ in range(levels):
        pw = each(lambda h: _dot3(pw[h], pw[h]))
        inv = each(lambda h: inv[h] + _dot3(inv[h], pw[h]))
    vv = each(lambda h: jnp.concatenate([v_ref[0, h], v_ref[0, h]], axis=0))
    xv = each(lambda h: _dot3(jnp.where(right, top[h], 0.0), vv[h]))
    x = each(lambda h: _dot3(at[h], st_ref[h]) + xv[h])
    u = each(lambda h: _dot3(inv[h], x[h]))
    uv = each(lambda h: jnp.concatenate([u[h], v_ref[0, h]], axis=0))
    y = each(lambda h: _bdot(bot[h], uv[h]) + _bdot(rt[h], st_ref[h]))
    st_new = each(lambda h: (st_ref[h] + _dot3(bk[h], uv[h], _TN)) * jnp.exp(wsum[h]))
    for h in heads:
        st_ref[h] = st_new[h]
        mu = jnp.mean(y[h], axis=-1, keepdims=True)
        yc = y[h] - mu
        var = jnp.mean(yc * yc, axis=-1, keepdims=True)
        y_ref[0, :, h * HEAD_DIM:(h + 1) * HEAD_DIM] = yc * lax.rsqrt(var + LNX_EPS)

    @pl.when(step == pl.num_programs(1) - 1)
    def _():
        for h in range(H_B):
            s_out[0, h] = st_ref[h].T


def _wkv_scan(r, lw, k, v, a, b, s0, chunk):
    bsz, h, t, n = r.shape
    seq = pl.BlockSpec((1, h, chunk, n), lambda i, j: (i, 0, j, 0))
    state = pl.BlockSpec((1, h, n, n), lambda i, j: (i, 0, 0, 0))
    return pl.pallas_call(
        functools.partial(_wkv_kernel, chunk=chunk),
        grid=(bsz, t // chunk),
        in_specs=[seq] * 6 + [state],
        out_specs=[pl.BlockSpec((1, chunk, h * n), lambda i, j: (i, j, 0)), state],
        out_shape=[SDS((bsz, t, h * n), f32), SDS((bsz, h, n, n), f32)],
        scratch_shapes=[pltpu.VMEM((h, n, n), f32)],
        compiler_params=_cparams(("arbitrary", "arbitrary")),
        name="wkv_scan",
    )(r, lw, k, v, a, b, s0)


def _outproj_kernel(attn_ref, yn_ref, bonus_ref, g_ref, x_ref, gt1_ref, sh2_ref, sc2_ref,
                    gattn_ref, lnxg_ref, lnxb_ref, wout_ref, gpost_ref, gpre_ref, rw_ref, rb_ref,
                    x1_ref, h2_ref, logit_ref):
    nb, tt, d = x_ref.shape
    o_a = _rms(attn_ref[...], gattn_ref[...])
    o_b = (yn_ref[...] * lnxg_ref[...] + lnxb_ref[...] + bonus_ref[...]) * g_ref[...]
    cat = jnp.concatenate([o_a, o_b], axis=-1).reshape(nb * tt, d)
    mixed = _bdot(cat, wout_ref[...]).reshape(nb, tt, d)
    x1 = x_ref[...] + gt1_ref[...] * _rms(mixed, gpost_ref[...])
    x1_ref[...] = x1
    h2 = _rms(x1, gpre_ref[...]) * (1.0 + sc2_ref[...]) + sh2_ref[...]
    h2_ref[...] = h2
    logits = _fdot(h2.reshape(nb * tt, d), rw_ref[...]) + rb_ref[...]
    logit_ref[...] = logits.reshape(nb, tt, N_EXPERTS)


def _out_proj(attn, yn, bonus, g, x, gt1, sh2, sc2, p, w_out_bf16, nb, tt):
    b, t, d = x.shape
    row = lambda w: pl.BlockSpec((nb, tt, w), lambda i, j: (i, j, 0))
    mod = pl.BlockSpec((nb, 1, d), lambda i, j: (i, 0, 0))
    vec = lambda a: a.reshape(1, -1)
    params = [vec(p['g_attn_out']), vec(p['lnx_g']), vec(p['lnx_b']), w_out_bf16,
              vec(p['g_mix_post']), vec(p['g_ffn_pre']), p['router_w'], vec(p['router_b'])]
    return pl.pallas_call(
        _outproj_kernel,
        grid=(b // nb, t // tt),
        in_specs=[row(W_A), row(W_B), row(W_B), row(W_B), row(d), mod, mod, mod]
                 + [pl.BlockSpec(a.shape, lambda i, j: (0, 0)) for a in params],
        out_specs=[row(d), row(d), row(N_EXPERTS)],
        out_shape=[SDS((b, t, d), f32), SDS((b, t, d), f32), SDS((b, t, N_EXPERTS), f32)],
        compiler_params=_cparams(("arbitrary", "arbitrary")),
        name="out_proj",
    )(attn, yn, bonus, g, x, gt1, sh2, sc2, *params)


def _route_kernel(logit_ref, idx_ref, gate_ref, rank_ref, count_ref, base_ref):
    tile = logit_ref.shape[0]

    @pl.when(pl.program_id(0) == 0)
    def _():
        base_ref[...] = jnp.zeros_like(base_ref)

    lane = lax.broadcasted_iota(i32, (tile, N_EXPERTS), 1).astype(f32)
    cur = logit_ref[...]
    hots, vals, idxs = [], [], []
    for _ in range(TOP_K):
        top = jnp.max(cur, axis=-1, keepdims=True)
        idx = jnp.min(jnp.where(cur == top, lane, float(N_EXPERTS)), axis=-1, keepdims=True)
        hot = lane == idx
        hots.append(hot)
        vals.append(top)
        idxs.append(idx)
        cur = jnp.where(hot, -jnp.inf, cur)
    es = [jnp.exp(v - vals[0]) for v in vals]
    total = sum(es)
    chosen = sum(h.astype(f32) for h in hots)
    ti = lax.broadcasted_iota(i32, (tile, tile), 0)
    si = lax.broadcasted_iota(i32, (tile, tile), 1)
    earlier = jnp.dot((ti > si).astype(bf16), chosen.astype(bf16), preferred_element_type=f32) + base_ref[...]
    ranks = [jnp.sum(jnp.where(h, earlier, 0.0), axis=-1, keepdims=True) for h in hots]
    base_ref[...] += jnp.sum(chosen, axis=0, keepdims=True)
    count_ref[...] = base_ref[...]
    col = lax.broadcasted_iota(i32, (tile, TOP_K), 1)
    pick = lambda parts: sum(jnp.where(col == k, parts[k], 0.0) for k in range(TOP_K))
    idx_ref[...] = pick(idxs).astype(i32)
    gate_ref[...] = pick([e / total for e in es])
    rank_ref[...] = pick(ranks).astype(i32)


def _route(logits, tile):
    m = logits.shape[0]
    tok = lambda dt: SDS((m, TOP_K), dt)
    blk = pl.BlockSpec((tile, TOP_K), lambda i: (i, 0))
    return pl.pallas_call(
        _route_kernel,
        grid=(m // tile,),
        in_specs=[pl.BlockSpec((tile, N_EXPERTS), lambda i: (i, 0))],
        out_specs=[blk, blk, blk, pl.BlockSpec((1, N_EXPERTS), lambda i: (0, 0))],
        out_shape=[tok(i32), tok(f32), tok(i32), SDS((1, N_EXPERTS), f32)],
        scratch_shapes=[pltpu.VMEM((1, N_EXPERTS), f32)],
        compiler_params=_cparams(("arbitrary",)),
        name="moe_route",
    )(logits)


def _row_copy(src, src_row, dst, dst_row, sem):
    return pltpu.make_async_copy(src.at[pl.ds(src_row, 1)], dst.at[pl.ds(dst_row, 1)], sem)


def _dispatch_kernel(dest_ref, x_ref, xs_in, xs_out, sem):
    del xs_in
    tile = x_ref.shape[0]

    def start(r, _):
        for k in range(TOP_K):
            _row_copy(x_ref, r, xs_out, dest_ref[k, r], sem).start()
        return 0

    def wait(r, _):
        for k in range(TOP_K):
            _row_copy(x_ref, r, xs_out, dest_ref[k, r], sem).wait()
        return 0

    lax.fori_loop(0, tile, start, 0)
    lax.fori_loop(0, tile, wait, 0)


def _dispatch(x, dest_t, n_slots, tile):
    m, d = x.shape
    return pl.pallas_call(
        _dispatch_kernel,
        grid=(m // tile,),
        in_specs=[pl.BlockSpec((TOP_K, tile), lambda i: (0, i), memory_space=pltpu.SMEM),
                  pl.BlockSpec((tile, d), lambda i: (i, 0)),
                  pl.BlockSpec(memory_space=pl.ANY)],
        out_specs=pl.BlockSpec(memory_space=pl.ANY),
        out_shape=SDS((n_slots, d), x.dtype),
        scratch_shapes=[pltpu.SemaphoreType.DMA(())],
        input_output_aliases={2: 0},
        compiler_params=_cparams(("arbitrary",)),
        name="moe_dispatch",
    )(dest_t, x, jnp.zeros((n_slots, d), x.dtype))


def _combine_kernel(dest_ref, gate_ref, ys_hbm, f_ref, buf_ref, sem):
    tile = f_ref.shape[0]

    def start(r, _):
        for k in range(TOP_K):
            _row_copy(ys_hbm, dest_ref[k, r], buf_ref.at[k], r, sem).start()
        return 0

    def wait(r, _):
        for k in range(TOP_K):
            _row_copy(ys_hbm, dest_ref[k, r], buf_ref.at[k], r, sem).wait()
        return 0

    lax.fori_loop(0, tile, start, 0)
    lax.fori_loop(0, tile, wait, 0)
    gate = gate_ref[...]
    f_ref[...] = sum(gate[:, k:k + 1] * buf_ref[k] for k in range(TOP_K))


def _combine(ys, dest_t, gate, tile):
    m = gate.shape[0]
    d = ys.shape[1]
    return pl.pallas_call(
        _combine_kernel,
        grid=(m // tile,),
        in_specs=[pl.BlockSpec((TOP_K, tile), lambda i: (0, i), memory_space=pltpu.SMEM),
                  pl.BlockSpec((tile, TOP_K), lambda i: (i, 0)),
                  pl.BlockSpec(memory_space=pl.ANY)],
        out_specs=pl.BlockSpec((tile, d), lambda i: (i, 0)),
        out_shape=SDS((m, d), f32),
        scratch_shapes=[pltpu.VMEM((TOP_K, tile, d), f32), pltpu.SemaphoreType.DMA(())],
        compiler_params=_cparams(("arbitrary",)),
        name="moe_combine",
    )(dest_t, gate, ys)


def _expert_kernel(be_ref, nused_ref, x_ref, w1_ref, b1_ref, w2_ref, b2_ref, y_ref, w1b_ref, w2b_ref):
    i = pl.program_id(0)
    prev = be_ref[jnp.maximum(i - 1, 0)]

    @pl.when((i == 0) | (be_ref[i] != prev))
    def _():
        w1b_ref[...] = w1_ref[0].astype(bf16)
        w2b_ref[...] = w2_ref[0].astype(bf16)

    @pl.when(i < nused_ref[0])
    def _():
        u = jnp.dot(x_ref[...].astype(bf16), w1b_ref[...], preferred_element_type=f32) + b1_ref[0]
        u_glu = jnp.minimum(u[:, :D_FF], SWIGLU_LIMIT)
        u_lin = jnp.clip(u[:, D_FF:], -SWIGLU_LIMIT, SWIGLU_LIMIT)
        act = u_glu * _sigmoid(SWIGLU_ALPHA * u_glu) * (u_lin + 1.0)
        y_ref[...] = jnp.dot(act.astype(bf16), w2b_ref[...], preferred_element_type=f32) + b2_ref[0]

    @pl.when(i >= nused_ref[0])
    def _():
        y_ref[...] = jnp.zeros_like(y_ref)


def _expert_ffn(xs, blk_expert, n_used, e_w1, e_b1, e_w2, e_b2, tm):
    n_slots, d = xs.shape
    n_blocks = n_slots // tm
    ex = lambda i, be, nu: (be[i], 0, 0)
    grid_spec = pltpu.PrefetchScalarGridSpec(
        num_scalar_prefetch=2,
        grid=(n_blocks,),
        in_specs=[pl.BlockSpec((tm, d), lambda i, be, nu: (i, 0)),
                  pl.BlockSpec((1, d, 2 * D_FF), ex),
                  pl.BlockSpec((1, 1, 2 * D_FF), ex),
                  pl.BlockSpec((1, D_FF, d), ex),
                  pl.BlockSpec((1, 1, d), ex)],
        out_specs=pl.BlockSpec((tm, d), lambda i, be, nu: (i, 0)),
        scratch_shapes=[pltpu.VMEM((d, 2 * D_FF), bf16), pltpu.VMEM((D_FF, d), bf16)],
    )
    return pl.pallas_call(
        _expert_kernel,
        grid_spec=grid_spec,
        out_shape=SDS((n_slots, d), f32),
        compiler_params=_cparams(("arbitrary",)),
        name="expert_ffn",
    )(blk_expert, n_used, xs, e_w1, e_b1.reshape(N_EXPERTS, 1, -1), e_w2, e_b2.reshape(N_EXPERTS, 1, -1))


def _moe(h2, logits, e_w1, e_b1, e_w2, e_b2, tm):
    m_real = h2.shape[0]
    tile = MOE_TOKEN_TILE
    extra = -m_real % tile
    h2 = jnp.pad(h2, ((0, extra), (0, 0)))
    logits = jnp.pad(logits, ((0, extra), (0, 0)))
    m = m_real + extra
    top_idx, gate, rank, counts = _route(logits, tile)
    counts = counts.reshape(N_EXPERTS).astype(i32)
    padded = (counts + tm - 1) // tm * tm
    pad_end = jnp.cumsum(padded)
    pad_start = pad_end - padded
    n_blocks = (m * TOP_K + N_EXPERTS * (tm - 1) + tm - 1) // tm
    blk_expert = jnp.minimum(jnp.searchsorted(pad_end, jnp.arange(n_blocks) * tm, side='right'),
                             N_EXPERTS - 1).astype(i32)
    n_used = (pad_end[-1] // tm).astype(i32).reshape(1)
    dest_t = (pad_start[top_idx] + rank).astype(i32).T
    xs = _dispatch(h2, dest_t, n_blocks * tm, tile)
    ys = _expert_ffn(xs, blk_expert, n_used, e_w1, e_b1, e_w2, e_b2, tm)
    return _combine(ys, dest_t, gate, tile)[:m_real]


def _final_kernel(x1_ref, f_ref, gt2_ref, g_ref, o_ref):
    o_ref[...] = x1_ref[...] + gt2_ref[...] * _rms(f_ref[...], g_ref[...])


def _final(x1, f, gt2, g, nb, tt):
    b, t, d = x1.shape
    row = pl.BlockSpec((nb, tt, d), lambda i, j: (i, j, 0))
    return pl.pallas_call(
        _final_kernel,
        grid=(b // nb, t // tt),
        in_specs=[row, row, pl.BlockSpec((nb, 1, d), lambda i, j: (i, 0, 0)),
                  pl.BlockSpec((1, d), lambda i, j: (0, 0))],
        out_specs=row,
        out_shape=SDS((b, t, d), f32),
        compiler_params=_cparams(("arbitrary", "arbitrary")),
        name="final_residual",
    )(x1, f, gt2, g.reshape(1, d))


def _mix_layer(x, mod, attend, shift_prev, wkv_prev, p, w_in_bf16, w_out_bf16, nb, tt, t_real, chunk):
    b, t, d = x.shape
    sh1, sc1, gt1, sh2, sc2, gt2 = [mod[:, i:i + 1, :] for i in range(6)]
    q, k, v, pb = _in_proj(x, sh1, sc1, p['g_mix_pre'], w_in_bf16, nb, tt)
    attn = attend(q, k, v)
    last = pb[:, tt - 1::tt]
    prev_first = jnp.concatenate([shift_prev[:, None], last[:, :-1]], axis=1)[:, :, None, :]
    r, lw, k2, v2, a, kb, g, bonus = _rwkv_prep(pb, prev_first, p, nb, tt, t_real)
    yn, wkv_new = _wkv_scan(r, lw, k2, v2, a, kb, wkv_prev, chunk)
    x1, h2, logits = _out_proj(attn, yn, bonus, g, x, gt1, sh2, sc2, p, w_out_bf16, nb, tt)
    return x1, h2, logits, gt2, k, v, wkv_new, pb[:, t_real - 1]


def kernel(x_prompt, x_sample, cache_k, cache_v, state_wkv, state_shift, page_table, c_prompt, c_sample,
           w_ada, b_ada, g_mix_pre, g_mix_post, g_ffn_pre, g_ffn_post, w_in, mu_shift,
           decay_base, decay_up, iclr_base, iclr_up, gate_up, k_k, k_a, r_k, lnx_g, lnx_b,
           g_attn_out, sb_bias, w_out, router_w, router_b, e_w1, e_b1, e_w2, e_b2):
    weights = dict(w_ada=w_ada, b_ada=b_ada, g_mix_pre=g_mix_pre, g_mix_post=g_mix_post,
                   g_ffn_pre=g_ffn_pre, g_ffn_post=g_ffn_post, w_in=w_in, mu_shift=mu_shift,
                   decay_base=decay_base, decay_up=decay_up, iclr_base=iclr_base, iclr_up=iclr_up,
                   gate_up=gate_up, k_k=k_k, k_a=k_a, r_k=r_k, lnx_g=lnx_g, lnx_b=lnx_b,
                   g_attn_out=g_attn_out, sb_bias=sb_bias, w_out=w_out, router_w=router_w,
                   router_b=router_b, e_w1=e_w1, e_b1=e_b1, e_w2=e_w2, e_b2=e_b2)
    depth = w_ada.shape[0]
    bp, tp, d = x_prompt.shape
    bs, ts, _ = x_sample.shape
    ts_pad = -(-ts // V7X_SUBLANES) * V7X_SUBLANES
    hp = x_prompt
    hs = jnp.pad(x_sample, ((0, 0), (0, ts_pad - ts), (0, 0)))
    outs = [[] for _ in range(8)]
    for l in range(depth):
        p = {name: w[l] for name, w in weights.items()}
        w_in_bf16 = p['w_in'].astype(bf16)
        w_out_bf16 = p['w_out'].astype(bf16)
        mod = _ada_mod(jnp.concatenate([c_prompt, c_sample], axis=0), p['w_ada'], p['b_ada'])
        mod = mod.reshape(bp + bs, 6, d)

        attend_p = lambda q, k, v: _attn_prompt(q, k, v, p['sb_bias'])
        x1p, h2p, lgp, gt2p, kp, vp, wp, shp = _mix_layer(
            hp, mod[:bp], attend_p, jnp.zeros((bp, SHIFT_W), f32),
            jnp.zeros((bp, H_B, HEAD_DIM, HEAD_DIM), f32), p, w_in_bf16, w_out_bf16,
            nb=1, tt=ROW_TILE, t_real=tp, chunk=WKV_CHUNK)

        attend_s = lambda q, k, v: jnp.pad(
            _attn_sample(q, k, v, cache_k, cache_v, l, page_table, p['sb_bias'], ts),
            ((0, 0), (0, ts_pad - ts), (0, 0)))
        x1s, h2s, lgs, gt2s, ks, vs, ws, shs = _mix_layer(
            hs, mod[bp:], attend_s, state_shift[l], state_wkv[l], p, w_in_bf16, w_out_bf16,
            nb=bs, tt=ts_pad, t_real=ts, chunk=ts_pad)

        mp = bp * tp
        h2_all = jnp.concatenate([h2p.reshape(mp, d), h2s[:, :ts].reshape(bs * ts, d)], axis=0)
        lg_all = jnp.concatenate([lgp.reshape(mp, N_EXPERTS), lgs[:, :ts].reshape(bs * ts, N_EXPERTS)], axis=0)
        f_all = _moe(h2_all, lg_all, p['e_w1'], p['e_b1'], p['e_w2'], p['e_b2'], MOE_TILE)
        fp = f_all[:mp].reshape(bp, tp, d)
        fs = jnp.pad(f_all[mp:].reshape(bs, ts, d), ((0, 0), (0, ts_pad - ts), (0, 0)))
        hp = _final(x1p, fp, gt2p, p['g_ffn_post'], 1, ROW_TILE)
        hs = _final(x1s, fs, gt2s, p['g_ffn_post'], bs, ts_pad)

        for lst, val in zip(outs, (kp.reshape(bp, tp, H_A, HEAD_DIM), vp.reshape(bp, tp, H_A, HEAD_DIM),
                                   ks[:, :ts].reshape(bs, ts, H_A, HEAD_DIM),
                                   vs[:, :ts].reshape(bs, ts, H_A, HEAD_DIM), wp, ws, shp, shs)):
            lst.append(val)
    return (hp, hs[:, :ts]) + tuple(jnp.stack(lst) for lst in outs)
```

```python
import functools

import jax
import jax.numpy as jnp
from jax import lax
from jax.experimental import pallas as pl
from jax.experimental.pallas import tpu as pltpu

f32 = jnp.float32
bf16 = jnp.bfloat16
i32 = jnp.int32
SDS = jax.ShapeDtypeStruct

D_MODEL = 1024
HEAD_DIM = 64
W_A = 512
W_B = 512
H_A = W_A // HEAD_DIM
H_B = W_B // HEAD_DIM
DECAY_LORA = 64
ICLR_LORA = 64
GATE_LORA = 128
SHIFT_W = 3 * W_B + DECAY_LORA + ICLR_LORA + GATE_LORA
N_EXPERTS = 32
TOP_K = 4
D_FF = D_MODEL
SWIGLU_ALPHA = 1.702
SWIGLU_LIMIT = 7.0
PAGE_SIZE = 128
RMS_EPS = 1e-6
LNX_EPS = 64e-5
LOG2E = 1.4426950408889634

V7X_SUBLANES = 8
V7X_LANES = 128
VMEM_LIMIT = 56 * 1024 * 1024

ROW_TILE = 256
WKV_CHUNK = 64
WKV_NSEQ = 2
ATTN_TQ = 512
ATTN_TK = 128
ATTN_NSUB = 4
PAGES_PER_STEP = 16
MOE_TILE = 256
MOE_TOKEN_TILE = 128
DMA_ISSUE_UNROLL = 8


def _cparams(sem):
    return pltpu.CompilerParams(dimension_semantics=sem, vmem_limit_bytes=VMEM_LIMIT)


def _bdot(a, b):
    return jnp.dot(a.astype(bf16), b.astype(bf16), preferred_element_type=f32)


def _fdot(a, b, dims=(((1,), (0,)), ((), ()))):
    return lax.dot_general(a, b, dims, precision=lax.Precision.HIGHEST, preferred_element_type=f32)


_NT = (((1,), (1,)), ((), ()))
_TN = (((0,), (0,)), ((), ()))


def _split_dot(x, m_bf16):
    hi = x.astype(bf16)
    lo = (x - hi.astype(f32)).astype(bf16)
    return (jnp.dot(hi, m_bf16, preferred_element_type=f32)
            + jnp.dot(lo, m_bf16, preferred_element_type=f32))


def _sigmoid(x):
    return 1.0 / (1.0 + jnp.exp(-x))


def _softplus(x):
    return jnp.maximum(x, 0.0) + jnp.log(1.0 + jnp.exp(-jnp.abs(x)))


def _softplus2(z):
    return jnp.maximum(z, 0.0) + jnp.log2(1.0 + jnp.exp2(-jnp.abs(z)))


def _split2(x):
    hi = x.astype(bf16)
    return hi, (x - hi.astype(f32)).astype(bf16)


def _split3(x):
    hi = x.astype(bf16)
    r = x - hi.astype(f32)
    mid = r.astype(bf16)
    return hi, mid, (r - mid.astype(f32)).astype(bf16)


def _dg(a, b, dims):
    return lax.dot_general(a, b, dims, preferred_element_type=f32)


_NN = (((1,), (0,)), ((), ()))


def _dot3(a, b, dims=_NN):
    ah, al = _split2(a)
    bh, bl = _split2(b)
    return _dg(ah, bh, dims) + _dg(ah, bl, dims) + _dg(al, bh, dims)


def _dot_exact_rhs(x3, m_bf16, dims=_NN):
    return _dg(x3[0], m_bf16, dims) + _dg(x3[1], m_bf16, dims) + _dg(x3[2], m_bf16, dims)


def _rms(x, g):
    return x * lax.rsqrt(jnp.mean(x * x, axis=-1, keepdims=True) + RMS_EPS) * g


def _ada_kernel(c_ref, w_ref, b_ref, o_ref):
    c = c_ref[...]
    o_ref[...] = _bdot(c * _sigmoid(c), w_ref[...]) + b_ref[...]


def _ada_mod(c, w_ada, b_ada):
    n, d = c.shape
    nout = w_ada.shape[1]
    tn = 1536
    return pl.pallas_call(
        _ada_kernel,
        grid=(nout // tn,),
        in_specs=[pl.BlockSpec((n, d), lambda j: (0, 0)),
                  pl.BlockSpec((d, tn), lambda j: (0, j)),
                  pl.BlockSpec((1, tn), lambda j: (0, j))],
        out_specs=pl.BlockSpec((n, tn), lambda j: (0, j)),
        out_shape=SDS((n, nout), f32),
        compiler_params=_cparams(("arbitrary",)),
        name="ada_mod",
    )(c, w_ada, b_ada.reshape(1, nout))


def _inproj_kernel(x_ref, sh_ref, sc_ref, g_ref, w_ref, q_ref, k_ref, v_ref, pb_ref, last_ref, *, t_last):
    nb, tt, d = x_ref.shape
    h = _rms(x_ref[...], g_ref[...]) * (1.0 + sc_ref[...]) + sh_ref[...]
    hb = h.reshape(nb * tt, d).astype(bf16)
    q_ref[...] = jnp.dot(hb, w_ref[:, 0:W_A], preferred_element_type=f32).reshape(nb, tt, W_A)
    k_ref[...] = jnp.dot(hb, w_ref[:, W_A:2 * W_A], preferred_element_type=f32).reshape(nb, tt, W_A)
    v_ref[...] = jnp.dot(hb, w_ref[:, 2 * W_A:3 * W_A], preferred_element_type=f32).reshape(nb, tt, W_A)
    pb = jnp.dot(hb, w_ref[:, 3 * W_A:], preferred_element_type=f32).reshape(nb, tt, SHIFT_W)
    pb_ref[...] = pb
    last_ref[:, 0] = pb[:, t_last:t_last + 1, :]


def _in_proj(x, sh1, sc1, g, w_in_bf16, nb, tt, t_real):
    b, t, d = x.shape
    row = lambda w: pl.BlockSpec((nb, tt, w), lambda i, j: (i, j, 0))
    mod = pl.BlockSpec((nb, 1, d), lambda i, j: (i, 0, 0))
    return pl.pallas_call(
        functools.partial(_inproj_kernel, t_last=min(tt, t_real) - 1),
        grid=(b // nb, t // tt),
        in_specs=[row(d), mod, mod,
                  pl.BlockSpec((1, d), lambda i, j: (0, 0)),
                  pl.BlockSpec(w_in_bf16.shape, lambda i, j: (0, 0))],
        out_specs=[row(W_A), row(W_A), row(W_A), row(SHIFT_W),
                   pl.BlockSpec((nb, 1, 1, SHIFT_W), lambda i, j: (i, j, 0, 0))],
        out_shape=[SDS((b, t, W_A), f32)] * 3 + [SDS((b, t, SHIFT_W), f32),
                                                  SDS((b, t // tt, 1, SHIFT_W), f32)],
        compiler_params=_cparams(("arbitrary", "arbitrary")),
        name="in_proj",
    )(x, sh1, sc1, g.reshape(1, d), w_in_bf16)


def _attn_prompt_kernel(bias_ref, q_ref, k_ref, v_ref, u2_ref, o_ref, kbd_ref, vbd_ref, acc_ref,
                        *, tq, tk, nsub):
    hp = pl.program_id(1)
    qi = pl.program_id(2)
    two = 2 * tk
    big = tk * nsub
    ratio = tq // big
    n_tiles = k_ref.shape[1] // tk

    @pl.when(qi == 0)
    def _():
        first = lax.broadcasted_iota(i32, (tk, 2 * HEAD_DIM), 1) < HEAD_DIM

        def build(j, _):
            start = pl.multiple_of(j * tk, tk)
            kt = k_ref[0, pl.ds(start, tk), :]
            vt = v_ref[0, pl.ds(start, tk), :]
            kbd_ref[j, 0:tk, :] = jnp.where(first, kt, 0.0).astype(bf16)
            kbd_ref[j, tk:two, :] = jnp.where(first, 0.0, kt).astype(bf16)
            vbd_ref[j, 0:tk, :] = jnp.where(first, vt, 0.0).astype(bf16)
            vbd_ref[j, tk:two, :] = jnp.where(first, 0.0, vt).astype(bf16)
            return 0

        lax.fori_loop(0, n_tiles, build, 0)

    qb = (q_ref[0] * (LOG2E * HEAD_DIM ** -0.5)).astype(bf16)
    lane2 = lax.broadcasted_iota(i32, (1, two), 1)
    bias2 = jnp.where(lane2 < tk, bias_ref[2 * hp], bias_ref[2 * hp + 1]) * LOG2E
    bias_row = jnp.concatenate([bias2] * nsub, axis=1)
    u2 = u2_ref[...]
    acc_ref[...] = jnp.zeros_like(acc_ref)
    row = lax.broadcasted_iota(i32, (tq, nsub * two), 0)
    col = lax.broadcasted_iota(i32, (tq, nsub * two), 1)
    key_off = (col // two) * tk + col % tk

    def chunk(tile0, mask, carry):
        kb = kbd_ref[pl.ds(tile0, nsub)].reshape(nsub * two, 2 * HEAD_DIM)
        vb = vbd_ref[pl.ds(tile0, nsub)].reshape(nsub * two, 2 * HEAD_DIM)
        z = _dg(qb, kb, _NT) + bias_row
        sp = _softplus2(z)
        if mask is not None:
            sp = jnp.where(mask, sp, 0.0)
        spb = sp.astype(bf16)
        ws = [None] * nsub
        for j in range(nsub - 1, -1, -1):
            incl = jnp.dot(spb[:, j * two:(j + 1) * two], u2, preferred_element_type=f32)
            ws[j] = jnp.exp2(z[:, j * two:(j + 1) * two] - incl - carry)
            carry = carry + jnp.concatenate([jnp.broadcast_to(incl[:, 0:1], (tq, tk)),
                                             jnp.broadcast_to(incl[:, tk:tk + 1], (tq, tk))], axis=1)
        w = jnp.concatenate(ws, axis=1)
        if mask is not None:
            w = jnp.where(mask, w, 0.0)
        acc_ref[...] += jnp.dot(w.astype(bf16), vb, preferred_element_type=f32)
        return carry

    carry = jnp.zeros((tq, two), f32)
    for dgl in range(ratio - 1, -1, -1):
        carry = chunk(qi * (tq // tk) + dgl * nsub, (key_off + dgl * big) < row, carry)

    def body(n, carry):
        return chunk((qi * ratio - 1 - n) * nsub, None, carry)

    lax.fori_loop(0, qi * ratio, body, carry)
    o_ref[0] = acc_ref[...]


def _attn_prompt(q, k, v, sb_bias):
    b, t, _ = q.shape
    tq, tk, nsub = ATTN_TQ, ATTN_TK, ATTN_NSUB
    j = lax.broadcasted_iota(i32, (2 * tk, 2 * tk), 0)
    s = lax.broadcasted_iota(i32, (2 * tk, 2 * tk), 1)
    u2 = ((j >= s) & ((j // tk) == (s // tk))).astype(bf16)
    seq = pl.BlockSpec((1, t, 2 * HEAD_DIM), lambda bi, hp, qi: (bi, 0, hp))
    return pl.pallas_call(
        functools.partial(_attn_prompt_kernel, tq=tq, tk=tk, nsub=nsub),
        grid=(b, H_A // 2, t // tq),
        in_specs=[pl.BlockSpec(memory_space=pltpu.SMEM),
                  pl.BlockSpec((1, tq, 2 * HEAD_DIM), lambda bi, hp, qi: (bi, qi, hp)),
                  seq, seq,
                  pl.BlockSpec((2 * tk, 2 * tk), lambda bi, hp, qi: (0, 0))],
        out_specs=pl.BlockSpec((1, tq, 2 * HEAD_DIM), lambda bi, hp, qi: (bi, qi, hp)),
        out_shape=SDS((b, t, W_A), f32),
        scratch_shapes=[pltpu.VMEM((t // tk, 2 * tk, 2 * HEAD_DIM), bf16),
                        pltpu.VMEM((t // tk, 2 * tk, 2 * HEAD_DIM), bf16),
                        pltpu.VMEM((tq, 2 * HEAD_DIM), f32)],
        compiler_params=_cparams(("arbitrary", "arbitrary", "arbitrary")),
        name="attn_prompt",
    )(sb_bias, q, k, v, u2)


def _attn_sample_kernel(pt_ref, bias_ref, qrow_ref, knew_ref, vnew_ref, uo_ref, *rest, n_new, pages):
    kp_refs = rest[:pages]
    vp_refs = rest[pages:2 * pages]
    o_ref, acc_ref, carry_ref = rest[2 * pages:]
    step = pl.program_id(1)
    n_row = H_A * n_new
    row_head = lax.broadcasted_iota(i32, (n_row, PAGE_SIZE), 0) // n_new
    bias = jnp.broadcast_to(bias_ref[...], (n_row, PAGE_SIZE))

    def add_values(w, value_of_head, dims):
        for h in range(H_A):
            acc_ref[...] += _dg(jnp.where(row_head[:, :w.shape[1]] == h, w, 0.0).astype(bf16),
                                value_of_head(h), dims)

    @pl.when(step == 0)
    def _():
        pad = knew_ref.shape[1]
        knew = knew_ref[0].astype(bf16)
        vnew = vnew_ref[0].astype(bf16)
        head_cols = lambda x, h: x[:, h * HEAD_DIM:(h + 1) * HEAD_DIM]
        z = sum(_dg(qrow_ref[0, h], head_cols(knew, h), _NT) for h in range(H_A)) + bias[:, :pad]
        s_idx = lax.broadcasted_iota(i32, (n_row, pad), 1)
        t_idx = lax.broadcasted_iota(i32, (n_row, pad), 0) % n_new
        mask = s_idx < t_idx
        sp = jnp.where(mask, _softplus2(z), 0.0)
        incl = jnp.zeros_like(sp)
        for j in range(n_new):
            incl = incl + jnp.where(s_idx <= j, sp[:, j:j + 1], 0.0)
        w = jnp.where(mask, jnp.exp2(z - incl), 0.0)
        acc_ref[...] = jnp.zeros_like(acc_ref)
        add_values(w, lambda h: head_cols(vnew, h), _NN)
        carry_ref[...] = jnp.broadcast_to(incl[:, 0:1], carry_ref.shape)

    uo = uo_ref[...]
    head_t = lambda ref, h: ref[0, 0, h].astype(bf16)
    z = [sum(_dg(qrow_ref[0, h], head_t(kp_refs[i], h), _NN) for h in range(H_A)) + bias
         for i in range(pages)]
    sp = [_softplus2(z[i]) for i in range(pages)]
    cr = [jnp.dot(sp[i].astype(bf16), uo, preferred_element_type=f32) for i in range(pages)]
    carry = carry_ref[...]
    for i in range(pages):
        w = jnp.exp2(z[i] - cr[i][:, :PAGE_SIZE] - carry)
        add_values(w, lambda h, i=i: head_t(vp_refs[i], h), _NT)
        carry = carry + cr[i][:, PAGE_SIZE:]
    carry_ref[...] = carry

    @pl.when(step == pl.num_programs(1) - 1)
    def _():
        o_ref[0] = acc_ref[...]


def _attn_sample(q, k_new, v_new, cache_k, cache_v, layer, page_table, sb_bias, n_new):
    b, pad, _ = q.shape
    n_pages = page_table.shape[1]
    pages = PAGES_PER_STEP
    n_row = H_A * n_new
    scale = LOG2E * HEAD_DIM ** -0.5
    qh = jnp.transpose(q[:, :n_new].reshape(b, n_new, H_A, HEAD_DIM) * scale, (0, 2, 1, 3))
    eye = jnp.eye(H_A, dtype=f32)
    qrow = (qh[:, :, None, :, :] * eye[None, :, :, None, None]).reshape(b, H_A, n_row, HEAD_DIM).astype(bf16)
    bias = jnp.repeat(sb_bias * LOG2E, n_new).reshape(n_row, 1)
    j = lax.broadcasted_iota(i32, (PAGE_SIZE, PAGE_SIZE), 0)
    s = lax.broadcasted_iota(i32, (PAGE_SIZE, PAGE_SIZE), 1)
    uo = jnp.concatenate([(j >= s).astype(bf16), jnp.ones((PAGE_SIZE, PAGE_SIZE), bf16)], axis=1)
    cache_k = jnp.transpose(cache_k, (0, 1, 3, 4, 2))
    cache_v = jnp.transpose(cache_v, (0, 1, 3, 4, 2))

    def page_spec(i):
        return pl.BlockSpec(
            (1, 1, H_A, HEAD_DIM, PAGE_SIZE),
            lambda bi, st, pt, i=i: (layer, pt[bi, n_pages - 1 - (st * pages + i)], 0, 0, 0))

    new_spec = pl.BlockSpec((1, pad, W_A), lambda bi, st, pt: (bi, 0, 0))
    grid_spec = pltpu.PrefetchScalarGridSpec(
        num_scalar_prefetch=1,
        grid=(b, n_pages // pages),
        in_specs=[pl.BlockSpec((n_row, 1), lambda bi, st, pt: (0, 0)),
                  pl.BlockSpec((1, H_A, n_row, HEAD_DIM), lambda bi, st, pt: (bi, 0, 0, 0)),
                  new_spec, new_spec,
                  pl.BlockSpec((PAGE_SIZE, 2 * PAGE_SIZE), lambda bi, st, pt: (0, 0))]
                 + [page_spec(i) for i in range(pages)] * 2,
        out_specs=pl.BlockSpec((1, n_row, HEAD_DIM), lambda bi, st, pt: (bi, 0, 0)),
        scratch_shapes=[pltpu.VMEM((n_row, HEAD_DIM), f32), pltpu.VMEM((n_row, PAGE_SIZE), f32)],
    )
    out = pl.pallas_call(
        functools.partial(_attn_sample_kernel, n_new=n_new, pages=pages),
        grid_spec=grid_spec,
        out_shape=SDS((b, n_row, HEAD_DIM), f32),
        compiler_params=_cparams(("arbitrary", "arbitrary")),
        name="attn_sample",
    )(page_table, bias, qrow, k_new, v_new, uo, *([cache_k] * pages), *([cache_v] * pages))
    return jnp.transpose(out.reshape(b, H_A, n_new, HEAD_DIM), (0, 2, 1, 3)).reshape(b, n_new, W_A)


def _prep_kernel(pb_ref, pf_ref, mu_ref, dbase_ref, dup_ref, ibase_ref, iup_ref, gup_ref,
                 kk_ref, ka_ref, rk_ref, hsum_ref,
                 r_out, lw_out, k_out, v_out, a_out, b_out, g_out, bonus_out, *, t_real):
    nb, tt, w = pb_ref.shape
    pb = pb_ref[...]
    tpos = lax.broadcasted_iota(i32, pb.shape, 1)
    prev = jnp.where(tpos == 0, pf_ref[:, 0], pltpu.roll(pb, 1, axis=1))
    x = (pb + (prev - pb) * mu_ref[...]).reshape(nb * tt, w)
    r = x[:, 0:W_B]
    k = x[:, W_B:2 * W_B]
    v = x[:, 2 * W_B:3 * W_B]
    o = 3 * W_B
    xw = x[:, o:o + DECAY_LORA]
    xa = x[:, o + DECAY_LORA:o + DECAY_LORA + ICLR_LORA]
    xg = x[:, o + DECAY_LORA + ICLR_LORA:]
    log_w = -_softplus(-(dbase_ref[...] + _bdot(jnp.tanh(xw), dup_ref[...]))) - 0.5
    lw = -jnp.exp(log_w)
    a = _sigmoid(ibase_ref[...] + _bdot(xa, iup_ref[...]))
    g = _bdot(_sigmoid(xg), gup_ref[...])
    hsum = hsum_ref[...]
    kkf = k * kk_ref[...]
    kk = kkf / jnp.maximum(jnp.sqrt(_split_dot(kkf * kkf, hsum)), 1e-12)
    k2 = k * (1.0 + (a - 1.0) * ka_ref[...])
    bonus = _split_dot(r * k2 * rk_ref[...], hsum) * v
    na = -kk
    kb = kk * a
    if t_real < tt:
        valid = (lax.broadcasted_iota(i32, (nb, tt, W_B), 1) < t_real).reshape(nb * tt, W_B)
        zero = lambda u: jnp.where(valid, u, 0.0)
        r, lw, k2, v, na, kb = zero(r), zero(lw), zero(k2), zero(v), zero(na), zero(kb)
    g_out[...] = g.reshape(nb, tt, W_B)
    bonus_out[...] = bonus.reshape(nb, tt, W_B)
    for val, ref in ((r, r_out), (lw, lw_out), (k2, k_out), (v, v_out), (na, a_out), (kb, b_out)):
        val = val.reshape(nb, tt, W_B)
        for h in range(H_B):
            ref[:, h, :, :] = val[:, :, h * HEAD_DIM:(h + 1) * HEAD_DIM]


def _rwkv_prep(pb, prev_first, p, nb, tt, t_real):
    b, t, w = pb.shape
    row = lambda i, j: (i, j, 0)
    const = lambda i, j: (0, 0)
    vec = lambda a: a.reshape(1, -1)
    head_of = jnp.arange(W_B) // HEAD_DIM
    hsum = (head_of[:, None] == head_of[None, :]).astype(bf16)
    params = [vec(p['mu_shift']), vec(p['decay_base']), p['decay_up'], vec(p['iclr_base']), p['iclr_up'],
              p['gate_up'], vec(p['k_k']), vec(p['k_a']), vec(p['r_k']), hsum]
    heads = pl.BlockSpec((nb, H_B, tt, HEAD_DIM), lambda i, j: (i, 0, j, 0))
    return pl.pallas_call(
        functools.partial(_prep_kernel, t_real=t_real),
        grid=(b // nb, t // tt),
        in_specs=[pl.BlockSpec((nb, tt, w), row),
                  pl.BlockSpec((nb, 1, 1, w), lambda i, j: (i, j, 0, 0))]
                 + [pl.BlockSpec(a.shape, const) for a in params],
        out_specs=[heads] * 6 + [pl.BlockSpec((nb, tt, W_B), row)] * 2,
        out_shape=[SDS((b, H_B, t, HEAD_DIM), f32)] * 6 + [SDS((b, t, W_B), f32)] * 2,
        compiler_params=_cparams(("arbitrary", "arbitrary")),
        name="rwkv_prep",
    )(pb, prev_first, *params)


def _wkv_kernel(r_ref, lw_ref, k_ref, v_ref, a_ref, b_ref, s0_ref, y_ref, s_out, st_ref, *, chunk):
    c = chunk
    n = HEAD_DIM
    step = pl.program_id(1)

    n_chain = lw_ref.shape[0] * H_B
    of = lambda ref, i: ref[i // H_B, i % H_B]

    @pl.when(step == 0)
    def _():
        for i in range(n_chain):
            st_ref[i] = of(s0_ref, i).T

    ti = lax.broadcasted_iota(i32, (c, c), 0)
    si = lax.broadcasted_iota(i32, (c, c), 1)
    tri = (ti >= si).astype(bf16)
    eye = (ti == si).astype(f32)
    ones = jnp.ones((c, n), bf16)
    row2 = lax.broadcasted_iota(i32, (c, 2 * c), 0)
    col2 = lax.broadcasted_iota(i32, (c, 2 * c), 1)
    strict2 = (col2 % c) < row2
    incl2 = (col2 % c) <= row2
    right = col2 >= c
    levels = max(c.bit_length() - 2, 0)
    heads = range(n_chain)
    each = lambda f: [f(h) for h in heads]
    lw = each(lambda h: of(lw_ref, h))
    lw3 = each(lambda h: _split3(lw[h]))
    cum = each(lambda h: sum(_dg(tri, t, _NN) for t in lw3[h]))
    wsum = each(lambda h: sum(_dg(t, ones, _TN) for t in lw3[h]))
    w_in = each(lambda h: jnp.exp(cum[h]))
    w_out = each(lambda h: jnp.exp(-cum[h]))
    at = each(lambda h: of(a_ref, h) * jnp.exp(cum[h] - lw[h]))
    rt = each(lambda h: of(r_ref, h) * w_in[h])
    bk = each(lambda h: jnp.concatenate([of(b_ref, h) * w_out[h], of(k_ref, h) * w_out[h]], axis=0))
    g = each(lambda h: _dot3(jnp.concatenate([at[h], rt[h]], axis=0), bk[h], _NT))
    top = each(lambda h: jnp.where(strict2, g[h][:c], 0.0))
    bot = each(lambda h: jnp.where(incl2, g[h][c:], 0.0))
    pw = each(lambda h: top[h][:, :c])
    inv = each(lambda h: eye + pw[h])
    for _ in range(levels):
        pw = each(lambda h: _dot3(pw[h], pw[h]))
        inv = each(lambda h: inv[h] + _dot3(inv[h], pw[h]))
    vv = each(lambda h: jnp.concatenate([of(v_ref, h), of(v_ref, h)], axis=0))
    xv = each(lambda h: _dot3(jnp.where(right, top[h], 0.0), vv[h]))
    x = each(lambda h: _dot3(at[h], st_ref[h]) + xv[h])
    u = each(lambda h: _dot3(inv[h], x[h]))
    uv = each(lambda h: jnp.concatenate([u[h], of(v_ref, h)], axis=0))
    y = each(lambda h: _bdot(bot[h], uv[h]) + _bdot(rt[h], st_ref[h]))
    st_new = each(lambda h: (st_ref[h] + _dot3(bk[h], uv[h], _TN)) * jnp.exp(wsum[h]))
    for h in heads:
        st_ref[h] = st_new[h]
        mu = jnp.mean(y[h], axis=-1, keepdims=True)
        yc = y[h] - mu
        var = jnp.mean(yc * yc, axis=-1, keepdims=True)
        col0 = (h % H_B) * HEAD_DIM
        y_ref[h // H_B, :, col0:col0 + HEAD_DIM] = yc * lax.rsqrt(var + LNX_EPS)

    @pl.when(step == pl.num_programs(1) - 1)
    def _():
        for i in range(n_chain):
            s_out[i // H_B, i % H_B] = st_ref[i].T


def _wkv_scan(r, lw, k, v, a, b, s0, chunk, nseq):
    bsz, h, t, n = r.shape
    seq = pl.BlockSpec((nseq, h, chunk, n), lambda i, j: (i, 0, j, 0))
    state = pl.BlockSpec((nseq, h, n, n), lambda i, j: (i, 0, 0, 0))
    return pl.pallas_call(
        functools.partial(_wkv_kernel, chunk=chunk),
        grid=(bsz // nseq, t // chunk),
        in_specs=[seq] * 6 + [state],
        out_specs=[pl.BlockSpec((nseq, chunk, h * n), lambda i, j: (i, j, 0)), state],
        out_shape=[SDS((bsz, t, h * n), f32), SDS((bsz, h, n, n), f32)],
        scratch_shapes=[pltpu.VMEM((nseq * h, n, n), f32)],
        compiler_params=_cparams(("arbitrary", "arbitrary")),
        name="wkv_scan",
    )(r, lw, k, v, a, b, s0)


def _outproj_kernel(attn_ref, yn_ref, bonus_ref, g_ref, x_ref, gt1_ref, sh2_ref, sc2_ref,
                    gattn_ref, lnxg_ref, lnxb_ref, wout_ref, gpost_ref, gpre_ref, rw_ref, rb_ref,
                    x1_ref, h2_ref, logit_ref):
    nb, tt, d = x_ref.shape
    o_a = _rms(attn_ref[...], gattn_ref[...])
    o_b = (yn_ref[...] * lnxg_ref[...] + lnxb_ref[...] + bonus_ref[...]) * g_ref[...]
    cat = jnp.concatenate([o_a, o_b], axis=-1).reshape(nb * tt, d)
    mixed = _bdot(cat, wout_ref[...]).reshape(nb, tt, d)
    x1 = x_ref[...] + gt1_ref[...] * _rms(mixed, gpost_ref[...])
    x1_ref[...] = x1
    h2 = _rms(x1, gpre_ref[...]) * (1.0 + sc2_ref[...]) + sh2_ref[...]
    h2_ref[...] = h2
    logits = _fdot(h2.reshape(nb * tt, d), rw_ref[...]) + rb_ref[...]
    logit_ref[...] = logits.reshape(nb, tt, N_EXPERTS)


def _out_proj(attn, yn, bonus, g, x, gt1, sh2, sc2, p, w_out_bf16, nb, tt):
    b, t, d = x.shape
    row = lambda w: pl.BlockSpec((nb, tt, w), lambda i, j: (i, j, 0))
    mod = pl.BlockSpec((nb, 1, d), lambda i, j: (i, 0, 0))
    vec = lambda a: a.reshape(1, -1)
    params = [vec(p['g_attn_out']), vec(p['lnx_g']), vec(p['lnx_b']), w_out_bf16,
              vec(p['g_mix_post']), vec(p['g_ffn_pre']), p['router_w'], vec(p['router_b'])]
    return pl.pallas_call(
        _outproj_kernel,
        grid=(b // nb, t // tt),
        in_specs=[row(W_A), row(W_B), row(W_B), row(W_B), row(d), mod, mod, mod]
                 + [pl.BlockSpec(a.shape, lambda i, j: (0, 0)) for a in params],
        out_specs=[row(d), row(d), row(N_EXPERTS)],
        out_shape=[SDS((b, t, d), f32), SDS((b, t, d), f32), SDS((b, t, N_EXPERTS), f32)],
        compiler_params=_cparams(("arbitrary", "arbitrary")),
        name="out_proj",
    )(attn, yn, bonus, g, x, gt1, sh2, sc2, *params)


def _route_kernel(logit_ref, idx_ref, gate_ref, rank_ref, count_ref, base_ref):
    tile = logit_ref.shape[0]

    @pl.when(pl.program_id(0) == 0)
    def _():
        base_ref[...] = jnp.zeros_like(base_ref)

    lane = lax.broadcasted_iota(i32, (tile, N_EXPERTS), 1).astype(f32)
    cur = logit_ref[...]
    hots, vals, idxs = [], [], []
    for _ in range(TOP_K):
        top = jnp.max(cur, axis=-1, keepdims=True)
        idx = jnp.min(jnp.where(cur == top, lane, float(N_EXPERTS)), axis=-1, keepdims=True)
        hot = lane == idx
        hots.append(hot)
        vals.append(top)
        idxs.append(idx)
        cur = jnp.where(hot, -jnp.inf, cur)
    es = [jnp.exp(v - vals[0]) for v in vals]
    total = sum(es)
    chosen = sum(h.astype(f32) for h in hots)
    ti = lax.broadcasted_iota(i32, (tile, tile), 0)
    si = lax.broadcasted_iota(i32, (tile, tile), 1)
    earlier = jnp.dot((ti > si).astype(bf16), chosen.astype(bf16), preferred_element_type=f32) + base_ref[...]
    ranks = [jnp.sum(jnp.where(h, earlier, 0.0), axis=-1, keepdims=True) for h in hots]
    base_ref[...] += jnp.sum(chosen, axis=0, keepdims=True)
    count_ref[...] = base_ref[...]
    col = lax.broadcasted_iota(i32, (tile, TOP_K), 1)
    pick = lambda parts: sum(jnp.where(col == k, parts[k], 0.0) for k in range(TOP_K))
    idx_ref[...] = pick(idxs).astype(i32)
    gate_ref[...] = pick([e / total for e in es])
    rank_ref[...] = pick(ranks).astype(i32)


def _route(logits, tile):
    m = logits.shape[0]
    tok = lambda dt: SDS((m, TOP_K), dt)
    blk = pl.BlockSpec((tile, TOP_K), lambda i: (i, 0))
    return pl.pallas_call(
        _route_kernel,
        grid=(m // tile,),
        in_specs=[pl.BlockSpec((tile, N_EXPERTS), lambda i: (i, 0))],
        out_specs=[blk, blk, blk, pl.BlockSpec((1, N_EXPERTS), lambda i: (0, 0))],
        out_shape=[tok(i32), tok(f32), tok(i32), SDS((1, N_EXPERTS), f32)],
        scratch_shapes=[pltpu.VMEM((1, N_EXPERTS), f32)],
        compiler_params=_cparams(("arbitrary",)),
        name="moe_route",
    )(logits)


def _row_copy(src, src_row, dst, dst_row, sem):
    return pltpu.make_async_copy(src.at[pl.ds(src_row, 1)], dst.at[pl.ds(dst_row, 1)], sem)


def _dispatch_kernel(dest_ref, xa_ref, xb_ref, xs_in, xs_out, sem, *, n_first):
    del xs_in
    tile = xa_ref.shape[0]

    def scatter_rows(x_ref):
        def start(r, _):
            for k in range(TOP_K):
                _row_copy(x_ref, r, xs_out, dest_ref[k, r], sem).start()
            return 0

        lax.fori_loop(0, tile, start, 0, unroll=DMA_ISSUE_UNROLL)
        for k in range(TOP_K):
            pltpu.make_async_copy(x_ref, xs_out.at[pl.ds(0, tile)], sem).wait()

    @pl.when(pl.program_id(0) < n_first)
    def _():
        scatter_rows(xa_ref)

    @pl.when(pl.program_id(0) >= n_first)
    def _():
        scatter_rows(xb_ref)


def _dispatch(xa, xb, dest_t, n_slots, tile):
    d = xa.shape[1]
    n_first = xa.shape[0] // tile
    return pl.pallas_call(
        functools.partial(_dispatch_kernel, n_first=n_first),
        grid=(n_first + xb.shape[0] // tile,),
        in_specs=[pl.BlockSpec((TOP_K, tile), lambda i: (0, i), memory_space=pltpu.SMEM),
                  pl.BlockSpec((tile, d), lambda i: (jnp.minimum(i, n_first - 1), 0)),
                  pl.BlockSpec((tile, d), lambda i: (jnp.maximum(i - n_first, 0), 0)),
                  pl.BlockSpec(memory_space=pl.ANY)],
        out_specs=pl.BlockSpec(memory_space=pl.ANY),
        out_shape=SDS((n_slots, d), xa.dtype),
        scratch_shapes=[pltpu.SemaphoreType.DMA(())],
        input_output_aliases={3: 0},
        compiler_params=_cparams(("arbitrary",)),
        name="moe_dispatch",
    )(dest_t, xa, xb, jnp.zeros((n_slots, d), xa.dtype))


def _combine_kernel(dest_ref, gate_ref, ys_hbm, fa_ref, fb_ref, buf_ref, sem, *, n_first):
    tile = fa_ref.shape[0]

    def start(r, _):
        for k in range(TOP_K):
            _row_copy(ys_hbm, dest_ref[k, r], buf_ref.at[k], r, sem).start()
        return 0

    lax.fori_loop(0, tile, start, 0, unroll=DMA_ISSUE_UNROLL)
    for k in range(TOP_K):
        pltpu.make_async_copy(ys_hbm.at[pl.ds(0, tile)], buf_ref.at[k], sem).wait()
    gate = gate_ref[...]
    f = sum(gate[:, k:k + 1] * buf_ref[k] for k in range(TOP_K))

    @pl.when(pl.program_id(0) < n_first)
    def _():
        fa_ref[...] = f

    @pl.when(pl.program_id(0) >= n_first)
    def _():
        fb_ref[...] = f


def _combine(ys, dest_t, gate, m_first, tile):
    m = gate.shape[0]
    d = ys.shape[1]
    n_first = m_first // tile
    return pl.pallas_call(
        functools.partial(_combine_kernel, n_first=n_first),
        grid=(m // tile,),
        in_specs=[pl.BlockSpec((TOP_K, tile), lambda i: (0, i), memory_space=pltpu.SMEM),
                  pl.BlockSpec((tile, TOP_K), lambda i: (i, 0)),
                  pl.BlockSpec(memory_space=pl.ANY)],
        out_specs=[pl.BlockSpec((tile, d), lambda i: (jnp.minimum(i, n_first - 1), 0)),
                   pl.BlockSpec((tile, d), lambda i: (jnp.maximum(i - n_first, 0), 0))],
        out_shape=[SDS((m_first, d), f32), SDS((m - m_first, d), f32)],
        scratch_shapes=[pltpu.VMEM((TOP_K, tile, d), f32), pltpu.SemaphoreType.DMA(())],
        compiler_params=_cparams(("arbitrary",)),
        name="moe_combine",
    )(dest_t, gate, ys)


def _expert_kernel(be_ref, nused_ref, x_ref, w1_ref, b1_ref, w2_ref, b2_ref, y_ref, w1b_ref, w2b_ref):
    i = pl.program_id(0)
    prev = be_ref[jnp.maximum(i - 1, 0)]

    @pl.when((i == 0) | (be_ref[i] != prev))
    def _():
        w1b_ref[...] = w1_ref[0].astype(bf16)
        w2b_ref[...] = w2_ref[0].astype(bf16)

    @pl.when(i < nused_ref[0])
    def _():
        u = jnp.dot(x_ref[...].astype(bf16), w1b_ref[...], preferred_element_type=f32) + b1_ref[0]
        u_glu = jnp.minimum(u[:, :D_FF], SWIGLU_LIMIT)
        u_lin = jnp.clip(u[:, D_FF:], -SWIGLU_LIMIT, SWIGLU_LIMIT)
        act = u_glu * _sigmoid(SWIGLU_ALPHA * u_glu) * (u_lin + 1.0)
        y_ref[...] = jnp.dot(act.astype(bf16), w2b_ref[...], preferred_element_type=f32) + b2_ref[0]

    @pl.when(i >= nused_ref[0])
    def _():
        y_ref[...] = jnp.zeros_like(y_ref)


def _expert_ffn(xs, blk_expert, n_used, e_w1, e_b1, e_w2, e_b2, tm):
    n_slots, d = xs.shape
    n_blocks = n_slots // tm
    ex = lambda i, be, nu: (be[i], 0, 0)
    grid_spec = pltpu.PrefetchScalarGridSpec(
        num_scalar_prefetch=2,
        grid=(n_blocks,),
        in_specs=[pl.BlockSpec((tm, d), lambda i, be, nu: (i, 0)),
                  pl.BlockSpec((1, d, 2 * D_FF), ex),
                  pl.BlockSpec((1, 1, 2 * D_FF), ex),
                  pl.BlockSpec((1, D_FF, d), ex),
                  pl.BlockSpec((1, 1, d), ex)],
        out_specs=pl.BlockSpec((tm, d), lambda i, be, nu: (i, 0)),
        scratch_shapes=[pltpu.VMEM((d, 2 * D_FF), bf16), pltpu.VMEM((D_FF, d), bf16)],
    )
    return pl.pallas_call(
        _expert_kernel,
        grid_spec=grid_spec,
        out_shape=SDS((n_slots, d), f32),
        compiler_params=_cparams(("arbitrary",)),
        name="expert_ffn",
    )(blk_expert, n_used, xs, e_w1, e_b1.reshape(N_EXPERTS, 1, -1), e_w2, e_b2.reshape(N_EXPERTS, 1, -1))


def _moe(h2_a, logits_a, h2_b, logits_b, e_w1, e_b1, e_w2, e_b2, tm):
    tile = MOE_TOKEN_TILE
    m_a, m_b = h2_a.shape[0], h2_b.shape[0]
    assert m_a % tile == 0
    extra = -m_b % tile
    h2_b = jnp.pad(h2_b, ((0, extra), (0, 0)))
    logits = jnp.concatenate([logits_a, logits_b, jnp.zeros((extra, N_EXPERTS), f32)], axis=0)
    m = m_a + m_b + extra
    top_idx, gate, rank, counts = _route(logits, tile)
    counts = counts.reshape(N_EXPERTS).astype(i32)
    padded = (counts + tm - 1) // tm * tm
    pad_end = jnp.cumsum(padded)
    pad_start = pad_end - padded
    n_blocks = (m * TOP_K + N_EXPERTS * (tm - 1) + tm - 1) // tm
    blk_row = (jnp.arange(n_blocks) * tm)[:, None]
    blk_expert = jnp.minimum(jnp.sum(pad_end[None, :] <= blk_row, axis=1), N_EXPERTS - 1).astype(i32)
    n_used = (pad_end[-1] // tm).astype(i32).reshape(1)
    group_start = jnp.sum(jnp.where(top_idx[:, :, None] == jnp.arange(N_EXPERTS), pad_start, 0), axis=-1)
    dest_t = (group_start + rank).astype(i32).T
    xs = _dispatch(h2_a, h2_b, dest_t, n_blocks * tm, tile)
    ys = _expert_ffn(xs, blk_expert, n_used, e_w1, e_b1, e_w2, e_b2, tm)
    f_a, f_b = _combine(ys, dest_t, gate, m_a, tile)
    return f_a, f_b[:m_b]


def _final_kernel(x1_ref, f_ref, gt2_ref, g_ref, o_ref):
    o_ref[...] = x1_ref[...] + gt2_ref[...] * _rms(f_ref[...], g_ref[...])


def _final(x1, f, gt2, g, nb, tt):
    b, t, d = x1.shape
    row = pl.BlockSpec((nb, tt, d), lambda i, j: (i, j, 0))
    return pl.pallas_call(
        _final_kernel,
        grid=(b // nb, t // tt),
        in_specs=[row, row, pl.BlockSpec((nb, 1, d), lambda i, j: (i, 0, 0)),
                  pl.BlockSpec((1, d), lambda i, j: (0, 0))],
        out_specs=row,
        out_shape=SDS((b, t, d), f32),
        compiler_params=_cparams(("arbitrary", "arbitrary")),
        name="final_residual",
    )(x1, f, gt2, g.reshape(1, d))


def _mix_layer(x, mod, attend, shift_prev, wkv_prev, p, w_in_bf16, w_out_bf16, nb, tt, t_real, chunk, wkv_nseq):
    b, t, d = x.shape
    sh1, sc1, gt1, sh2, sc2, gt2 = [mod[:, i:i + 1, :] for i in range(6)]
    q, k, v, pb, last = _in_proj(x, sh1, sc1, p['g_mix_pre'], w_in_bf16, nb, tt, t_real)
    attn = attend(q, k, v)
    prev_first = jnp.concatenate([shift_prev[:, None, None, :], last[:, :-1]], axis=1)
    r, lw, k2, v2, a, kb, g, bonus = _rwkv_prep(pb, prev_first, p, nb, tt, t_real)
    yn, wkv_new = _wkv_scan(r, lw, k2, v2, a, kb, wkv_prev, chunk, wkv_nseq)
    x1, h2, logits = _out_proj(attn, yn, bonus, g, x, gt1, sh2, sc2, p, w_out_bf16, nb, tt)
    return x1, h2, logits, gt2, k, v, wkv_new, last[:, -1, 0]


def kernel(x_prompt, x_sample, cache_k, cache_v, state_wkv, state_shift, page_table, c_prompt, c_sample,
           w_ada, b_ada, g_mix_pre, g_mix_post, g_ffn_pre, g_ffn_post, w_in, mu_shift,
           decay_base, decay_up, iclr_base, iclr_up, gate_up, k_k, k_a, r_k, lnx_g, lnx_b,
           g_attn_out, sb_bias, w_out, router_w, router_b, e_w1, e_b1, e_w2, e_b2):
    weights = dict(w_ada=w_ada, b_ada=b_ada, g_mix_pre=g_mix_pre, g_mix_post=g_mix_post,
                   g_ffn_pre=g_ffn_pre, g_ffn_post=g_ffn_post, w_in=w_in, mu_shift=mu_shift,
                   decay_base=decay_base, decay_up=decay_up, iclr_base=iclr_base, iclr_up=iclr_up,
                   gate_up=gate_up, k_k=k_k, k_a=k_a, r_k=r_k, lnx_g=lnx_g, lnx_b=lnx_b,
                   g_attn_out=g_attn_out, sb_bias=sb_bias, w_out=w_out, router_w=router_w,
                   router_b=router_b, e_w1=e_w1, e_b1=e_b1, e_w2=e_w2, e_b2=e_b2)
    depth = w_ada.shape[0]
    bp, tp, d = x_prompt.shape
    bs, ts, _ = x_sample.shape
    ts_pad = -(-ts // V7X_SUBLANES) * V7X_SUBLANES
    hp = x_prompt
    hs = jnp.pad(x_sample, ((0, 0), (0, ts_pad - ts), (0, 0)))
    outs = [[] for _ in range(8)]
    for l in range(depth):
        p = {name: w[l] for name, w in weights.items()}
        w_in_bf16 = p['w_in'].astype(bf16)
        w_out_bf16 = p['w_out'].astype(bf16)
        mod = _ada_mod(jnp.concatenate([c_prompt, c_sample], axis=0), p['w_ada'], p['b_ada'])
        mod = mod.reshape(bp + bs, 6, d)

        attend_p = lambda q, k, v: _attn_prompt(q, k, v, p['sb_bias'])
        x1p, h2p, lgp, gt2p, kp, vp, wp, shp = _mix_layer(
            hp, mod[:bp], attend_p, jnp.zeros((bp, SHIFT_W), f32),
            jnp.zeros((bp, H_B, HEAD_DIM, HEAD_DIM), f32), p, w_in_bf16, w_out_bf16,
            nb=1, tt=ROW_TILE, t_real=tp, chunk=WKV_CHUNK, wkv_nseq=WKV_NSEQ)

        attend_s = lambda q, k, v: jnp.pad(
            _attn_sample(q, k, v, cache_k, cache_v, l, page_table, p['sb_bias'], ts),
            ((0, 0), (0, ts_pad - ts), (0, 0)))
        x1s, h2s, lgs, gt2s, ks, vs, ws, shs = _mix_layer(
            hs, mod[bp:], attend_s, state_shift[l], state_wkv[l], p, w_in_bf16, w_out_bf16,
            nb=bs, tt=ts_pad, t_real=ts, chunk=ts_pad, wkv_nseq=WKV_NSEQ)

        mp = bp * tp
        fp, fs = _moe(h2p.reshape(mp, d), lgp.reshape(mp, N_EXPERTS),
                      h2s[:, :ts].reshape(bs * ts, d), lgs[:, :ts].reshape(bs * ts, N_EXPERTS),
                      p['e_w1'], p['e_b1'], p['e_w2'], p['e_b2'], MOE_TILE)
        fp = fp.reshape(bp, tp, d)
        fs = jnp.pad(fs.reshape(bs, ts, d), ((0, 0), (0, ts_pad - ts), (0, 0)))
        hp = _final(x1p, fp, gt2p, p['g_ffn_post'], 1, ROW_TILE)
        hs = _final(x1s, fs, gt2s, p['g_ffn_post'], bs, ts_pad)

        for lst, val in zip(outs, (kp.reshape(bp, tp, H_A, HEAD_DIM), vp.reshape(bp, tp, H_A, HEAD_DIM),
                                   ks[:, :ts].reshape(bs, ts, H_A, HEAD_DIM),
                                   vs[:, :ts].reshape(bs, ts, H_A, HEAD_DIM), wp, ws, shp, shs)):
            lst.append(val)
    return (hp, hs[:, :ts]) + tuple(jnp.stack(lst) for lst in outs)
```

```python
import functools
import math

import jax
import jax.numpy as jnp
from jax import lax
from jax.experimental import pallas as pl
from jax.experimental.pallas import tpu as pltpu

f32 = jnp.float32
bf16 = jnp.bfloat16
i32 = jnp.int32
SDS = jax.ShapeDtypeStruct

D_MODEL = 1024
HEAD_DIM = 64
W_A = 512
W_B = 512
H_A = W_A // HEAD_DIM
H_B = W_B // HEAD_DIM
DECAY_LORA = 64
ICLR_LORA = 64
GATE_LORA = 128
SHIFT_W = 3 * W_B + DECAY_LORA + ICLR_LORA + GATE_LORA
N_EXPERTS = 32
TOP_K = 4
D_FF = D_MODEL
SWIGLU_ALPHA = 1.702
SWIGLU_LIMIT = 7.0
PAGE_SIZE = 128
RMS_EPS = 1e-6
LNX_EPS = 64e-5
LOG2E = 1.4426950408889634

V7X_SUBLANES = 8
V7X_LANES = 128
VMEM_LIMIT = 56 * 1024 * 1024

ROW_TILE = 256
WKV_CHUNK = 64
WKV_NSEQ = 2
ATTN_TQ = 512
ATTN_TK = 128
ATTN_NSUB = 4
PAGES_PER_STEP = 16
MOE_TILE = 256
MOE_TOKEN_TILE = 128
DMA_ISSUE_UNROLL = 8


def _cparams(sem):
    return pltpu.CompilerParams(dimension_semantics=sem, vmem_limit_bytes=VMEM_LIMIT)


def _bdot(a, b):
    return jnp.dot(a.astype(bf16), b.astype(bf16), preferred_element_type=f32)


def _fdot(a, b, dims=(((1,), (0,)), ((), ()))):
    return lax.dot_general(a, b, dims, precision=lax.Precision.HIGHEST, preferred_element_type=f32)


_NT = (((1,), (1,)), ((), ()))
_TN = (((0,), (0,)), ((), ()))


def _split_dot(x, m_bf16):
    hi = x.astype(bf16)
    lo = (x - hi.astype(f32)).astype(bf16)
    return (jnp.dot(hi, m_bf16, preferred_element_type=f32)
            + jnp.dot(lo, m_bf16, preferred_element_type=f32))


def _sigmoid(x):
    return 1.0 / (1.0 + jnp.exp(-x))


def _softplus(x):
    return jnp.maximum(x, 0.0) + jnp.log(1.0 + jnp.exp(-jnp.abs(x)))


def _softplus2(z):
    return jnp.maximum(z, 0.0) + jnp.log2(1.0 + jnp.exp2(-jnp.abs(z)))


def _split2(x):
    hi = x.astype(bf16)
    return hi, (x - hi.astype(f32)).astype(bf16)


def _split3(x):
    hi = x.astype(bf16)
    r = x - hi.astype(f32)
    mid = r.astype(bf16)
    return hi, mid, (r - mid.astype(f32)).astype(bf16)


def _dg(a, b, dims):
    return lax.dot_general(a, b, dims, preferred_element_type=f32)


_NN = (((1,), (0,)), ((), ()))


def _dot3(a, b, dims=_NN):
    ah, al = _split2(a)
    bh, bl = _split2(b)
    return _dg(ah, bh, dims) + _dg(ah, bl, dims) + _dg(al, bh, dims)


def _dot_exact_rhs(x3, m_bf16, dims=_NN):
    return _dg(x3[0], m_bf16, dims) + _dg(x3[1], m_bf16, dims) + _dg(x3[2], m_bf16, dims)


def _rms(x, g):
    return x * lax.rsqrt(jnp.mean(x * x, axis=-1, keepdims=True) + RMS_EPS) * g


def _store_token_tiles(ref, first_row, x2d):
    rows = x2d.shape[0]
    flat = ref.reshape(math.prod(ref.shape) // V7X_LANES, V7X_LANES)
    for c in range(V7X_SUBLANES):
        flat[pl.ds(first_row * V7X_SUBLANES + c, rows, stride=V7X_SUBLANES), :] = (
            x2d[:, c * V7X_LANES:(c + 1) * V7X_LANES])


def _load_token_tiles(ref, first_row, rows):
    flat = ref.reshape(math.prod(ref.shape) // V7X_LANES, V7X_LANES)
    return jnp.concatenate([flat[pl.ds(first_row * V7X_SUBLANES + c, rows, stride=V7X_SUBLANES), :]
                            for c in range(V7X_SUBLANES)], axis=1)


def _ada_kernel(c_ref, w_ref, b_ref, o_ref):
    c = c_ref[...]
    o_ref[...] = _bdot(c * _sigmoid(c), w_ref[...]) + b_ref[...]


def _ada_mod(c, w_ada, b_ada):
    n, d = c.shape
    nout = w_ada.shape[1]
    tn = 1536
    return pl.pallas_call(
        _ada_kernel,
        grid=(nout // tn,),
        in_specs=[pl.BlockSpec((n, d), lambda j: (0, 0)),
                  pl.BlockSpec((d, tn), lambda j: (0, j)),
                  pl.BlockSpec((1, tn), lambda j: (0, j))],
        out_specs=pl.BlockSpec((n, tn), lambda j: (0, j)),
        out_shape=SDS((n, nout), f32),
        compiler_params=_cparams(("arbitrary",)),
        name="ada_mod",
    )(c, w_ada, b_ada.reshape(1, nout))


def _inproj_kernel(x_ref, sh_ref, sc_ref, g_ref, w_ref, q_ref, k_ref, v_ref, pb_ref, last_ref, *, t_last):
    nb, tt, d = x_ref.shape
    h = _rms(x_ref[...], g_ref[...]) * (1.0 + sc_ref[...]) + sh_ref[...]
    hb = h.reshape(nb * tt, d).astype(bf16)
    q_ref[...] = jnp.dot(hb, w_ref[:, 0:W_A], preferred_element_type=f32).reshape(nb, tt, W_A)
    k_ref[...] = jnp.dot(hb, w_ref[:, W_A:2 * W_A], preferred_element_type=f32).reshape(nb, tt, W_A)
    v_ref[...] = jnp.dot(hb, w_ref[:, 2 * W_A:3 * W_A], preferred_element_type=f32).reshape(nb, tt, W_A)
    pb = jnp.dot(hb, w_ref[:, 3 * W_A:], preferred_element_type=f32).reshape(nb, tt, SHIFT_W)
    pb_ref[...] = pb
    last_ref[:, 0] = pb[:, t_last:t_last + 1, :]


def _in_proj(x, sh1, sc1, g, w_in_bf16, nb, tt, t_real):
    b, t, d = x.shape
    row = lambda w: pl.BlockSpec((nb, tt, w), lambda i, j: (i, j, 0))
    mod = pl.BlockSpec((nb, 1, d), lambda i, j: (i, 0, 0))
    return pl.pallas_call(
        functools.partial(_inproj_kernel, t_last=min(tt, t_real) - 1),
        grid=(b // nb, t // tt),
        in_specs=[row(d), mod, mod,
                  pl.BlockSpec((1, d), lambda i, j: (0, 0)),
                  pl.BlockSpec(w_in_bf16.shape, lambda i, j: (0, 0))],
        out_specs=[row(W_A), row(W_A), row(W_A), row(SHIFT_W),
                   pl.BlockSpec((nb, 1, 1, SHIFT_W), lambda i, j: (i, j, 0, 0))],
        out_shape=[SDS((b, t, W_A), f32)] * 3 + [SDS((b, t, SHIFT_W), f32),
                                                  SDS((b, t // tt, 1, SHIFT_W), f32)],
        compiler_params=_cparams(("arbitrary", "arbitrary")),
        name="in_proj",
    )(x, sh1, sc1, g.reshape(1, d), w_in_bf16)


def _attn_prompt_kernel(bias_ref, q_ref, k_ref, v_ref, u2_ref, o_ref, kbd_ref, vbd_ref, acc_ref,
                        *, tq, tk, nsub):
    hp = pl.program_id(1)
    qi = pl.program_id(2)
    two = 2 * tk
    big = tk * nsub
    ratio = tq // big
    n_tiles = k_ref.shape[1] // tk

    @pl.when(qi == 0)
    def _():
        first = lax.broadcasted_iota(i32, (tk, 2 * HEAD_DIM), 1) < HEAD_DIM

        def build(j, _):
            start = pl.multiple_of(j * tk, tk)
            kt = k_ref[0, pl.ds(start, tk), :]
            vt = v_ref[0, pl.ds(start, tk), :]
            kbd_ref[j, 0:tk, :] = jnp.where(first, kt, 0.0).astype(bf16)
            kbd_ref[j, tk:two, :] = jnp.where(first, 0.0, kt).astype(bf16)
            vbd_ref[j, 0:tk, :] = jnp.where(first, vt, 0.0).astype(bf16)
            vbd_ref[j, tk:two, :] = jnp.where(first, 0.0, vt).astype(bf16)
            return 0

        lax.fori_loop(0, n_tiles, build, 0)

    qb = (q_ref[0] * (LOG2E * HEAD_DIM ** -0.5)).astype(bf16)
    lane2 = lax.broadcasted_iota(i32, (1, two), 1)
    bias2 = jnp.where(lane2 < tk, bias_ref[2 * hp], bias_ref[2 * hp + 1]) * LOG2E
    bias_row = jnp.concatenate([bias2] * nsub, axis=1)
    u2 = u2_ref[...]
    acc_ref[...] = jnp.zeros_like(acc_ref)
    row = lax.broadcasted_iota(i32, (tq, nsub * two), 0)
    col = lax.broadcasted_iota(i32, (tq, nsub * two), 1)
    key_off = (col // two) * tk + col % tk

    def chunk(tile0, mask, carry):
        kb = kbd_ref[pl.ds(tile0, nsub)].reshape(nsub * two, 2 * HEAD_DIM)
        vb = vbd_ref[pl.ds(tile0, nsub)].reshape(nsub * two, 2 * HEAD_DIM)
        z = _dg(qb, kb, _NT) + bias_row
        sp = _softplus2(z)
        if mask is not None:
            sp = jnp.where(mask, sp, 0.0)
        spb = sp.astype(bf16)
        ws = [None] * nsub
        for j in range(nsub - 1, -1, -1):
            incl = jnp.dot(spb[:, j * two:(j + 1) * two], u2, preferred_element_type=f32)
            ws[j] = jnp.exp2(z[:, j * two:(j + 1) * two] - incl - carry)
            carry = carry + jnp.concatenate([jnp.broadcast_to(incl[:, 0:1], (tq, tk)),
                                             jnp.broadcast_to(incl[:, tk:tk + 1], (tq, tk))], axis=1)
        w = jnp.concatenate(ws, axis=1)
        if mask is not None:
            w = jnp.where(mask, w, 0.0)
        acc_ref[...] += jnp.dot(w.astype(bf16), vb, preferred_element_type=f32)
        return carry

    carry = jnp.zeros((tq, two), f32)
    for dgl in range(ratio - 1, -1, -1):
        carry = chunk(qi * (tq // tk) + dgl * nsub, (key_off + dgl * big) < row, carry)

    def body(n, carry):
        return chunk((qi * ratio - 1 - n) * nsub, None, carry)

    lax.fori_loop(0, qi * ratio, body, carry)
    o_ref[0] = acc_ref[...]


def _attn_prompt(q, k, v, sb_bias):
    b, t, _ = q.shape
    tq, tk, nsub = ATTN_TQ, ATTN_TK, ATTN_NSUB
    j = lax.broadcasted_iota(i32, (2 * tk, 2 * tk), 0)
    s = lax.broadcasted_iota(i32, (2 * tk, 2 * tk), 1)
    u2 = ((j >= s) & ((j // tk) == (s // tk))).astype(bf16)
    seq = pl.BlockSpec((1, t, 2 * HEAD_DIM), lambda bi, hp, qi: (bi, 0, hp))
    return pl.pallas_call(
        functools.partial(_attn_prompt_kernel, tq=tq, tk=tk, nsub=nsub),
        grid=(b, H_A // 2, t // tq),
        in_specs=[pl.BlockSpec(memory_space=pltpu.SMEM),
                  pl.BlockSpec((1, tq, 2 * HEAD_DIM), lambda bi, hp, qi: (bi, qi, hp)),
                  seq, seq,
                  pl.BlockSpec((2 * tk, 2 * tk), lambda bi, hp, qi: (0, 0))],
        out_specs=pl.BlockSpec((1, tq, 2 * HEAD_DIM), lambda bi, hp, qi: (bi, qi, hp)),
        out_shape=SDS((b, t, W_A), f32),
        scratch_shapes=[pltpu.VMEM((t // tk, 2 * tk, 2 * HEAD_DIM), bf16),
                        pltpu.VMEM((t // tk, 2 * tk, 2 * HEAD_DIM), bf16),
                        pltpu.VMEM((tq, 2 * HEAD_DIM), f32)],
        compiler_params=_cparams(("arbitrary", "arbitrary", "arbitrary")),
        name="attn_prompt",
    )(sb_bias, q, k, v, u2)


def _attn_sample_kernel(pt_ref, bias_ref, qrow_ref, knew_ref, vnew_ref, uo_ref, *rest, n_new, pages):
    kp_refs = rest[:pages]
    vp_refs = rest[pages:2 * pages]
    o_ref, acc_ref, carry_ref = rest[2 * pages:]
    step = pl.program_id(1)
    n_row = H_A * n_new
    row_head = lax.broadcasted_iota(i32, (n_row, PAGE_SIZE), 0) // n_new
    bias = jnp.broadcast_to(bias_ref[...], (n_row, PAGE_SIZE))

    def add_values(w, value_of_head, dims):
        for h in range(H_A):
            acc_ref[...] += _dg(jnp.where(row_head[:, :w.shape[1]] == h, w, 0.0).astype(bf16),
                                value_of_head(h), dims)

    @pl.when(step == 0)
    def _():
        pad = knew_ref.shape[1]
        knew = knew_ref[0].astype(bf16)
        vnew = vnew_ref[0].astype(bf16)
        head_cols = lambda x, h: x[:, h * HEAD_DIM:(h + 1) * HEAD_DIM]
        z = sum(_dg(qrow_ref[0, h], head_cols(knew, h), _NT) for h in range(H_A)) + bias[:, :pad]
        s_idx = lax.broadcasted_iota(i32, (n_row, pad), 1)
        t_idx = lax.broadcasted_iota(i32, (n_row, pad), 0) % n_new
        mask = s_idx < t_idx
        sp = jnp.where(mask, _softplus2(z), 0.0)
        incl = jnp.zeros_like(sp)
        for j in range(n_new):
            incl = incl + jnp.where(s_idx <= j, sp[:, j:j + 1], 0.0)
        w = jnp.where(mask, jnp.exp2(z - incl), 0.0)
        acc_ref[...] = jnp.zeros_like(acc_ref)
        add_values(w, lambda h: head_cols(vnew, h), _NN)
        carry_ref[...] = jnp.broadcast_to(incl[:, 0:1], carry_ref.shape)

    uo = uo_ref[...]
    head_t = lambda ref, h: ref[0, 0, h].astype(bf16)
    z = [sum(_dg(qrow_ref[0, h], head_t(kp_refs[i], h), _NN) for h in range(H_A)) + bias
         for i in range(pages)]
    sp = [_softplus2(z[i]) for i in range(pages)]
    cr = [jnp.dot(sp[i].astype(bf16), uo, preferred_element_type=f32) for i in range(pages)]
    carry = carry_ref[...]
    for i in range(pages):
        w = jnp.exp2(z[i] - cr[i][:, :PAGE_SIZE] - carry)
        add_values(w, lambda h, i=i: head_t(vp_refs[i], h), _NT)
        carry = carry + cr[i][:, PAGE_SIZE:]
    carry_ref[...] = carry

    @pl.when(step == pl.num_programs(1) - 1)
    def _():
        o_ref[0] = acc_ref[...]


def _attn_sample(q, k_new, v_new, cache_k, cache_v, layer, page_table, sb_bias, n_new):
    b, pad, _ = q.shape
    n_pages = page_table.shape[1]
    pages = PAGES_PER_STEP
    n_row = H_A * n_new
    scale = LOG2E * HEAD_DIM ** -0.5
    qh = jnp.transpose(q[:, :n_new].reshape(b, n_new, H_A, HEAD_DIM) * scale, (0, 2, 1, 3))
    eye = jnp.eye(H_A, dtype=f32)
    qrow = (qh[:, :, None, :, :] * eye[None, :, :, None, None]).reshape(b, H_A, n_row, HEAD_DIM).astype(bf16)
    bias = jnp.repeat(sb_bias * LOG2E, n_new).reshape(n_row, 1)
    j = lax.broadcasted_iota(i32, (PAGE_SIZE, PAGE_SIZE), 0)
    s = lax.broadcasted_iota(i32, (PAGE_SIZE, PAGE_SIZE), 1)
    uo = jnp.concatenate([(j >= s).astype(bf16), jnp.ones((PAGE_SIZE, PAGE_SIZE), bf16)], axis=1)
    cache_k = jnp.transpose(cache_k, (0, 1, 3, 4, 2))
    cache_v = jnp.transpose(cache_v, (0, 1, 3, 4, 2))

    def page_spec(i):
        return pl.BlockSpec(
            (1, 1, H_A, HEAD_DIM, PAGE_SIZE),
            lambda bi, st, pt, i=i: (layer, pt[bi, n_pages - 1 - (st * pages + i)], 0, 0, 0))

    new_spec = pl.BlockSpec((1, pad, W_A), lambda bi, st, pt: (bi, 0, 0))
    grid_spec = pltpu.PrefetchScalarGridSpec(
        num_scalar_prefetch=1,
        grid=(b, n_pages // pages),
        in_specs=[pl.BlockSpec((n_row, 1), lambda bi, st, pt: (0, 0)),
                  pl.BlockSpec((1, H_A, n_row, HEAD_DIM), lambda bi, st, pt: (bi, 0, 0, 0)),
                  new_spec, new_spec,
                  pl.BlockSpec((PAGE_SIZE, 2 * PAGE_SIZE), lambda bi, st, pt: (0, 0))]
                 + [page_spec(i) for i in range(pages)] * 2,
        out_specs=pl.BlockSpec((1, n_row, HEAD_DIM), lambda bi, st, pt: (bi, 0, 0)),
        scratch_shapes=[pltpu.VMEM((n_row, HEAD_DIM), f32), pltpu.VMEM((n_row, PAGE_SIZE), f32)],
    )
    out = pl.pallas_call(
        functools.partial(_attn_sample_kernel, n_new=n_new, pages=pages),
        grid_spec=grid_spec,
        out_shape=SDS((b, n_row, HEAD_DIM), f32),
        compiler_params=_cparams(("arbitrary", "arbitrary")),
        name="attn_sample",
    )(page_table, bias, qrow, k_new, v_new, uo, *([cache_k] * pages), *([cache_v] * pages))
    return jnp.transpose(out.reshape(b, H_A, n_new, HEAD_DIM), (0, 2, 1, 3)).reshape(b, n_new, W_A)


def _prep_kernel(pb_ref, pf_ref, mu_ref, dbase_ref, dup_ref, ibase_ref, iup_ref, gup_ref,
                 kk_ref, ka_ref, rk_ref, hsum_ref,
                 r_out, lw_out, k_out, v_out, a_out, b_out, g_out, bonus_out, *, t_real):
    nb, tt, w = pb_ref.shape
    pb = pb_ref[...]
    tpos = lax.broadcasted_iota(i32, pb.shape, 1)
    prev = jnp.where(tpos == 0, pf_ref[:, 0], pltpu.roll(pb, 1, axis=1))
    x = (pb + (prev - pb) * mu_ref[...]).reshape(nb * tt, w)
    r = x[:, 0:W_B]
    k = x[:, W_B:2 * W_B]
    v = x[:, 2 * W_B:3 * W_B]
    o = 3 * W_B
    xw = x[:, o:o + DECAY_LORA]
    xa = x[:, o + DECAY_LORA:o + DECAY_LORA + ICLR_LORA]
    xg = x[:, o + DECAY_LORA + ICLR_LORA:]
    log_w = -_softplus(-(dbase_ref[...] + _bdot(jnp.tanh(xw), dup_ref[...]))) - 0.5
    lw = -jnp.exp(log_w)
    a = _sigmoid(ibase_ref[...] + _bdot(xa, iup_ref[...]))
    g = _bdot(_sigmoid(xg), gup_ref[...])
    hsum = hsum_ref[...]
    kkf = k * kk_ref[...]
    kk = kkf / jnp.maximum(jnp.sqrt(_split_dot(kkf * kkf, hsum)), 1e-12)
    k2 = k * (1.0 + (a - 1.0) * ka_ref[...])
    bonus = _split_dot(r * k2 * rk_ref[...], hsum) * v
    na = -kk
    kb = kk * a
    if t_real < tt:
        valid = (lax.broadcasted_iota(i32, (nb, tt, W_B), 1) < t_real).reshape(nb * tt, W_B)
        zero = lambda u: jnp.where(valid, u, 0.0)
        r, lw, k2, v, na, kb = zero(r), zero(lw), zero(k2), zero(v), zero(na), zero(kb)
    g_out[...] = g.reshape(nb, tt, W_B)
    bonus_out[...] = bonus.reshape(nb, tt, W_B)
    for val, ref in ((r, r_out), (lw, lw_out), (k2, k_out), (v, v_out), (na, a_out), (kb, b_out)):
        val = val.reshape(nb, tt, W_B)
        for h in range(H_B):
            ref[:, h, :, :] = val[:, :, h * HEAD_DIM:(h + 1) * HEAD_DIM]


def _rwkv_prep(pb, prev_first, p, nb, tt, t_real):
    b, t, w = pb.shape
    row = lambda i, j: (i, j, 0)
    const = lambda i, j: (0, 0)
    vec = lambda a: a.reshape(1, -1)
    head_of = jnp.arange(W_B) // HEAD_DIM
    hsum = (head_of[:, None] == head_of[None, :]).astype(bf16)
    params = [vec(p['mu_shift']), vec(p['decay_base']), p['decay_up'], vec(p['iclr_base']), p['iclr_up'],
              p['gate_up'], vec(p['k_k']), vec(p['k_a']), vec(p['r_k']), hsum]
    heads = pl.BlockSpec((nb, H_B, tt, HEAD_DIM), lambda i, j: (i, 0, j, 0))
    return pl.pallas_call(
        functools.partial(_prep_kernel, t_real=t_real),
        grid=(b // nb, t // tt),
        in_specs=[pl.BlockSpec((nb, tt, w), row),
                  pl.BlockSpec((nb, 1, 1, w), lambda i, j: (i, j, 0, 0))]
                 + [pl.BlockSpec(a.shape, const) for a in params],
        out_specs=[heads] * 6 + [pl.BlockSpec((nb, tt, W_B), row)] * 2,
        out_shape=[SDS((b, H_B, t, HEAD_DIM), f32)] * 6 + [SDS((b, t, W_B), f32)] * 2,
        compiler_params=_cparams(("arbitrary", "arbitrary")),
        name="rwkv_prep",
    )(pb, prev_first, *params)


def _wkv_kernel(r_ref, lw_ref, k_ref, v_ref, a_ref, b_ref, s0_ref, y_ref, s_out, st_ref, *, chunk):
    c = chunk
    n = HEAD_DIM
    step = pl.program_id(1)

    n_chain = lw_ref.shape[0] * H_B
    of = lambda ref, i: ref[i // H_B, i % H_B]

    @pl.when(step == 0)
    def _():
        for i in range(n_chain):
            st_ref[i] = of(s0_ref, i).T

    ti = lax.broadcasted_iota(i32, (c, c), 0)
    si = lax.broadcasted_iota(i32, (c, c), 1)
    tri = (ti >= si).astype(bf16)
    eye = (ti == si).astype(f32)
    ones = jnp.ones((c, n), bf16)
    row2 = lax.broadcasted_iota(i32, (c, 2 * c), 0)
    col2 = lax.broadcasted_iota(i32, (c, 2 * c), 1)
    strict2 = (col2 % c) < row2
    incl2 = (col2 % c) <= row2
    right = col2 >= c
    levels = max(c.bit_length() - 2, 0)
    heads = range(n_chain)
    each = lambda f: [f(h) for h in heads]
    lw = each(lambda h: of(lw_ref, h))
    lw3 = each(lambda h: _split3(lw[h]))
    cum = each(lambda h: sum(_dg(tri, t, _NN) for t in lw3[h]))
    wsum = each(lambda h: sum(_dg(t, ones, _TN) for t in lw3[h]))
    w_in = each(lambda h: jnp.exp(cum[h]))
    w_out = each(lambda h: jnp.exp(-cum[h]))
    at = each(lambda h: of(a_ref, h) * jnp.exp(cum[h] - lw[h]))
    rt = each(lambda h: of(r_ref, h) * w_in[h])
    bk = each(lambda h: jnp.concatenate([of(b_ref, h) * w_out[h], of(k_ref, h) * w_out[h]], axis=0))
    g = each(lambda h: _dot3(jnp.concatenate([at[h], rt[h]], axis=0), bk[h], _NT))
    top = each(lambda h: jnp.where(strict2, g[h][:c], 0.0))
    bot = each(lambda h: jnp.where(incl2, g[h][c:], 0.0))
    pw = each(lambda h: top[h][:, :c])
    inv = each(lambda h: eye + pw[h])
    for _ in range(levels):
        pw = each(lambda h: _dot3(pw[h], pw[h]))
        inv = each(lambda h: inv[h] + _dot3(inv[h], pw[h]))
    vv = each(lambda h: jnp.concatenate([of(v_ref, h), of(v_ref, h)], axis=0))
    xv = each(lambda h: _dot3(jnp.where(right, top[h], 0.0), vv[h]))
    x = each(lambda h: _dot3(at[h], st_ref[h]) + xv[h])
    u = each(lambda h: _dot3(inv[h], x[h]))
    uv = each(lambda h: jnp.concatenate([u[h], of(v_ref, h)], axis=0))
    y = each(lambda h: _bdot(bot[h], uv[h]) + _bdot(rt[h], st_ref[h]))
    st_new = each(lambda h: (st_ref[h] + _dot3(bk[h], uv[h], _TN)) * jnp.exp(wsum[h]))
    for h in heads:
        st_ref[h] = st_new[h]
        mu = jnp.mean(y[h], axis=-1, keepdims=True)
        yc = y[h] - mu
        var = jnp.mean(yc * yc, axis=-1, keepdims=True)
        col0 = (h % H_B) * HEAD_DIM
        y_ref[h // H_B, :, col0:col0 + HEAD_DIM] = yc * lax.rsqrt(var + LNX_EPS)

    @pl.when(step == pl.num_programs(1) - 1)
    def _():
        for i in range(n_chain):
            s_out[i // H_B, i % H_B] = st_ref[i].T


def _wkv_scan(r, lw, k, v, a, b, s0, chunk, nseq):
    bsz, h, t, n = r.shape
    seq = pl.BlockSpec((nseq, h, chunk, n), lambda i, j: (i, 0, j, 0))
    state = pl.BlockSpec((nseq, h, n, n), lambda i, j: (i, 0, 0, 0))
    return pl.pallas_call(
        functools.partial(_wkv_kernel, chunk=chunk),
        grid=(bsz // nseq, t // chunk),
        in_specs=[seq] * 6 + [state],
        out_specs=[pl.BlockSpec((nseq, chunk, h * n), lambda i, j: (i, j, 0)), state],
        out_shape=[SDS((bsz, t, h * n), f32), SDS((bsz, h, n, n), f32)],
        scratch_shapes=[pltpu.VMEM((nseq * h, n, n), f32)],
        compiler_params=_cparams(("arbitrary", "arbitrary")),
        name="wkv_scan",
    )(r, lw, k, v, a, b, s0)


def _outproj_kernel(attn_ref, yn_ref, bonus_ref, g_ref, x_ref, gt1_ref, sh2_ref, sc2_ref,
                    gattn_ref, lnxg_ref, lnxb_ref, wout_ref, gpost_ref, gpre_ref, rw_ref, rb_ref,
                    x1_ref, h2_ref, logit_ref):
    nb, tt, d = x_ref.shape
    o_a = _rms(attn_ref[...], gattn_ref[...])
    o_b = (yn_ref[...] * lnxg_ref[...] + lnxb_ref[...] + bonus_ref[...]) * g_ref[...]
    cat = jnp.concatenate([o_a, o_b], axis=-1).reshape(nb * tt, d)
    catb = cat.astype(bf16)
    half = d // 2
    mixed = jnp.concatenate([jnp.dot(catb, wout_ref[:, :half], preferred_element_type=f32),
                             jnp.dot(catb, wout_ref[:, half:], preferred_element_type=f32)],
                            axis=1).reshape(nb, tt, d)
    x1 = x_ref[...] + gt1_ref[...] * _rms(mixed, gpost_ref[...])
    x1_ref[...] = x1
    h2 = _rms(x1, gpre_ref[...]) * (1.0 + sc2_ref[...]) + sh2_ref[...]
    _store_token_tiles(h2_ref, 0, h2.reshape(nb * tt, d))
    logits = _dot3(h2.reshape(nb * tt, d), rw_ref[...]) + rb_ref[...]
    logit_ref[...] = logits.reshape(nb, tt, N_EXPERTS)


def _out_proj(attn, yn, bonus, g, x, gt1, sh2, sc2, p, w_out_bf16, nb, tt):
    b, t, d = x.shape
    row = lambda w: pl.BlockSpec((nb, tt, w), lambda i, j: (i, j, 0))
    mod = pl.BlockSpec((nb, 1, d), lambda i, j: (i, 0, 0))
    vec = lambda a: a.reshape(1, -1)
    params = [vec(p['g_attn_out']), vec(p['lnx_g']), vec(p['lnx_b']), w_out_bf16,
              vec(p['g_mix_post']), vec(p['g_ffn_pre']), p['router_w'], vec(p['router_b'])]
    return pl.pallas_call(
        _outproj_kernel,
        grid=(b // nb, t // tt),
        in_specs=[row(W_A), row(W_B), row(W_B), row(W_B), row(d), mod, mod, mod]
                 + [pl.BlockSpec(a.shape, lambda i, j: (0, 0)) for a in params],
        out_specs=[row(d), pl.BlockSpec((nb, tt, V7X_SUBLANES, V7X_LANES), lambda i, j: (i, j, 0, 0)),
                   row(N_EXPERTS)],
        out_shape=[SDS((b, t, d), f32), SDS((b, t, V7X_SUBLANES, V7X_LANES), f32),
                   SDS((b, t, N_EXPERTS), f32)],
        compiler_params=_cparams(("arbitrary", "arbitrary")),
        name="out_proj",
    )(attn, yn, bonus, g, x, gt1, sh2, sc2, *params)


def _route_kernel(logit_ref, idx_ref, gate_ref, rank_ref, count_ref, base_ref):
    tile = logit_ref.shape[0]

    @pl.when(pl.program_id(0) == 0)
    def _():
        base_ref[...] = jnp.zeros_like(base_ref)

    lane = lax.broadcasted_iota(i32, (tile, N_EXPERTS), 1).astype(f32)
    cur = logit_ref[...]
    hots, vals, idxs = [], [], []
    for _ in range(TOP_K):
        top = jnp.max(cur, axis=-1, keepdims=True)
        idx = jnp.min(jnp.where(cur == top, lane, float(N_EXPERTS)), axis=-1, keepdims=True)
        hot = lane == idx
        hots.append(hot)
        vals.append(top)
        idxs.append(idx)
        cur = jnp.where(hot, -jnp.inf, cur)
    es = [jnp.exp(v - vals[0]) for v in vals]
    total = sum(es)
    chosen = sum(h.astype(f32) for h in hots)
    ti = lax.broadcasted_iota(i32, (tile, tile), 0)
    si = lax.broadcasted_iota(i32, (tile, tile), 1)
    earlier = jnp.dot((ti > si).astype(bf16), chosen.astype(bf16), preferred_element_type=f32) + base_ref[...]
    ranks = [jnp.sum(jnp.where(h, earlier, 0.0), axis=-1, keepdims=True) for h in hots]
    base_ref[...] += jnp.sum(chosen, axis=0, keepdims=True)
    count_ref[...] = base_ref[...]
    col = lax.broadcasted_iota(i32, (tile, TOP_K), 1)
    pick = lambda parts: sum(jnp.where(col == k, parts[k], 0.0) for k in range(TOP_K))
    idx_ref[...] = pick(idxs).astype(i32)
    gate_ref[...] = pick([e / total for e in es])
    rank_ref[...] = pick(ranks).astype(i32)


def _route(logits, tile):
    m = logits.shape[0]
    tok = lambda dt: SDS((m, TOP_K), dt)
    blk = pl.BlockSpec((tile, TOP_K), lambda i: (i, 0))
    return pl.pallas_call(
        _route_kernel,
        grid=(m // tile,),
        in_specs=[pl.BlockSpec((tile, N_EXPERTS), lambda i: (i, 0))],
        out_specs=[blk, blk, blk, pl.BlockSpec((1, N_EXPERTS), lambda i: (0, 0))],
        out_shape=[tok(i32), tok(f32), tok(i32), SDS((1, N_EXPERTS), f32)],
        scratch_shapes=[pltpu.VMEM((1, N_EXPERTS), f32)],
        compiler_params=_cparams(("arbitrary",)),
        name="moe_route",
    )(logits)


def _row_copy(src, src_row, dst, dst_row, sem):
    return pltpu.make_async_copy(src.at[pl.ds(src_row, 1)], dst.at[pl.ds(dst_row, 1)], sem)


def _dispatch_kernel(dest_ref, xa_ref, xb_ref, xs_in, xs_out, sem, *, n_first):
    del xs_in
    tile = xa_ref.shape[0]

    def scatter_rows(x_ref):
        def start(r, _):
            for k in range(TOP_K):
                _row_copy(x_ref, r, xs_out, dest_ref[k, r], sem).start()
            return 0

        lax.fori_loop(0, tile, start, 0, unroll=DMA_ISSUE_UNROLL)
        for k in range(TOP_K):
            pltpu.make_async_copy(x_ref, xs_out.at[pl.ds(0, tile)], sem).wait()

    @pl.when(pl.program_id(0) < n_first)
    def _():
        scatter_rows(xa_ref)

    @pl.when(pl.program_id(0) >= n_first)
    def _():
        scatter_rows(xb_ref)


def _dispatch(xa, xb, dest_t, n_slots, tile):
    row_tile = xa.shape[1:]
    n_first = xa.shape[0] // tile
    return pl.pallas_call(
        functools.partial(_dispatch_kernel, n_first=n_first),
        grid=(n_first + xb.shape[0] // tile,),
        in_specs=[pl.BlockSpec((TOP_K, tile), lambda i: (0, i), memory_space=pltpu.SMEM),
                  pl.BlockSpec((tile,) + row_tile, lambda i: (jnp.minimum(i, n_first - 1), 0, 0)),
                  pl.BlockSpec((tile,) + row_tile, lambda i: (jnp.maximum(i - n_first, 0), 0, 0)),
                  pl.BlockSpec(memory_space=pl.ANY)],
        out_specs=pl.BlockSpec(memory_space=pl.ANY),
        out_shape=SDS((n_slots,) + row_tile, xa.dtype),
        scratch_shapes=[pltpu.SemaphoreType.DMA(())],
        input_output_aliases={3: 0},
        compiler_params=_cparams(("arbitrary",)),
        name="moe_dispatch",
    )(dest_t, xa, xb, jnp.zeros((n_slots,) + row_tile, xa.dtype))


def _combine_kernel(dest_ref, gate_ref, ys_hbm, fa_ref, fb_ref, buf_ref, sem, *, n_first):
    tile = fa_ref.shape[0]

    def start(r, _):
        for k in range(TOP_K):
            _row_copy(ys_hbm, dest_ref[k, r], buf_ref.at[k], r, sem).start()
        return 0

    lax.fori_loop(0, tile, start, 0, unroll=DMA_ISSUE_UNROLL)
    for k in range(TOP_K):
        pltpu.make_async_copy(ys_hbm.at[pl.ds(0, tile)], buf_ref.at[k], sem).wait()
    gate = gate_ref[...]
    f = sum(gate[:, k:k + 1] * _load_token_tiles(buf_ref, k * tile, tile) for k in range(TOP_K))

    @pl.when(pl.program_id(0) < n_first)
    def _():
        fa_ref[...] = f

    @pl.when(pl.program_id(0) >= n_first)
    def _():
        fb_ref[...] = f


def _combine(ys, dest_t, gate, m_first, tile):
    m = gate.shape[0]
    d = math.prod(ys.shape[1:])
    n_first = m_first // tile
    return pl.pallas_call(
        functools.partial(_combine_kernel, n_first=n_first),
        grid=(m // tile,),
        in_specs=[pl.BlockSpec((TOP_K, tile), lambda i: (0, i), memory_space=pltpu.SMEM),
                  pl.BlockSpec((tile, TOP_K), lambda i: (i, 0)),
                  pl.BlockSpec(memory_space=pl.ANY)],
        out_specs=[pl.BlockSpec((tile, d), lambda i: (jnp.minimum(i, n_first - 1), 0)),
                   pl.BlockSpec((tile, d), lambda i: (jnp.maximum(i - n_first, 0), 0))],
        out_shape=[SDS((m_first, d), f32), SDS((m - m_first, d), f32)],
        scratch_shapes=[pltpu.VMEM((TOP_K, tile) + ys.shape[1:], f32), pltpu.SemaphoreType.DMA(())],
        compiler_params=_cparams(("arbitrary",)),
        name="moe_combine",
    )(dest_t, gate, ys)


def _expert_kernel(be_ref, nused_ref, x_ref, w1_ref, b1_ref, w2_ref, b2_ref, y_ref, w1b_ref, w2b_ref):
    i = pl.program_id(0)
    prev = be_ref[jnp.maximum(i - 1, 0)]

    @pl.when((i == 0) | (be_ref[i] != prev))
    def _():
        w1b_ref[...] = w1_ref[0].astype(bf16)
        w2b_ref[...] = w2_ref[0].astype(bf16)

    @pl.when(i < nused_ref[0])
    def _():
        x = _load_token_tiles(x_ref, 0, x_ref.shape[0]).astype(bf16)
        u = jnp.dot(x, w1b_ref[...], preferred_element_type=f32) + b1_ref[0]
        u_glu = jnp.minimum(u[:, :D_FF], SWIGLU_LIMIT)
        u_lin = jnp.clip(u[:, D_FF:], -SWIGLU_LIMIT, SWIGLU_LIMIT)
        act = u_glu * _sigmoid(SWIGLU_ALPHA * u_glu) * (u_lin + 1.0)
        y = jnp.dot(act.astype(bf16), w2b_ref[...], preferred_element_type=f32) + b2_ref[0]
        _store_token_tiles(y_ref, 0, y)

    @pl.when(i >= nused_ref[0])
    def _():
        y_ref[...] = jnp.zeros_like(y_ref)


def _expert_ffn(xs, blk_expert, n_used, e_w1, e_b1, e_w2, e_b2, tm):
    n_slots = xs.shape[0]
    row_tile = xs.shape[1:]
    d = math.prod(row_tile)
    n_blocks = n_slots // tm
    rows = pl.BlockSpec((tm,) + row_tile, lambda i, be, nu: (i, 0, 0))
    ex = lambda i, be, nu: (be[i], 0, 0)
    grid_spec = pltpu.PrefetchScalarGridSpec(
        num_scalar_prefetch=2,
        grid=(n_blocks,),
        in_specs=[rows,
                  pl.BlockSpec((1, d, 2 * D_FF), ex),
                  pl.BlockSpec((1, 1, 2 * D_FF), ex),
                  pl.BlockSpec((1, D_FF, d), ex),
                  pl.BlockSpec((1, 1, d), ex)],
        out_specs=rows,
        scratch_shapes=[pltpu.VMEM((d, 2 * D_FF), bf16), pltpu.VMEM((D_FF, d), bf16)],
    )
    return pl.pallas_call(
        _expert_kernel,
        grid_spec=grid_spec,
        out_shape=SDS((n_slots,) + row_tile, f32),
        compiler_params=_cparams(("arbitrary",)),
        name="expert_ffn",
    )(blk_expert, n_used, xs, e_w1, e_b1.reshape(N_EXPERTS, 1, -1), e_w2, e_b2.reshape(N_EXPERTS, 1, -1))


def _moe(h2_a, logits_a, h2_b, logits_b, e_w1, e_b1, e_w2, e_b2, tm):
    tile = MOE_TOKEN_TILE
    m_a, m_b = h2_a.shape[0], h2_b.shape[0]
    assert m_a % tile == 0
    extra = -m_b % tile
    h2_b = jnp.pad(h2_b, ((0, extra), (0, 0), (0, 0)))
    logits = jnp.concatenate([logits_a, logits_b, jnp.zeros((extra, N_EXPERTS), f32)], axis=0)
    m = m_a + m_b + extra
    top_idx, gate, rank, counts = _route(logits, tile)
    counts = counts.reshape(N_EXPERTS).astype(i32)
    padded = (counts + tm - 1) // tm * tm
    pad_end = jnp.cumsum(padded)
    pad_start = pad_end - padded
    n_blocks = (m * TOP_K + N_EXPERTS * (tm - 1) + tm - 1) // tm
    blk_row = (jnp.arange(n_blocks) * tm)[:, None]
    blk_expert = jnp.minimum(jnp.sum(pad_end[None, :] <= blk_row, axis=1), N_EXPERTS - 1).astype(i32)
    n_used = (pad_end[-1] // tm).astype(i32).reshape(1)
    group_start = jnp.sum(jnp.where(top_idx[:, :, None] == jnp.arange(N_EXPERTS), pad_start, 0), axis=-1)
    dest_t = (group_start + rank).astype(i32).T
    xs = _dispatch(h2_a, h2_b, dest_t, n_blocks * tm, tile)
    ys = _expert_ffn(xs, blk_expert, n_used, e_w1, e_b1, e_w2, e_b2, tm)
    f_a, f_b = _combine(ys, dest_t, gate, m_a, tile)
    return f_a, f_b[:m_b]


def _final_kernel(x1_ref, f_ref, gt2_ref, g_ref, o_ref):
    o_ref[...] = x1_ref[...] + gt2_ref[...] * _rms(f_ref[...], g_ref[...])


def _final(x1, f, gt2, g, nb, tt):
    b, t, d = x1.shape
    row = pl.BlockSpec((nb, tt, d), lambda i, j: (i, j, 0))
    return pl.pallas_call(
        _final_kernel,
        grid=(b // nb, t // tt),
        in_specs=[row, row, pl.BlockSpec((nb, 1, d), lambda i, j: (i, 0, 0)),
                  pl.BlockSpec((1, d), lambda i, j: (0, 0))],
        out_specs=row,
        out_shape=SDS((b, t, d), f32),
        compiler_params=_cparams(("arbitrary", "arbitrary")),
        name="final_residual",
    )(x1, f, gt2, g.reshape(1, d))


def _mix_layer(x, mod, attend, shift_prev, wkv_prev, p, w_in_bf16, w_out_bf16, nb, tt, t_real, chunk, wkv_nseq):
    b, t, d = x.shape
    sh1, sc1, gt1, sh2, sc2, gt2 = [mod[:, i:i + 1, :] for i in range(6)]
    q, k, v, pb, last = _in_proj(x, sh1, sc1, p['g_mix_pre'], w_in_bf16, nb, tt, t_real)
    attn = attend(q, k, v)
    prev_first = jnp.concatenate([shift_prev[:, None, None, :], last[:, :-1]], axis=1)
    r, lw, k2, v2, a, kb, g, bonus = _rwkv_prep(pb, prev_first, p, nb, tt, t_real)
    yn, wkv_new = _wkv_scan(r, lw, k2, v2, a, kb, wkv_prev, chunk, wkv_nseq)
    x1, h2, logits = _out_proj(attn, yn, bonus, g, x, gt1, sh2, sc2, p, w_out_bf16, nb, tt)
    return x1, h2, logits, gt2, k, v, wkv_new, last[:, -1, 0]


def kernel(x_prompt, x_sample, cache_k, cache_v, state_wkv, state_shift, page_table, c_prompt, c_sample,
           w_ada, b_ada, g_mix_pre, g_mix_post, g_ffn_pre, g_ffn_post, w_in, mu_shift,
           decay_base, decay_up, iclr_base, iclr_up, gate_up, k_k, k_a, r_k, lnx_g, lnx_b,
           g_attn_out, sb_bias, w_out, router_w, router_b, e_w1, e_b1, e_w2, e_b2):
    weights = dict(w_ada=w_ada, b_ada=b_ada, g_mix_pre=g_mix_pre, g_mix_post=g_mix_post,
                   g_ffn_pre=g_ffn_pre, g_ffn_post=g_ffn_post, w_in=w_in, mu_shift=mu_shift,
                   decay_base=decay_base, decay_up=decay_up, iclr_base=iclr_base, iclr_up=iclr_up,
                   gate_up=gate_up, k_k=k_k, k_a=k_a, r_k=r_k, lnx_g=lnx_g, lnx_b=lnx_b,
                   g_attn_out=g_attn_out, sb_bias=sb_bias, w_out=w_out, router_w=router_w,
                   router_b=router_b, e_w1=e_w1, e_b1=e_b1, e_w2=e_w2, e_b2=e_b2)
    depth = w_ada.shape[0]
    bp, tp, d = x_prompt.shape
    bs, ts, _ = x_sample.shape
    ts_pad = -(-ts // V7X_SUBLANES) * V7X_SUBLANES
    hp = x_prompt
    hs = jnp.pad(x_sample, ((0, 0), (0, ts_pad - ts), (0, 0)))
    outs = [[] for _ in range(8)]
    for l in range(depth):
        p = {name: w[l] for name, w in weights.items()}
        w_in_bf16 = p['w_in'].astype(bf16)
        w_out_bf16 = p['w_out'].astype(bf16)
        mod = _ada_mod(jnp.concatenate([c_prompt, c_sample], axis=0), p['w_ada'], p['b_ada'])
        mod = mod.reshape(bp + bs, 6, d)

        attend_p = lambda q, k, v: _attn_prompt(q, k, v, p['sb_bias'])
        x1p, h2p, lgp, gt2p, kp, vp, wp, shp = _mix_layer(
            hp, mod[:bp], attend_p, jnp.zeros((bp, SHIFT_W), f32),
            jnp.zeros((bp, H_B, HEAD_DIM, HEAD_DIM), f32), p, w_in_bf16, w_out_bf16,
            nb=1, tt=ROW_TILE, t_real=tp, chunk=WKV_CHUNK, wkv_nseq=WKV_NSEQ)

        attend_s = lambda q, k, v: jnp.pad(
            _attn_sample(q, k, v, cache_k, cache_v, l, page_table, p['sb_bias'], ts),
            ((0, 0), (0, ts_pad - ts), (0, 0)))
        x1s, h2s, lgs, gt2s, ks, vs, ws, shs = _mix_layer(
            hs, mod[bp:], attend_s, state_shift[l], state_wkv[l], p, w_in_bf16, w_out_bf16,
            nb=bs, tt=ts_pad, t_real=ts, chunk=ts_pad, wkv_nseq=WKV_NSEQ)

        mp = bp * tp
        fp, fs = _moe(h2p.reshape((mp,) + h2p.shape[2:]), lgp.reshape(mp, N_EXPERTS),
                      h2s[:, :ts].reshape((bs * ts,) + h2s.shape[2:]), lgs[:, :ts].reshape(bs * ts, N_EXPERTS),
                      p['e_w1'], p['e_b1'], p['e_w2'], p['e_b2'], MOE_TILE)
        fp = fp.reshape(bp, tp, d)
        fs = jnp.pad(fs.reshape(bs, ts, d), ((0, 0), (0, ts_pad - ts), (0, 0)))
        hp = _final(x1p, fp, gt2p, p['g_ffn_post'], 1, ROW_TILE)
        hs = _final(x1s, fs, gt2s, p['g_ffn_post'], bs, ts_pad)

        for lst, val in zip(outs, (kp.reshape(bp, tp, H_A, HEAD_DIM), vp.reshape(bp, tp, H_A, HEAD_DIM),
                                   ks[:, :ts].reshape(bs, ts, H_A, HEAD_DIM),
                                   vs[:, :ts].reshape(bs, ts, H_A, HEAD_DIM), wp, ws, shp, shs)):
            lst.append(val)
    return (hp, hs[:, :ts]) + tuple(jnp.stack(lst) for lst in outs)
```

```python
import functools
import math

import jax
import jax.numpy as jnp
from jax import lax
from jax.experimental import pallas as pl
from jax.experimental.pallas import tpu as pltpu

f32 = jnp.float32
bf16 = jnp.bfloat16
i32 = jnp.int32
SDS = jax.ShapeDtypeStruct

D_MODEL = 1024
HEAD_DIM = 64
W_A = 512
W_B = 512
H_A = W_A // HEAD_DIM
H_B = W_B // HEAD_DIM
DECAY_LORA = 64
ICLR_LORA = 64
GATE_LORA = 128
SHIFT_W = 3 * W_B + DECAY_LORA + ICLR_LORA + GATE_LORA
N_EXPERTS = 32
TOP_K = 4
D_FF = D_MODEL
SWIGLU_ALPHA = 1.702
SWIGLU_LIMIT = 7.0
PAGE_SIZE = 128
RMS_EPS = 1e-6
LNX_EPS = 64e-5
LOG2E = 1.4426950408889634

V7X_SUBLANES = 8
V7X_LANES = 128
VMEM_LIMIT = 56 * 1024 * 1024

ROW_TILE = 256
WKV_CHUNK = 64
WKV_NSEQ = 2
ATTN_TQ = 512
ATTN_TK = 128
ATTN_NSUB = 4
PAGES_PER_STEP = 16
MOE_TILE = 256
MOE_TOKEN_TILE = 128
DMA_ISSUE_UNROLL = 8


def _cparams(sem):
    return pltpu.CompilerParams(dimension_semantics=sem, vmem_limit_bytes=VMEM_LIMIT)


def _bdot(a, b):
    return jnp.dot(a.astype(bf16), b.astype(bf16), preferred_element_type=f32)


def _fdot(a, b, dims=(((1,), (0,)), ((), ()))):
    return lax.dot_general(a, b, dims, precision=lax.Precision.HIGHEST, preferred_element_type=f32)


_NT = (((1,), (1,)), ((), ()))
_TN = (((0,), (0,)), ((), ()))


def _split_dot(x, m_bf16):
    hi = x.astype(bf16)
    lo = (x - hi.astype(f32)).astype(bf16)
    return (jnp.dot(hi, m_bf16, preferred_element_type=f32)
            + jnp.dot(lo, m_bf16, preferred_element_type=f32))


def _sigmoid(x):
    return 1.0 / (1.0 + jnp.exp(-x))


def _softplus(x):
    return jnp.maximum(x, 0.0) + jnp.log(1.0 + jnp.exp(-jnp.abs(x)))


def _softplus2(z):
    return jnp.maximum(z, 0.0) + jnp.log2(1.0 + jnp.exp2(-jnp.abs(z)))


def _split2(x):
    hi = x.astype(bf16)
    return hi, (x - hi.astype(f32)).astype(bf16)


def _split3(x):
    hi = x.astype(bf16)
    r = x - hi.astype(f32)
    mid = r.astype(bf16)
    return hi, mid, (r - mid.astype(f32)).astype(bf16)


def _dg(a, b, dims):
    return lax.dot_general(a, b, dims, preferred_element_type=f32)


_NN = (((1,), (0,)), ((), ()))


def _dot3(a, b, dims=_NN):
    ah, al = _split2(a)
    bh, bl = _split2(b)
    return _dg(ah, bh, dims) + _dg(ah, bl, dims) + _dg(al, bh, dims)


def _dot_exact_rhs(x3, m_bf16, dims=_NN):
    return _dg(x3[0], m_bf16, dims) + _dg(x3[1], m_bf16, dims) + _dg(x3[2], m_bf16, dims)


def _rms(x, g):
    return x * lax.rsqrt(jnp.mean(x * x, axis=-1, keepdims=True) + RMS_EPS) * g


def _store_token_tiles(ref, first_row, x2d):
    rows = x2d.shape[0]
    flat = ref.reshape(math.prod(ref.shape) // V7X_LANES, V7X_LANES)
    for c in range(V7X_SUBLANES):
        flat[pl.ds(first_row * V7X_SUBLANES + c, rows, stride=V7X_SUBLANES), :] = (
            x2d[:, c * V7X_LANES:(c + 1) * V7X_LANES])


def _load_token_tiles(ref, first_row, rows):
    flat = ref.reshape(math.prod(ref.shape) // V7X_LANES, V7X_LANES)
    return jnp.concatenate([flat[pl.ds(first_row * V7X_SUBLANES + c, rows, stride=V7X_SUBLANES), :]
                            for c in range(V7X_SUBLANES)], axis=1)


def _ada_kernel(c_ref, w_ref, b_ref, o_ref):
    c = c_ref[...]
    o_ref[...] = _bdot(c * _sigmoid(c), w_ref[...]) + b_ref[...]


def _ada_mod(c, w_ada, b_ada):
    n, d = c.shape
    nout = w_ada.shape[1]
    tn = 1536
    return pl.pallas_call(
        _ada_kernel,
        grid=(nout // tn,),
        in_specs=[pl.BlockSpec((n, d), lambda j: (0, 0)),
                  pl.BlockSpec((d, tn), lambda j: (0, j)),
                  pl.BlockSpec((1, tn), lambda j: (0, j))],
        out_specs=pl.BlockSpec((n, tn), lambda j: (0, j)),
        out_shape=SDS((n, nout), f32),
        compiler_params=_cparams(("arbitrary",)),
        name="ada_mod",
    )(c, w_ada, b_ada.reshape(1, nout))


def _inproj_kernel(x_ref, sh_ref, sc_ref, g_ref, w_ref, q_ref, k_ref, v_ref, pb_ref, last_ref, *, t_last):
    nb, tt, d = x_ref.shape
    h = _rms(x_ref[...], g_ref[...]) * (1.0 + sc_ref[...]) + sh_ref[...]
    hb = h.reshape(nb * tt, d).astype(bf16)
    q_ref[...] = jnp.dot(hb, w_ref[:, 0:W_A], preferred_element_type=f32).reshape(nb, tt, W_A)
    k_ref[...] = jnp.dot(hb, w_ref[:, W_A:2 * W_A], preferred_element_type=f32).reshape(nb, tt, W_A)
    v_ref[...] = jnp.dot(hb, w_ref[:, 2 * W_A:3 * W_A], preferred_element_type=f32).reshape(nb, tt, W_A)
    pb = jnp.dot(hb, w_ref[:, 3 * W_A:], preferred_element_type=f32).reshape(nb, tt, SHIFT_W)
    pb_ref[...] = pb
    last_ref[:, 0] = pb[:, t_last:t_last + 1, :]


def _in_proj(x, sh1, sc1, g, w_in_bf16, nb, tt, t_real):
    b, t, d = x.shape
    row = lambda w: pl.BlockSpec((nb, tt, w), lambda i, j: (i, j, 0))
    mod = pl.BlockSpec((nb, 1, d), lambda i, j: (i, 0, 0))
    return pl.pallas_call(
        functools.partial(_inproj_kernel, t_last=min(tt, t_real) - 1),
        grid=(b // nb, t // tt),
        in_specs=[row(d), mod, mod,
                  pl.BlockSpec((1, d), lambda i, j: (0, 0)),
                  pl.BlockSpec(w_in_bf16.shape, lambda i, j: (0, 0))],
        out_specs=[row(W_A), row(W_A), row(W_A), row(SHIFT_W),
                   pl.BlockSpec((nb, 1, 1, SHIFT_W), lambda i, j: (i, j, 0, 0))],
        out_shape=[SDS((b, t, W_A), f32)] * 3 + [SDS((b, t, SHIFT_W), f32),
                                                  SDS((b, t // tt, 1, SHIFT_W), f32)],
        compiler_params=_cparams(("arbitrary", "arbitrary")),
        name="in_proj",
    )(x, sh1, sc1, g.reshape(1, d), w_in_bf16)


def _attn_prompt_kernel(bias_ref, q_ref, k_ref, v_ref, u2_ref, o_ref, kbd_ref, vbd_ref, acc_ref,
                        *, tq, tk, nsub):
    hp = pl.program_id(1)
    qi = pl.program_id(2)
    two = 2 * tk
    big = tk * nsub
    ratio = tq // big
    n_tiles = k_ref.shape[1] // tk

    @pl.when(qi == 0)
    def _():
        first = lax.broadcasted_iota(i32, (tk, 2 * HEAD_DIM), 1) < HEAD_DIM

        def build(j, _):
            start = pl.multiple_of(j * tk, tk)
            kt = k_ref[0, pl.ds(start, tk), :]
            vt = v_ref[0, pl.ds(start, tk), :]
            kbd_ref[j, 0:tk, :] = jnp.where(first, kt, 0.0).astype(bf16)
            kbd_ref[j, tk:two, :] = jnp.where(first, 0.0, kt).astype(bf16)
            vbd_ref[j, 0:tk, :] = jnp.where(first, vt, 0.0).astype(bf16)
            vbd_ref[j, tk:two, :] = jnp.where(first, 0.0, vt).astype(bf16)
            return 0

        lax.fori_loop(0, n_tiles, build, 0)

    qb = (q_ref[0] * (LOG2E * HEAD_DIM ** -0.5)).astype(bf16)
    lane2 = lax.broadcasted_iota(i32, (1, two), 1)
    bias2 = jnp.where(lane2 < tk, bias_ref[2 * hp], bias_ref[2 * hp + 1]) * LOG2E
    bias_row = jnp.concatenate([bias2] * nsub, axis=1)
    u2 = u2_ref[...]
    acc_ref[...] = jnp.zeros_like(acc_ref)
    row = lax.broadcasted_iota(i32, (tq, nsub * two), 0)
    col = lax.broadcasted_iota(i32, (tq, nsub * two), 1)
    key_off = (col // two) * tk + col % tk

    def chunk(tile0, mask, carry):
        kb = kbd_ref[pl.ds(tile0, nsub)].reshape(nsub * two, 2 * HEAD_DIM)
        vb = vbd_ref[pl.ds(tile0, nsub)].reshape(nsub * two, 2 * HEAD_DIM)
        z = _dg(qb, kb, _NT) + bias_row
        sp = _softplus2(z)
        if mask is not None:
            sp = jnp.where(mask, sp, 0.0)
        spb = sp.astype(bf16)
        ws = [None] * nsub
        for j in range(nsub - 1, -1, -1):
            incl = jnp.dot(spb[:, j * two:(j + 1) * two], u2, preferred_element_type=f32)
            ws[j] = jnp.exp2(z[:, j * two:(j + 1) * two] - incl - carry)
            carry = carry + jnp.concatenate([jnp.broadcast_to(incl[:, 0:1], (tq, tk)),
                                             jnp.broadcast_to(incl[:, tk:tk + 1], (tq, tk))], axis=1)
        w = jnp.concatenate(ws, axis=1)
        if mask is not None:
            w = jnp.where(mask, w, 0.0)
        acc_ref[...] += jnp.dot(w.astype(bf16), vb, preferred_element_type=f32)
        return carry

    carry = jnp.zeros((tq, two), f32)
    for dgl in range(ratio - 1, -1, -1):
        carry = chunk(qi * (tq // tk) + dgl * nsub, (key_off + dgl * big) < row, carry)

    def body(n, carry):
        return chunk((qi * ratio - 1 - n) * nsub, None, carry)

    lax.fori_loop(0, qi * ratio, body, carry)
    o_ref[0] = acc_ref[...]


def _attn_prompt(q, k, v, sb_bias):
    b, t, _ = q.shape
    tq, tk, nsub = ATTN_TQ, ATTN_TK, ATTN_NSUB
    j = lax.broadcasted_iota(i32, (2 * tk, 2 * tk), 0)
    s = lax.broadcasted_iota(i32, (2 * tk, 2 * tk), 1)
    u2 = ((j >= s) & ((j // tk) == (s // tk))).astype(bf16)
    seq = pl.BlockSpec((1, t, 2 * HEAD_DIM), lambda bi, hp, qi: (bi, 0, hp))
    return pl.pallas_call(
        functools.partial(_attn_prompt_kernel, tq=tq, tk=tk, nsub=nsub),
        grid=(b, H_A // 2, t // tq),
        in_specs=[pl.BlockSpec(memory_space=pltpu.SMEM),
                  pl.BlockSpec((1, tq, 2 * HEAD_DIM), lambda bi, hp, qi: (bi, qi, hp)),
                  seq, seq,
                  pl.BlockSpec((2 * tk, 2 * tk), lambda bi, hp, qi: (0, 0))],
        out_specs=pl.BlockSpec((1, tq, 2 * HEAD_DIM), lambda bi, hp, qi: (bi, qi, hp)),
        out_shape=SDS((b, t, W_A), f32),
        scratch_shapes=[pltpu.VMEM((t // tk, 2 * tk, 2 * HEAD_DIM), bf16),
                        pltpu.VMEM((t // tk, 2 * tk, 2 * HEAD_DIM), bf16),
                        pltpu.VMEM((tq, 2 * HEAD_DIM), f32)],
        compiler_params=_cparams(("arbitrary", "arbitrary", "arbitrary")),
        name="attn_prompt",
    )(sb_bias, q, k, v, u2)


def _attn_sample_kernel(pt_ref, bias_ref, qrow_ref, knew_ref, vnew_ref, uo_ref, *rest, n_new, pages):
    kp_refs = rest[:pages]
    vp_refs = rest[pages:2 * pages]
    o_ref, acc_ref, carry_ref = rest[2 * pages:]
    step = pl.program_id(1)
    n_row = H_A * n_new
    row_head = lax.broadcasted_iota(i32, (n_row, PAGE_SIZE), 0) // n_new
    bias = jnp.broadcast_to(bias_ref[...], (n_row, PAGE_SIZE))

    def add_values(w, value_of_head, dims):
        for h in range(H_A):
            acc_ref[...] += _dg(jnp.where(row_head[:, :w.shape[1]] == h, w, 0.0).astype(bf16),
                                value_of_head(h), dims)

    @pl.when(step == 0)
    def _():
        pad = knew_ref.shape[1]
        knew = knew_ref[0].astype(bf16)
        vnew = vnew_ref[0].astype(bf16)
        head_cols = lambda x, h: x[:, h * HEAD_DIM:(h + 1) * HEAD_DIM]
        z = sum(_dg(qrow_ref[0, h], head_cols(knew, h), _NT) for h in range(H_A)) + bias[:, :pad]
        s_idx = lax.broadcasted_iota(i32, (n_row, pad), 1)
        t_idx = lax.broadcasted_iota(i32, (n_row, pad), 0) % n_new
        mask = s_idx < t_idx
        sp = jnp.where(mask, _softplus2(z), 0.0)
        incl = jnp.zeros_like(sp)
        for j in range(n_new):
            incl = incl + jnp.where(s_idx <= j, sp[:, j:j + 1], 0.0)
        w = jnp.where(mask, jnp.exp2(z - incl), 0.0)
        acc_ref[...] = jnp.zeros_like(acc_ref)
        add_values(w, lambda h: head_cols(vnew, h), _NN)
        carry_ref[...] = jnp.broadcast_to(incl[:, 0:1], carry_ref.shape)

    uo = uo_ref[...]
    head_t = lambda ref, h: ref[0, 0, h].astype(bf16)
    z = [sum(_dg(qrow_ref[0, h], head_t(kp_refs[i], h), _NN) for h in range(H_A)) + bias
         for i in range(pages)]
    sp = [_softplus2(z[i]) for i in range(pages)]
    cr = [jnp.dot(sp[i].astype(bf16), uo, preferred_element_type=f32) for i in range(pages)]
    carry = carry_ref[...]
    for i in range(pages):
        w = jnp.exp2(z[i] - cr[i][:, :PAGE_SIZE] - carry)
        add_values(w, lambda h, i=i: head_t(vp_refs[i], h), _NT)
        carry = carry + cr[i][:, PAGE_SIZE:]
    carry_ref[...] = carry

    @pl.when(step == pl.num_programs(1) - 1)
    def _():
        o_ref[0] = acc_ref[...]


def _attn_sample(q, k_new, v_new, cache_k, cache_v, layer, page_table, sb_bias, n_new):
    b, pad, _ = q.shape
    n_pages = page_table.shape[1]
    pages = PAGES_PER_STEP
    n_row = H_A * n_new
    scale = LOG2E * HEAD_DIM ** -0.5
    qh = jnp.transpose(q[:, :n_new].reshape(b, n_new, H_A, HEAD_DIM) * scale, (0, 2, 1, 3))
    eye = jnp.eye(H_A, dtype=f32)
    qrow = (qh[:, :, None, :, :] * eye[None, :, :, None, None]).reshape(b, H_A, n_row, HEAD_DIM).astype(bf16)
    bias = jnp.repeat(sb_bias * LOG2E, n_new).reshape(n_row, 1)
    j = lax.broadcasted_iota(i32, (PAGE_SIZE, PAGE_SIZE), 0)
    s = lax.broadcasted_iota(i32, (PAGE_SIZE, PAGE_SIZE), 1)
    uo = jnp.concatenate([(j >= s).astype(bf16), jnp.ones((PAGE_SIZE, PAGE_SIZE), bf16)], axis=1)
    cache_k = jnp.transpose(cache_k, (0, 1, 3, 4, 2))
    cache_v = jnp.transpose(cache_v, (0, 1, 3, 4, 2))

    def page_spec(i):
        return pl.BlockSpec(
            (1, 1, H_A, HEAD_DIM, PAGE_SIZE),
            lambda bi, st, pt, i=i: (layer, pt[bi, n_pages - 1 - (st * pages + i)], 0, 0, 0))

    new_spec = pl.BlockSpec((1, pad, W_A), lambda bi, st, pt: (bi, 0, 0))
    grid_spec = pltpu.PrefetchScalarGridSpec(
        num_scalar_prefetch=1,
        grid=(b, n_pages // pages),
        in_specs=[pl.BlockSpec((n_row, 1), lambda bi, st, pt: (0, 0)),
                  pl.BlockSpec((1, H_A, n_row, HEAD_DIM), lambda bi, st, pt: (bi, 0, 0, 0)),
                  new_spec, new_spec,
                  pl.BlockSpec((PAGE_SIZE, 2 * PAGE_SIZE), lambda bi, st, pt: (0, 0))]
                 + [page_spec(i) for i in range(pages)] * 2,
        out_specs=pl.BlockSpec((1, n_row, HEAD_DIM), lambda bi, st, pt: (bi, 0, 0)),
        scratch_shapes=[pltpu.VMEM((n_row, HEAD_DIM), f32), pltpu.VMEM((n_row, PAGE_SIZE), f32)],
    )
    out = pl.pallas_call(
        functools.partial(_attn_sample_kernel, n_new=n_new, pages=pages),
        grid_spec=grid_spec,
        out_shape=SDS((b, n_row, HEAD_DIM), f32),
        compiler_params=_cparams(("arbitrary", "arbitrary")),
        name="attn_sample",
    )(page_table, bias, qrow, k_new, v_new, uo, *([cache_k] * pages), *([cache_v] * pages))
    return jnp.transpose(out.reshape(b, H_A, n_new, HEAD_DIM), (0, 2, 1, 3)).reshape(b, n_new, W_A)


def _prep_kernel(pb_ref, pf_ref, mu_ref, dbase_ref, dup_ref, ibase_ref, iup_ref, gup_ref,
                 kk_ref, ka_ref, rk_ref, hsum_ref,
                 r_out, lw_out, k_out, v_out, a_out, b_out, g_out, bonus_out, *, t_real):
    nb, tt, w = pb_ref.shape
    pb = pb_ref[...]
    tpos = lax.broadcasted_iota(i32, pb.shape, 1)
    prev = jnp.where(tpos == 0, pf_ref[:, 0], pltpu.roll(pb, 1, axis=1))
    x = (pb + (prev - pb) * mu_ref[...]).reshape(nb * tt, w)
    r = x[:, 0:W_B]
    k = x[:, W_B:2 * W_B]
    v = x[:, 2 * W_B:3 * W_B]
    o = 3 * W_B
    xw = x[:, o:o + DECAY_LORA]
    xa = x[:, o + DECAY_LORA:o + DECAY_LORA + ICLR_LORA]
    xg = x[:, o + DECAY_LORA + ICLR_LORA:]
    log_w = -_softplus(-(dbase_ref[...] + _bdot(jnp.tanh(xw), dup_ref[...]))) - 0.5
    lw = -jnp.exp(log_w)
    a = _sigmoid(ibase_ref[...] + _bdot(xa, iup_ref[...]))
    g = _bdot(_sigmoid(xg), gup_ref[...])
    hsum = hsum_ref[...]
    kkf = k * kk_ref[...]
    kk = kkf / jnp.maximum(jnp.sqrt(_split_dot(kkf * kkf, hsum)), 1e-12)
    k2 = k * (1.0 + (a - 1.0) * ka_ref[...])
    bonus = _split_dot(r * k2 * rk_ref[...], hsum) * v
    na = -kk
    kb = kk * a
    if t_real < tt:
        valid = (lax.broadcasted_iota(i32, (nb, tt, W_B), 1) < t_real).reshape(nb * tt, W_B)
        zero = lambda u: jnp.where(valid, u, 0.0)
        r, lw, k2, v, na, kb = zero(r), zero(lw), zero(k2), zero(v), zero(na), zero(kb)
    g_out[...] = g.reshape(nb, tt, W_B)
    bonus_out[...] = bonus.reshape(nb, tt, W_B)
    for val, ref in ((r, r_out), (lw, lw_out), (k2, k_out), (v, v_out), (na, a_out), (kb, b_out)):
        val = val.reshape(nb, tt, W_B)
        for h in range(H_B):
            ref[:, h, :, :] = val[:, :, h * HEAD_DIM:(h + 1) * HEAD_DIM]


def _rwkv_prep(pb, prev_first, p, nb, tt, t_real):
    b, t, w = pb.shape
    row = lambda i, j: (i, j, 0)
    const = lambda i, j: (0, 0)
    vec = lambda a: a.reshape(1, -1)
    head_of = jnp.arange(W_B) // HEAD_DIM
    hsum = (head_of[:, None] == head_of[None, :]).astype(bf16)
    params = [vec(p['mu_shift']), vec(p['decay_base']), p['decay_up'], vec(p['iclr_base']), p['iclr_up'],
              p['gate_up'], vec(p['k_k']), vec(p['k_a']), vec(p['r_k']), hsum]
    heads = pl.BlockSpec((nb, H_B, tt, HEAD_DIM), lambda i, j: (i, 0, j, 0))
    return pl.pallas_call(
        functools.partial(_prep_kernel, t_real=t_real),
        grid=(b // nb, t // tt),
        in_specs=[pl.BlockSpec((nb, tt, w), row),
                  pl.BlockSpec((nb, 1, 1, w), lambda i, j: (i, j, 0, 0))]
                 + [pl.BlockSpec(a.shape, const) for a in params],
        out_specs=[heads] * 6 + [pl.BlockSpec((nb, tt, W_B), row)] * 2,
        out_shape=[SDS((b, H_B, t, HEAD_DIM), f32)] * 6 + [SDS((b, t, W_B), f32)] * 2,
        compiler_params=_cparams(("arbitrary", "arbitrary")),
        name="rwkv_prep",
    )(pb, prev_first, *params)


def _wkv_kernel(r_ref, lw_ref, k_ref, v_ref, a_ref, b_ref, s0_ref, y_ref, s_out, st_ref, *, chunk):
    c = chunk
    n = HEAD_DIM
    step = pl.program_id(1)

    n_chain = lw_ref.shape[0] * H_B
    of = lambda ref, i: ref[i // H_B, i % H_B]

    @pl.when(step == 0)
    def _():
        for i in range(n_chain):
            st_ref[i] = of(s0_ref, i).T

    ti = lax.broadcasted_iota(i32, (c, c), 0)
    si = lax.broadcasted_iota(i32, (c, c), 1)
    tri = (ti >= si).astype(bf16)
    eye = (ti == si).astype(f32)
    ones = jnp.ones((c, n), bf16)
    row2 = lax.broadcasted_iota(i32, (c, 2 * c), 0)
    col2 = lax.broadcasted_iota(i32, (c, 2 * c), 1)
    strict2 = (col2 % c) < row2
    incl2 = (col2 % c) <= row2
    right = col2 >= c
    levels = max(c.bit_length() - 2, 0)
    heads = range(n_chain)
    each = lambda f: [f(h) for h in heads]
    lw = each(lambda h: of(lw_ref, h))
    lw3 = each(lambda h: _split3(lw[h]))
    cum = each(lambda h: sum(_dg(tri, t, _NN) for t in lw3[h]))
    wsum = each(lambda h: sum(_dg(t, ones, _TN) for t in lw3[h]))
    w_in = each(lambda h: jnp.exp(cum[h]))
    w_out = each(lambda h: jnp.exp(-cum[h]))
    at = each(lambda h: of(a_ref, h) * jnp.exp(cum[h] - lw[h]))
    rt = each(lambda h: of(r_ref, h) * w_in[h])
    bk = each(lambda h: jnp.concatenate([of(b_ref, h) * w_out[h], of(k_ref, h) * w_out[h]], axis=0))
    g = each(lambda h: _dot3(jnp.concatenate([at[h], rt[h]], axis=0), bk[h], _NT))
    top = each(lambda h: jnp.where(strict2, g[h][:c], 0.0))
    bot = each(lambda h: jnp.where(incl2, g[h][c:], 0.0))
    pw = each(lambda h: top[h][:, :c])
    inv = each(lambda h: eye + pw[h])
    for _ in range(levels):
        pw = each(lambda h: _dot3(pw[h], pw[h]))
        inv = each(lambda h: inv[h] + _dot3(inv[h], pw[h]))
    vv = each(lambda h: jnp.concatenate([of(v_ref, h), of(v_ref, h)], axis=0))
    xv = each(lambda h: _dot3(jnp.where(right, top[h], 0.0), vv[h]))
    x = each(lambda h: _dot3(at[h], st_ref[h]) + xv[h])
    u = each(lambda h: _dot3(inv[h], x[h]))
    uv = each(lambda h: jnp.concatenate([u[h], of(v_ref, h)], axis=0))
    y = each(lambda h: _bdot(bot[h], uv[h]) + _bdot(rt[h], st_ref[h]))
    st_new = each(lambda h: (st_ref[h] + _dot3(bk[h], uv[h], _TN)) * jnp.exp(wsum[h]))
    for h in heads:
        st_ref[h] = st_new[h]
        mu = jnp.mean(y[h], axis=-1, keepdims=True)
        yc = y[h] - mu
        var = jnp.mean(yc * yc, axis=-1, keepdims=True)
        col0 = (h % H_B) * HEAD_DIM
        y_ref[h // H_B, :, col0:col0 + HEAD_DIM] = yc * lax.rsqrt(var + LNX_EPS)

    @pl.when(step == pl.num_programs(1) - 1)
    def _():
        for i in range(n_chain):
            s_out[i // H_B, i % H_B] = st_ref[i].T


def _wkv_scan(r, lw, k, v, a, b, s0, chunk, nseq):
    bsz, h, t, n = r.shape
    seq = pl.BlockSpec((nseq, h, chunk, n), lambda i, j: (i, 0, j, 0))
    state = pl.BlockSpec((nseq, h, n, n), lambda i, j: (i, 0, 0, 0))
    return pl.pallas_call(
        functools.partial(_wkv_kernel, chunk=chunk),
        grid=(bsz // nseq, t // chunk),
        in_specs=[seq] * 6 + [state],
        out_specs=[pl.BlockSpec((nseq, chunk, h * n), lambda i, j: (i, j, 0)), state],
        out_shape=[SDS((bsz, t, h * n), f32), SDS((bsz, h, n, n), f32)],
        scratch_shapes=[pltpu.VMEM((nseq * h, n, n), f32)],
        compiler_params=_cparams(("arbitrary", "arbitrary")),
        name="wkv_scan",
    )(r, lw, k, v, a, b, s0)


def _outproj_kernel(attn_ref, yn_ref, bonus_ref, g_ref, x_ref, gt1_ref, sh2_ref, sc2_ref,
                    gattn_ref, lnxg_ref, lnxb_ref, wout_ref, gpost_ref, gpre_ref, rw_ref, rb_ref,
                    x1_ref, h2_ref, logit_ref):
    nb, tt, d = x_ref.shape
    o_a = _rms(attn_ref[...], gattn_ref[...])
    o_b = (yn_ref[...] * lnxg_ref[...] + lnxb_ref[...] + bonus_ref[...]) * g_ref[...]
    cat = jnp.concatenate([o_a, o_b], axis=-1).reshape(nb * tt, d)
    catb = cat.astype(bf16)
    half = d // 2
    mixed = jnp.concatenate([jnp.dot(catb, wout_ref[:, :half], preferred_element_type=f32),
                             jnp.dot(catb, wout_ref[:, half:], preferred_element_type=f32)],
                            axis=1).reshape(nb, tt, d)
    x1 = x_ref[...] + gt1_ref[...] * _rms(mixed, gpost_ref[...])
    x1_ref[...] = x1
    h2 = _rms(x1, gpre_ref[...]) * (1.0 + sc2_ref[...]) + sh2_ref[...]
    _store_token_tiles(h2_ref, 0, h2.reshape(nb * tt, d))
    logits = _dot3(h2.reshape(nb * tt, d), rw_ref[...]) + rb_ref[...]
    logit_ref[...] = logits.reshape(nb, tt, N_EXPERTS)


def _out_proj(attn, yn, bonus, g, x, gt1, sh2, sc2, p, w_out_bf16, nb, tt):
    b, t, d = x.shape
    row = lambda w: pl.BlockSpec((nb, tt, w), lambda i, j: (i, j, 0))
    mod = pl.BlockSpec((nb, 1, d), lambda i, j: (i, 0, 0))
    vec = lambda a: a.reshape(1, -1)
    params = [vec(p['g_attn_out']), vec(p['lnx_g']), vec(p['lnx_b']), w_out_bf16,
              vec(p['g_mix_post']), vec(p['g_ffn_pre']), p['router_w'], vec(p['router_b'])]
    return pl.pallas_call(
        _outproj_kernel,
        grid=(b // nb, t // tt),
        in_specs=[row(W_A), row(W_B), row(W_B), row(W_B), row(d), mod, mod, mod]
                 + [pl.BlockSpec(a.shape, lambda i, j: (0, 0)) for a in params],
        out_specs=[row(d), pl.BlockSpec((nb, tt, V7X_SUBLANES, V7X_LANES), lambda i, j: (i, j, 0, 0)),
                   row(N_EXPERTS)],
        out_shape=[SDS((b, t, d), f32), SDS((b, t, V7X_SUBLANES, V7X_LANES), f32),
                   SDS((b, t, N_EXPERTS), f32)],
        compiler_params=_cparams(("arbitrary", "arbitrary")),
        name="out_proj",
    )(attn, yn, bonus, g, x, gt1, sh2, sc2, *params)


def _route_kernel(logit_ref, idx_ref, gate_ref, rank_ref, count_ref, base_ref):
    tile = logit_ref.shape[0]

    @pl.when(pl.program_id(0) == 0)
    def _():
        base_ref[...] = jnp.zeros_like(base_ref)

    lane = lax.broadcasted_iota(i32, (tile, N_EXPERTS), 1).astype(f32)
    cur = logit_ref[...]
    hots, vals, idxs = [], [], []
    for _ in range(TOP_K):
        top = jnp.max(cur, axis=-1, keepdims=True)
        idx = jnp.min(jnp.where(cur == top, lane, float(N_EXPERTS)), axis=-1, keepdims=True)
        hot = lane == idx
        hots.append(hot)
        vals.append(top)
        idxs.append(idx)
        cur = jnp.where(hot, -jnp.inf, cur)
    es = [jnp.exp(v - vals[0]) for v in vals]
    total = sum(es)
    chosen = sum(h.astype(f32) for h in hots)
    ti = lax.broadcasted_iota(i32, (tile, tile), 0)
    si = lax.broadcasted_iota(i32, (tile, tile), 1)
    earlier = jnp.dot((ti > si).astype(bf16), chosen.astype(bf16), preferred_element_type=f32) + base_ref[...]
    ranks = [jnp.sum(jnp.where(h, earlier, 0.0), axis=-1, keepdims=True) for h in hots]
    base_ref[...] += jnp.sum(chosen, axis=0, keepdims=True)
    count_ref[...] = base_ref[...]
    col = lax.broadcasted_iota(i32, (tile, TOP_K), 1)
    pick = lambda parts: sum(jnp.where(col == k, parts[k], 0.0) for k in range(TOP_K))
    idx_ref[...] = pick(idxs).astype(i32)
    gate_ref[...] = pick([e / total for e in es])
    rank_ref[...] = pick(ranks).astype(i32)


def _route(logits, tile):
    m = logits.shape[0]
    tok = lambda dt: SDS((m, TOP_K), dt)
    blk = pl.BlockSpec((tile, TOP_K), lambda i: (i, 0))
    return pl.pallas_call(
        _route_kernel,
        grid=(m // tile,),
        in_specs=[pl.BlockSpec((tile, N_EXPERTS), lambda i: (i, 0))],
        out_specs=[blk, blk, blk, pl.BlockSpec((1, N_EXPERTS), lambda i: (0, 0))],
        out_shape=[tok(i32), tok(f32), tok(i32), SDS((1, N_EXPERTS), f32)],
        scratch_shapes=[pltpu.VMEM((1, N_EXPERTS), f32)],
        compiler_params=_cparams(("arbitrary",)),
        name="moe_route",
    )(logits)


def _row_copy(src, src_row, dst, dst_row, sem):
    return pltpu.make_async_copy(src.at[pl.ds(src_row, 1)], dst.at[pl.ds(dst_row, 1)], sem)


def _dispatch_kernel(dest_ref, xa_ref, xb_ref, xs_in, xs_out, sem, *, n_first):
    del xs_in
    tile = xa_ref.shape[0]

    def scatter_rows(x_ref):
        def start(r, _):
            for k in range(TOP_K):
                _row_copy(x_ref, r, xs_out, dest_ref[k, r], sem).start(priority=k % 2)
            return 0

        lax.fori_loop(0, tile, start, 0, unroll=DMA_ISSUE_UNROLL)
        for k in range(TOP_K):
            pltpu.make_async_copy(x_ref, xs_out.at[pl.ds(0, tile)], sem).wait()

    @pl.when(pl.program_id(0) < n_first)
    def _():
        scatter_rows(xa_ref)

    @pl.when(pl.program_id(0) >= n_first)
    def _():
        scatter_rows(xb_ref)


def _dispatch(xa, xb, dest_t, n_slots, tile):
    row_tile = xa.shape[1:]
    n_first = xa.shape[0] // tile
    return pl.pallas_call(
        functools.partial(_dispatch_kernel, n_first=n_first),
        grid=(n_first + xb.shape[0] // tile,),
        in_specs=[pl.BlockSpec((TOP_K, tile), lambda i: (0, i), memory_space=pltpu.SMEM),
                  pl.BlockSpec((tile,) + row_tile, lambda i: (jnp.minimum(i, n_first - 1), 0, 0)),
                  pl.BlockSpec((tile,) + row_tile, lambda i: (jnp.maximum(i - n_first, 0), 0, 0)),
                  pl.BlockSpec(memory_space=pl.ANY)],
        out_specs=pl.BlockSpec(memory_space=pl.ANY),
        out_shape=SDS((n_slots,) + row_tile, xa.dtype),
        scratch_shapes=[pltpu.SemaphoreType.DMA(())],
        input_output_aliases={3: 0},
        compiler_params=_cparams(("arbitrary",)),
        name="moe_dispatch",
    )(dest_t, xa, xb, jnp.zeros((n_slots,) + row_tile, xa.dtype))


def _combine_kernel(dest_ref, gate_ref, ys_hbm, fa_ref, fb_ref, buf_ref, sem, *, n_first):
    tile = fa_ref.shape[0]

    def start(r, _):
        for k in range(TOP_K):
            _row_copy(ys_hbm, dest_ref[k, r], buf_ref.at[k], r, sem).start(priority=k % 2)
        return 0

    lax.fori_loop(0, tile, start, 0, unroll=DMA_ISSUE_UNROLL)
    for k in range(TOP_K):
        pltpu.make_async_copy(ys_hbm.at[pl.ds(0, tile)], buf_ref.at[k], sem).wait()
    gate = gate_ref[...]
    f = sum(gate[:, k:k + 1] * _load_token_tiles(buf_ref, k * tile, tile) for k in range(TOP_K))

    @pl.when(pl.program_id(0) < n_first)
    def _():
        fa_ref[...] = f

    @pl.when(pl.program_id(0) >= n_first)
    def _():
        fb_ref[...] = f


def _combine(ys, dest_t, gate, m_first, tile):
    m = gate.shape[0]
    d = math.prod(ys.shape[1:])
    n_first = m_first // tile
    return pl.pallas_call(
        functools.partial(_combine_kernel, n_first=n_first),
        grid=(m // tile,),
        in_specs=[pl.BlockSpec((TOP_K, tile), lambda i: (0, i), memory_space=pltpu.SMEM),
                  pl.BlockSpec((tile, TOP_K), lambda i: (i, 0)),
                  pl.BlockSpec(memory_space=pl.ANY)],
        out_specs=[pl.BlockSpec((tile, d), lambda i: (jnp.minimum(i, n_first - 1), 0)),
                   pl.BlockSpec((tile, d), lambda i: (jnp.maximum(i - n_first, 0), 0))],
        out_shape=[SDS((m_first, d), f32), SDS((m - m_first, d), f32)],
        scratch_shapes=[pltpu.VMEM((TOP_K, tile) + ys.shape[1:], f32), pltpu.SemaphoreType.DMA(())],
        compiler_params=_cparams(("arbitrary",)),
        name="moe_combine",
    )(dest_t, gate, ys)


def _expert_kernel(be_ref, nused_ref, x_ref, w1_ref, b1_ref, w2_ref, b2_ref, y_ref, w1b_ref, w2b_ref):
    i = pl.program_id(0)
    prev = be_ref[jnp.maximum(i - 1, 0)]

    @pl.when((i == 0) | (be_ref[i] != prev))
    def _():
        w1b_ref[...] = w1_ref[0].astype(bf16)
        w2b_ref[...] = w2_ref[0].astype(bf16)

    @pl.when(i < nused_ref[0])
    def _():
        x = _load_token_tiles(x_ref, 0, x_ref.shape[0]).astype(bf16)
        u = jnp.dot(x, w1b_ref[...], preferred_element_type=f32) + b1_ref[0]
        u_glu = jnp.minimum(u[:, :D_FF], SWIGLU_LIMIT)
        u_lin = jnp.clip(u[:, D_FF:], -SWIGLU_LIMIT, SWIGLU_LIMIT)
        act = u_glu * _sigmoid(SWIGLU_ALPHA * u_glu) * (u_lin + 1.0)
        y = jnp.dot(act.astype(bf16), w2b_ref[...], preferred_element_type=f32) + b2_ref[0]
        _store_token_tiles(y_ref, 0, y)

    @pl.when(i >= nused_ref[0])
    def _():
        y_ref[...] = jnp.zeros_like(y_ref)


def _expert_ffn(xs, blk_expert, n_used, e_w1, e_b1, e_w2, e_b2, tm):
    n_slots = xs.shape[0]
    row_tile = xs.shape[1:]
    d = math.prod(row_tile)
    n_blocks = n_slots // tm
    rows = pl.BlockSpec((tm,) + row_tile, lambda i, be, nu: (i, 0, 0))
    ex = lambda i, be, nu: (be[i], 0, 0)
    grid_spec = pltpu.PrefetchScalarGridSpec(
        num_scalar_prefetch=2,
        grid=(n_blocks,),
        in_specs=[rows,
                  pl.BlockSpec((1, d, 2 * D_FF), ex),
                  pl.BlockSpec((1, 1, 2 * D_FF), ex),
                  pl.BlockSpec((1, D_FF, d), ex),
                  pl.BlockSpec((1, 1, d), ex)],
        out_specs=rows,
        scratch_shapes=[pltpu.VMEM((d, 2 * D_FF), bf16), pltpu.VMEM((D_FF, d), bf16)],
    )
    return pl.pallas_call(
        _expert_kernel,
        grid_spec=grid_spec,
        out_shape=SDS((n_slots,) + row_tile, f32),
        compiler_params=_cparams(("arbitrary",)),
        name="expert_ffn",
    )(blk_expert, n_used, xs, e_w1, e_b1.reshape(N_EXPERTS, 1, -1), e_w2, e_b2.reshape(N_EXPERTS, 1, -1))


def _moe(h2_a, logits_a, h2_b, logits_b, e_w1, e_b1, e_w2, e_b2, tm):
    tile = MOE_TOKEN_TILE
    m_a, m_b = h2_a.shape[0], h2_b.shape[0]
    assert m_a % tile == 0
    extra = -m_b % tile
    h2_b = jnp.pad(h2_b, ((0, extra), (0, 0), (0, 0)))
    logits = jnp.concatenate([logits_a, logits_b, jnp.zeros((extra, N_EXPERTS), f32)], axis=0)
    m = m_a + m_b + extra
    top_idx, gate, rank, counts = _route(logits, tile)
    counts = counts.reshape(N_EXPERTS).astype(i32)
    padded = (counts + tm - 1) // tm * tm
    pad_end = jnp.cumsum(padded)
    pad_start = pad_end - padded
    n_blocks = (m * TOP_K + N_EXPERTS * (tm - 1) + tm - 1) // tm
    blk_row = (jnp.arange(n_blocks) * tm)[:, None]
    blk_expert = jnp.minimum(jnp.sum(pad_end[None, :] <= blk_row, axis=1), N_EXPERTS - 1).astype(i32)
    n_used = (pad_end[-1] // tm).astype(i32).reshape(1)
    group_start = jnp.sum(jnp.where(top_idx[:, :, None] == jnp.arange(N_EXPERTS), pad_start, 0), axis=-1)
    dest_t = (group_start + rank).astype(i32).T
    xs = _dispatch(h2_a, h2_b, dest_t, n_blocks * tm, tile)
    ys = _expert_ffn(xs, blk_expert, n_used, e_w1, e_b1, e_w2, e_b2, tm)
    f_a, f_b = _combine(ys, dest_t, gate, m_a, tile)
    return f_a, f_b[:m_b]


def _final_kernel(x1_ref, f_ref, gt2_ref, g_ref, o_ref):
    o_ref[...] = x1_ref[...] + gt2_ref[...] * _rms(f_ref[...], g_ref[...])


def _final(x1, f, gt2, g, nb, tt):
    b, t, d = x1.shape
    row = pl.BlockSpec((nb, tt, d), lambda i, j: (i, j, 0))
    return pl.pallas_call(
        _final_kernel,
        grid=(b // nb, t // tt),
        in_specs=[row, row, pl.BlockSpec((nb, 1, d), lambda i, j: (i, 0, 0)),
                  pl.BlockSpec((1, d), lambda i, j: (0, 0))],
        out_specs=row,
        out_shape=SDS((b, t, d), f32),
        compiler_params=_cparams(("arbitrary", "arbitrary")),
        name="final_residual",
    )(x1, f, gt2, g.reshape(1, d))


def _mix_layer(x, mod, attend, shift_prev, wkv_prev, p, w_in_bf16, w_out_bf16, nb, tt, t_real, chunk, wkv_nseq):
    b, t, d = x.shape
    sh1, sc1, gt1, sh2, sc2, gt2 = [mod[:, i:i + 1, :] for i in range(6)]
    q, k, v, pb, last = _in_proj(x, sh1, sc1, p['g_mix_pre'], w_in_bf16, nb, tt, t_real)
    attn = attend(q, k, v)
    prev_first = jnp.concatenate([shift_prev[:, None, None, :], last[:, :-1]], axis=1)
    r, lw, k2, v2, a, kb, g, bonus = _rwkv_prep(pb, prev_first, p, nb, tt, t_real)
    yn, wkv_new = _wkv_scan(r, lw, k2, v2, a, kb, wkv_prev, chunk, wkv_nseq)
    x1, h2, logits = _out_proj(attn, yn, bonus, g, x, gt1, sh2, sc2, p, w_out_bf16, nb, tt)
    return x1, h2, logits, gt2, k, v, wkv_new, last[:, -1, 0]


def kernel(x_prompt, x_sample, cache_k, cache_v, state_wkv, state_shift, page_table, c_prompt, c_sample,
           w_ada, b_ada, g_mix_pre, g_mix_post, g_ffn_pre, g_ffn_post, w_in, mu_shift,
           decay_base, decay_up, iclr_base, iclr_up, gate_up, k_k, k_a, r_k, lnx_g, lnx_b,
           g_attn_out, sb_bias, w_out, router_w, router_b, e_w1, e_b1, e_w2, e_b2):
    weights = dict(w_ada=w_ada, b_ada=b_ada, g_mix_pre=g_mix_pre, g_mix_post=g_mix_post,
                   g_ffn_pre=g_ffn_pre, g_ffn_post=g_ffn_post, w_in=w_in, mu_shift=mu_shift,
                   decay_base=decay_base, decay_up=decay_up, iclr_base=iclr_base, iclr_up=iclr_up,
                   gate_up=gate_up, k_k=k_k, k_a=k_a, r_k=r_k, lnx_g=lnx_g, lnx_b=lnx_b,
                   g_attn_out=g_attn_out, sb_bias=sb_bias, w_out=w_out, router_w=router_w,
                   router_b=router_b, e_w1=e_w1, e_b1=e_b1, e_w2=e_w2, e_b2=e_b2)
    depth = w_ada.shape[0]
    bp, tp, d = x_prompt.shape
    bs, ts, _ = x_sample.shape
    ts_pad = -(-ts // V7X_SUBLANES) * V7X_SUBLANES
    hp = x_prompt
    hs = jnp.pad(x_sample, ((0, 0), (0, ts_pad - ts), (0, 0)))
    outs = [[] for _ in range(8)]
    for l in range(depth):
        p = {name: w[l] for name, w in weights.items()}
        w_in_bf16 = p['w_in'].astype(bf16)
        w_out_bf16 = p['w_out'].astype(bf16)
        mod = _ada_mod(jnp.concatenate([c_prompt, c_sample], axis=0), p['w_ada'], p['b_ada'])
        mod = mod.reshape(bp + bs, 6, d)

        attend_p = lambda q, k, v: _attn_prompt(q, k, v, p['sb_bias'])
        x1p, h2p, lgp, gt2p, kp, vp, wp, shp = _mix_layer(
            hp, mod[:bp], attend_p, jnp.zeros((bp, SHIFT_W), f32),
            jnp.zeros((bp, H_B, HEAD_DIM, HEAD_DIM), f32), p, w_in_bf16, w_out_bf16,
            nb=1, tt=ROW_TILE, t_real=tp, chunk=WKV_CHUNK, wkv_nseq=WKV_NSEQ)

        attend_s = lambda q, k, v: jnp.pad(
            _attn_sample(q, k, v, cache_k, cache_v, l, page_table, p['sb_bias'], ts),
            ((0, 0), (0, ts_pad - ts), (0, 0)))
        x1s, h2s, lgs, gt2s, ks, vs, ws, shs = _mix_layer(
            hs, mod[bp:], attend_s, state_shift[l], state_wkv[l], p, w_in_bf16, w_out_bf16,
            nb=bs, tt=ts_pad, t_real=ts, chunk=ts_pad, wkv_nseq=WKV_NSEQ)

        mp = bp * tp
        fp, fs = _moe(h2p.reshape((mp,) + h2p.shape[2:]), lgp.reshape(mp, N_EXPERTS),
                      h2s[:, :ts].reshape((bs * ts,) + h2s.shape[2:]), lgs[:, :ts].reshape(bs * ts, N_EXPERTS),
                      p['e_w1'], p['e_b1'], p['e_w2'], p['e_b2'], MOE_TILE)
        fp = fp.reshape(bp, tp, d)
        fs = jnp.pad(fs.reshape(bs, ts, d), ((0, 0), (0, ts_pad - ts), (0, 0)))
        hp = _final(x1p, fp, gt2p, p['g_ffn_post'], 1, ROW_TILE)
        hs = _final(x1s, fs, gt2s, p['g_ffn_post'], bs, ts_pad)

        for lst, val in zip(outs, (kp.reshape(bp, tp, H_A, HEAD_DIM), vp.reshape(bp, tp, H_A, HEAD_DIM),
                                   ks[:, :ts].reshape(bs, ts, H_A, HEAD_DIM),
                                   vs[:, :ts].reshape(bs, ts, H_A, HEAD_DIM), wp, ws, shp, shs)):
            lst.append(val)
    return (hp, hs[:, :ts]) + tuple(jnp.stack(lst) for lst in outs)
```

```python
import functools
import math

import jax
import jax.numpy as jnp
from jax import lax
from jax.experimental import pallas as pl
from jax.experimental.pallas import tpu as pltpu

f32 = jnp.float32
bf16 = jnp.bfloat16
i32 = jnp.int32
SDS = jax.ShapeDtypeStruct

D_MODEL = 1024
HEAD_DIM = 64
W_A = 512
W_B = 512
H_A = W_A // HEAD_DIM
H_B = W_B // HEAD_DIM
DECAY_LORA = 64
ICLR_LORA = 64
GATE_LORA = 128
SHIFT_W = 3 * W_B + DECAY_LORA + ICLR_LORA + GATE_LORA
N_EXPERTS = 32
TOP_K = 4
D_FF = D_MODEL
SWIGLU_ALPHA = 1.702
SWIGLU_LIMIT = 7.0
PAGE_SIZE = 128
RMS_EPS = 1e-6
LNX_EPS = 64e-5
LOG2E = 1.4426950408889634

V7X_SUBLANES = 8
V7X_LANES = 128
VMEM_LIMIT = 56 * 1024 * 1024

ROW_TILE = 256
WKV_CHUNK = 64
WKV_NSEQ = 2
ATTN_TQ = 512
ATTN_TK = 128
ATTN_NSUB = 4
PAGES_PER_STEP = 16
MOE_TILE = 256
MOE_TOKEN_TILE = 128
DMA_ISSUE_UNROLL = 8


def _cparams(sem):
    return pltpu.CompilerParams(dimension_semantics=sem, vmem_limit_bytes=VMEM_LIMIT)


def _bdot(a, b):
    return jnp.dot(a.astype(bf16), b.astype(bf16), preferred_element_type=f32)


def _fdot(a, b, dims=(((1,), (0,)), ((), ()))):
    return lax.dot_general(a, b, dims, precision=lax.Precision.HIGHEST, preferred_element_type=f32)


_NT = (((1,), (1,)), ((), ()))
_TN = (((0,), (0,)), ((), ()))


def _split_dot(x, m_bf16):
    hi = x.astype(bf16)
    lo = (x - hi.astype(f32)).astype(bf16)
    return (jnp.dot(hi, m_bf16, preferred_element_type=f32)
            + jnp.dot(lo, m_bf16, preferred_element_type=f32))


def _sigmoid(x):
    return 1.0 / (1.0 + jnp.exp(-x))


def _softplus(x):
    return jnp.maximum(x, 0.0) + jnp.log(1.0 + jnp.exp(-jnp.abs(x)))


def _softplus2(z):
    return jnp.maximum(z, 0.0) + jnp.log2(1.0 + jnp.exp2(-jnp.abs(z)))


def _split2(x):
    hi = x.astype(bf16)
    return hi, (x - hi.astype(f32)).astype(bf16)


def _split3(x):
    hi = x.astype(bf16)
    r = x - hi.astype(f32)
    mid = r.astype(bf16)
    return hi, mid, (r - mid.astype(f32)).astype(bf16)


def _dg(a, b, dims):
    return lax.dot_general(a, b, dims, preferred_element_type=f32)


_NN = (((1,), (0,)), ((), ()))


def _dot3(a, b, dims=_NN):
    ah, al = _split2(a)
    bh, bl = _split2(b)
    return _dg(ah, bh, dims) + _dg(ah, bl, dims) + _dg(al, bh, dims)


def _dot_exact_rhs(x3, m_bf16, dims=_NN):
    return _dg(x3[0], m_bf16, dims) + _dg(x3[1], m_bf16, dims) + _dg(x3[2], m_bf16, dims)


def _rms(x, g):
    return x * lax.rsqrt(jnp.mean(x * x, axis=-1, keepdims=True) + RMS_EPS) * g


def _store_token_tiles(ref, first_row, x2d):
    rows = x2d.shape[0]
    flat = ref.reshape(math.prod(ref.shape) // V7X_LANES, V7X_LANES)
    for c in range(V7X_SUBLANES):
        flat[pl.ds(first_row * V7X_SUBLANES + c, rows, stride=V7X_SUBLANES), :] = (
            x2d[:, c * V7X_LANES:(c + 1) * V7X_LANES])


def _load_token_tiles(ref, first_row, rows):
    flat = ref.reshape(math.prod(ref.shape) // V7X_LANES, V7X_LANES)
    return jnp.concatenate([flat[pl.ds(first_row * V7X_SUBLANES + c, rows, stride=V7X_SUBLANES), :]
                            for c in range(V7X_SUBLANES)], axis=1)


def _ada_kernel(c_ref, w_ref, b_ref, o_ref):
    c = c_ref[...]
    o_ref[...] = _bdot(c * _sigmoid(c), w_ref[...]) + b_ref[...]


def _ada_mod(c, w_ada, b_ada):
    n, d = c.shape
    nout = w_ada.shape[1]
    tn = 1536
    return pl.pallas_call(
        _ada_kernel,
        grid=(nout // tn,),
        in_specs=[pl.BlockSpec((n, d), lambda j: (0, 0)),
                  pl.BlockSpec((d, tn), lambda j: (0, j)),
                  pl.BlockSpec((1, tn), lambda j: (0, j))],
        out_specs=pl.BlockSpec((n, tn), lambda j: (0, j)),
        out_shape=SDS((n, nout), f32),
        compiler_params=_cparams(("arbitrary",)),
        name="ada_mod",
    )(c, w_ada, b_ada.reshape(1, nout))


def _inproj_kernel(x_ref, sh_ref, sc_ref, g_ref, w_ref, q_ref, k_ref, v_ref, pb_ref, last_ref, *, t_last):
    nb, tt, d = x_ref.shape
    h = _rms(x_ref[...], g_ref[...]) * (1.0 + sc_ref[...]) + sh_ref[...]
    hb = h.reshape(nb * tt, d).astype(bf16)
    q_ref[...] = jnp.dot(hb, w_ref[:, 0:W_A], preferred_element_type=f32).reshape(nb, tt, W_A)
    k_ref[...] = jnp.dot(hb, w_ref[:, W_A:2 * W_A], preferred_element_type=f32).reshape(nb, tt, W_A)
    v_ref[...] = jnp.dot(hb, w_ref[:, 2 * W_A:3 * W_A], preferred_element_type=f32).reshape(nb, tt, W_A)
    pb = jnp.dot(hb, w_ref[:, 3 * W_A:], preferred_element_type=f32).reshape(nb, tt, SHIFT_W)
    pb_ref[...] = pb
    last_ref[:, 0] = pb[:, t_last:t_last + 1, :]


def _in_proj(x, sh1, sc1, g, w_in_bf16, nb, tt, t_real):
    b, t, d = x.shape
    row = lambda w: pl.BlockSpec((nb, tt, w), lambda i, j: (i, j, 0))
    mod = pl.BlockSpec((nb, 1, d), lambda i, j: (i, 0, 0))
    return pl.pallas_call(
        functools.partial(_inproj_kernel, t_last=min(tt, t_real) - 1),
        grid=(b // nb, t // tt),
        in_specs=[row(d), mod, mod,
                  pl.BlockSpec((1, d), lambda i, j: (0, 0)),
                  pl.BlockSpec(w_in_bf16.shape, lambda i, j: (0, 0))],
        out_specs=[row(W_A), row(W_A), row(W_A), row(SHIFT_W),
                   pl.BlockSpec((nb, 1, 1, SHIFT_W), lambda i, j: (i, j, 0, 0))],
        out_shape=[SDS((b, t, W_A), f32)] * 3 + [SDS((b, t, SHIFT_W), f32),
                                                  SDS((b, t // tt, 1, SHIFT_W), f32)],
        compiler_params=_cparams(("arbitrary", "arbitrary")),
        name="in_proj",
    )(x, sh1, sc1, g.reshape(1, d), w_in_bf16)


def _attn_prompt_kernel(bias_ref, q_ref, k_ref, v_ref, u2_ref, o_ref, kbd_ref, vbd_ref, acc_ref,
                        *, tq, tk, nsub):
    hp = pl.program_id(1)
    qi = pl.program_id(2)
    two = 2 * tk
    big = tk * nsub
    ratio = tq // big
    n_tiles = k_ref.shape[1] // tk

    @pl.when(qi == 0)
    def _():
        first = lax.broadcasted_iota(i32, (tk, 2 * HEAD_DIM), 1) < HEAD_DIM

        def build(j, _):
            start = pl.multiple_of(j * tk, tk)
            kt = k_ref[0, pl.ds(start, tk), :]
            vt = v_ref[0, pl.ds(start, tk), :]
            kbd_ref[j, 0:tk, :] = jnp.where(first, kt, 0.0).astype(bf16)
            kbd_ref[j, tk:two, :] = jnp.where(first, 0.0, kt).astype(bf16)
            vbd_ref[j, 0:tk, :] = jnp.where(first, vt, 0.0).astype(bf16)
            vbd_ref[j, tk:two, :] = jnp.where(first, 0.0, vt).astype(bf16)
            return 0

        lax.fori_loop(0, n_tiles, build, 0)

    qb = (q_ref[0] * (LOG2E * HEAD_DIM ** -0.5)).astype(bf16)
    lane2 = lax.broadcasted_iota(i32, (1, two), 1)
    bias2 = jnp.where(lane2 < tk, bias_ref[2 * hp], bias_ref[2 * hp + 1]) * LOG2E
    bias_row = jnp.concatenate([bias2] * nsub, axis=1)
    u2 = u2_ref[...]
    acc_ref[...] = jnp.zeros_like(acc_ref)
    row = lax.broadcasted_iota(i32, (tq, nsub * two), 0)
    col = lax.broadcasted_iota(i32, (tq, nsub * two), 1)
    key_off = (col // two) * tk + col % tk

    def chunk(tile0, mask, carry):
        kb = kbd_ref[pl.ds(tile0, nsub)].reshape(nsub * two, 2 * HEAD_DIM)
        vb = vbd_ref[pl.ds(tile0, nsub)].reshape(nsub * two, 2 * HEAD_DIM)
        z = _dg(qb, kb, _NT) + bias_row
        sp = _softplus2(z)
        if mask is not None:
            sp = jnp.where(mask, sp, 0.0)
        spb = sp.astype(bf16)
        ws = [None] * nsub
        for j in range(nsub - 1, -1, -1):
            incl = jnp.dot(spb[:, j * two:(j + 1) * two], u2, preferred_element_type=f32)
            ws[j] = jnp.exp2(z[:, j * two:(j + 1) * two] - incl - carry)
            carry = carry + jnp.concatenate([jnp.broadcast_to(incl[:, 0:1], (tq, tk)),
                                             jnp.broadcast_to(incl[:, tk:tk + 1], (tq, tk))], axis=1)
        w = jnp.concatenate(ws, axis=1)
        if mask is not None:
            w = jnp.where(mask, w, 0.0)
        acc_ref[...] += jnp.dot(w.astype(bf16), vb, preferred_element_type=f32)
        return carry

    carry = jnp.zeros((tq, two), f32)
    for dgl in range(ratio - 1, -1, -1):
        carry = chunk(qi * (tq // tk) + dgl * nsub, (key_off + dgl * big) < row, carry)

    def body(n, carry):
        return chunk((qi * ratio - 1 - n) * nsub, None, carry)

    lax.fori_loop(0, qi * ratio, body, carry)
    o_ref[0] = acc_ref[...]


def _attn_prompt(q, k, v, sb_bias):
    b, t, _ = q.shape
    tq, tk, nsub = ATTN_TQ, ATTN_TK, ATTN_NSUB
    j = lax.broadcasted_iota(i32, (2 * tk, 2 * tk), 0)
    s = lax.broadcasted_iota(i32, (2 * tk, 2 * tk), 1)
    u2 = ((j >= s) & ((j // tk) == (s // tk))).astype(bf16)
    seq = pl.BlockSpec((1, t, 2 * HEAD_DIM), lambda bi, hp, qi: (bi, 0, hp))
    return pl.pallas_call(
        functools.partial(_attn_prompt_kernel, tq=tq, tk=tk, nsub=nsub),
        grid=(b, H_A // 2, t // tq),
        in_specs=[pl.BlockSpec(memory_space=pltpu.SMEM),
                  pl.BlockSpec((1, tq, 2 * HEAD_DIM), lambda bi, hp, qi: (bi, qi, hp)),
                  seq, seq,
                  pl.BlockSpec((2 * tk, 2 * tk), lambda bi, hp, qi: (0, 0))],
        out_specs=pl.BlockSpec((1, tq, 2 * HEAD_DIM), lambda bi, hp, qi: (bi, qi, hp)),
        out_shape=SDS((b, t, W_A), f32),
        scratch_shapes=[pltpu.VMEM((t // tk, 2 * tk, 2 * HEAD_DIM), bf16),
                        pltpu.VMEM((t // tk, 2 * tk, 2 * HEAD_DIM), bf16),
                        pltpu.VMEM((tq, 2 * HEAD_DIM), f32)],
        compiler_params=_cparams(("arbitrary", "arbitrary", "arbitrary")),
        name="attn_prompt",
    )(sb_bias, q, k, v, u2)


def _attn_sample_kernel(pt_ref, bias_ref, qrow_ref, knew_ref, vnew_ref, uo_ref, *rest, n_new, pages):
    kp_refs = rest[:pages]
    vp_refs = rest[pages:2 * pages]
    o_ref, acc_ref, carry_ref = rest[2 * pages:]
    step = pl.program_id(1)
    n_row = H_A * n_new
    row_head = lax.broadcasted_iota(i32, (n_row, PAGE_SIZE), 0) // n_new
    bias = jnp.broadcast_to(bias_ref[...], (n_row, PAGE_SIZE))

    def add_values(w, value_of_head, dims):
        for h in range(H_A):
            acc_ref[...] += _dg(jnp.where(row_head[:, :w.shape[1]] == h, w, 0.0).astype(bf16),
                                value_of_head(h), dims)

    @pl.when(step == 0)
    def _():
        pad = knew_ref.shape[1]
        knew = knew_ref[0].astype(bf16)
        vnew = vnew_ref[0].astype(bf16)
        head_cols = lambda x, h: x[:, h * HEAD_DIM:(h + 1) * HEAD_DIM]
        z = sum(_dg(qrow_ref[0, h], head_cols(knew, h), _NT) for h in range(H_A)) + bias[:, :pad]
        s_idx = lax.broadcasted_iota(i32, (n_row, pad), 1)
        t_idx = lax.broadcasted_iota(i32, (n_row, pad), 0) % n_new
        mask = s_idx < t_idx
        sp = jnp.where(mask, _softplus2(z), 0.0)
        incl = jnp.zeros_like(sp)
        for j in range(n_new):
            incl = incl + jnp.where(s_idx <= j, sp[:, j:j + 1], 0.0)
        w = jnp.where(mask, jnp.exp2(z - incl), 0.0)
        acc_ref[...] = jnp.zeros_like(acc_ref)
        add_values(w, lambda h: head_cols(vnew, h), _NN)
        carry_ref[...] = jnp.broadcast_to(incl[:, 0:1], carry_ref.shape)

    uo = uo_ref[...]
    head_t = lambda ref, h: ref[0, 0, h].astype(bf16)
    z = [sum(_dg(qrow_ref[0, h], head_t(kp_refs[i], h), _NN) for h in range(H_A)) + bias
         for i in range(pages)]
    sp = [_softplus2(z[i]) for i in range(pages)]
    cr = [jnp.dot(sp[i].astype(bf16), uo, preferred_element_type=f32) for i in range(pages)]
    carry = carry_ref[...]
    for i in range(pages):
        w = jnp.exp2(z[i] - cr[i][:, :PAGE_SIZE] - carry)
        add_values(w, lambda h, i=i: head_t(vp_refs[i], h), _NT)
        carry = carry + cr[i][:, PAGE_SIZE:]
    carry_ref[...] = carry

    @pl.when(step == pl.num_programs(1) - 1)
    def _():
        o_ref[0] = acc_ref[...]


def _attn_sample(q, k_new, v_new, cache_k, cache_v, layer, page_table, sb_bias, n_new):
    b, pad, _ = q.shape
    n_pages = page_table.shape[1]
    pages = PAGES_PER_STEP
    n_row = H_A * n_new
    scale = LOG2E * HEAD_DIM ** -0.5
    qh = jnp.transpose(q[:, :n_new].reshape(b, n_new, H_A, HEAD_DIM) * scale, (0, 2, 1, 3))
    eye = jnp.eye(H_A, dtype=f32)
    qrow = (qh[:, :, None, :, :] * eye[None, :, :, None, None]).reshape(b, H_A, n_row, HEAD_DIM).astype(bf16)
    bias = jnp.repeat(sb_bias * LOG2E, n_new).reshape(n_row, 1)
    j = lax.broadcasted_iota(i32, (PAGE_SIZE, PAGE_SIZE), 0)
    s = lax.broadcasted_iota(i32, (PAGE_SIZE, PAGE_SIZE), 1)
    uo = jnp.concatenate([(j >= s).astype(bf16), jnp.ones((PAGE_SIZE, PAGE_SIZE), bf16)], axis=1)
    cache_k = jnp.transpose(cache_k, (0, 1, 3, 4, 2))
    cache_v = jnp.transpose(cache_v, (0, 1, 3, 4, 2))

    def page_spec(i):
        return pl.BlockSpec(
            (1, 1, H_A, HEAD_DIM, PAGE_SIZE),
            lambda bi, st, pt, i=i: (layer, pt[bi, n_pages - 1 - (st * pages + i)], 0, 0, 0))

    new_spec = pl.BlockSpec((1, pad, W_A), lambda bi, st, pt: (bi, 0, 0))
    grid_spec = pltpu.PrefetchScalarGridSpec(
        num_scalar_prefetch=1,
        grid=(b, n_pages // pages),
        in_specs=[pl.BlockSpec((n_row, 1), lambda bi, st, pt: (0, 0)),
                  pl.BlockSpec((1, H_A, n_row, HEAD_DIM), lambda bi, st, pt: (bi, 0, 0, 0)),
                  new_spec, new_spec,
                  pl.BlockSpec((PAGE_SIZE, 2 * PAGE_SIZE), lambda bi, st, pt: (0, 0))]
                 + [page_spec(i) for i in range(pages)] * 2,
        out_specs=pl.BlockSpec((1, n_row, HEAD_DIM), lambda bi, st, pt: (bi, 0, 0)),
        scratch_shapes=[pltpu.VMEM((n_row, HEAD_DIM), f32), pltpu.VMEM((n_row, PAGE_SIZE), f32)],
    )
    out = pl.pallas_call(
        functools.partial(_attn_sample_kernel, n_new=n_new, pages=pages),
        grid_spec=grid_spec,
        out_shape=SDS((b, n_row, HEAD_DIM), f32),
        compiler_params=_cparams(("arbitrary", "arbitrary")),
        name="attn_sample",
    )(page_table, bias, qrow, k_new, v_new, uo, *([cache_k] * pages), *([cache_v] * pages))
    return jnp.transpose(out.reshape(b, H_A, n_new, HEAD_DIM), (0, 2, 1, 3)).reshape(b, n_new, W_A)


def _prep_kernel(pb_ref, pf_ref, mu_ref, dbase_ref, dup_ref, ibase_ref, iup_ref, gup_ref,
                 kk_ref, ka_ref, rk_ref, hsum_ref,
                 r_out, lw_out, k_out, v_out, a_out, b_out, g_out, bonus_out, *, t_real):
    nb, tt, w = pb_ref.shape
    pb = pb_ref[...]
    tpos = lax.broadcasted_iota(i32, pb.shape, 1)
    prev = jnp.where(tpos == 0, pf_ref[:, 0], pltpu.roll(pb, 1, axis=1))
    x = (pb + (prev - pb) * mu_ref[...]).reshape(nb * tt, w)
    r = x[:, 0:W_B]
    k = x[:, W_B:2 * W_B]
    v = x[:, 2 * W_B:3 * W_B]
    o = 3 * W_B
    xw = x[:, o:o + DECAY_LORA]
    xa = x[:, o + DECAY_LORA:o + DECAY_LORA + ICLR_LORA]
    xg = x[:, o + DECAY_LORA + ICLR_LORA:]
    log_w = -_softplus(-(dbase_ref[...] + _bdot(jnp.tanh(xw), dup_ref[...]))) - 0.5
    lw = -jnp.exp(log_w)
    a = _sigmoid(ibase_ref[...] + _bdot(xa, iup_ref[...]))
    g = _bdot(_sigmoid(xg), gup_ref[...])
    hsum = hsum_ref[...]
    kkf = k * kk_ref[...]
    kk = kkf / jnp.maximum(jnp.sqrt(_split_dot(kkf * kkf, hsum)), 1e-12)
    k2 = k * (1.0 + (a - 1.0) * ka_ref[...])
    bonus = _split_dot(r * k2 * rk_ref[...], hsum) * v
    na = -kk
    kb = kk * a
    if t_real < tt:
        valid = (lax.broadcasted_iota(i32, (nb, tt, W_B), 1) < t_real).reshape(nb * tt, W_B)
        zero = lambda u: jnp.where(valid, u, 0.0)
        r, lw, k2, v, na, kb = zero(r), zero(lw), zero(k2), zero(v), zero(na), zero(kb)
    g_out[...] = g.reshape(nb, tt, W_B)
    bonus_out[...] = bonus.reshape(nb, tt, W_B)
    for val, ref in ((r, r_out), (lw, lw_out), (k2, k_out), (v, v_out), (na, a_out), (kb, b_out)):
        val = val.reshape(nb, tt, W_B)
        for h in range(H_B):
            ref[:, h, :, :] = val[:, :, h * HEAD_DIM:(h + 1) * HEAD_DIM]


def _rwkv_prep(pb, prev_first, p, nb, tt, t_real):
    b, t, w = pb.shape
    row = lambda i, j: (i, j, 0)
    const = lambda i, j: (0, 0)
    vec = lambda a: a.reshape(1, -1)
    head_of = jnp.arange(W_B) // HEAD_DIM
    hsum = (head_of[:, None] == head_of[None, :]).astype(bf16)
    params = [vec(p['mu_shift']), vec(p['decay_base']), p['decay_up'], vec(p['iclr_base']), p['iclr_up'],
              p['gate_up'], vec(p['k_k']), vec(p['k_a']), vec(p['r_k']), hsum]
    heads = pl.BlockSpec((nb, H_B, tt, HEAD_DIM), lambda i, j: (i, 0, j, 0))
    return pl.pallas_call(
        functools.partial(_prep_kernel, t_real=t_real),
        grid=(b // nb, t // tt),
        in_specs=[pl.BlockSpec((nb, tt, w), row),
                  pl.BlockSpec((nb, 1, 1, w), lambda i, j: (i, j, 0, 0))]
                 + [pl.BlockSpec(a.shape, const) for a in params],
        out_specs=[heads] * 6 + [pl.BlockSpec((nb, tt, W_B), row)] * 2,
        out_shape=[SDS((b, H_B, t, HEAD_DIM), f32)] * 6 + [SDS((b, t, W_B), f32)] * 2,
        compiler_params=_cparams(("arbitrary", "arbitrary")),
        name="rwkv_prep",
    )(pb, prev_first, *params)


def _wkv_kernel(r_ref, lw_ref, k_ref, v_ref, a_ref, b_ref, s0_ref, y_ref, s_out, st_ref, *, chunk):
    c = chunk
    n = HEAD_DIM
    step = pl.program_id(1)

    n_chain = lw_ref.shape[0] * H_B
    of = lambda ref, i: ref[i // H_B, i % H_B]

    @pl.when(step == 0)
    def _():
        for i in range(n_chain):
            st_ref[i] = of(s0_ref, i).T

    ti = lax.broadcasted_iota(i32, (c, c), 0)
    si = lax.broadcasted_iota(i32, (c, c), 1)
    tri = (ti >= si).astype(bf16)
    eye = (ti == si).astype(f32)
    ones = jnp.ones((c, n), bf16)
    row2 = lax.broadcasted_iota(i32, (c, 2 * c), 0)
    col2 = lax.broadcasted_iota(i32, (c, 2 * c), 1)
    strict2 = (col2 % c) < row2
    incl2 = (col2 % c) <= row2
    right = col2 >= c
    levels = max(c.bit_length() - 2, 0)
    heads = range(n_chain)
    each = lambda f: [f(h) for h in heads]
    lw = each(lambda h: of(lw_ref, h))
    lw3 = each(lambda h: _split3(lw[h]))
    cum = each(lambda h: sum(_dg(tri, t, _NN) for t in lw3[h]))
    wsum = each(lambda h: sum(_dg(t, ones, _TN) for t in lw3[h]))
    w_in = each(lambda h: jnp.exp(cum[h]))
    w_out = each(lambda h: jnp.exp(-cum[h]))
    at = each(lambda h: of(a_ref, h) * jnp.exp(cum[h] - lw[h]))
    rt = each(lambda h: of(r_ref, h) * w_in[h])
    bk = each(lambda h: jnp.concatenate([of(b_ref, h) * w_out[h], of(k_ref, h) * w_out[h]], axis=0))
    g = each(lambda h: _dot3(jnp.concatenate([at[h], rt[h]], axis=0), bk[h], _NT))
    top = each(lambda h: jnp.where(strict2, g[h][:c], 0.0))
    bot = each(lambda h: jnp.where(incl2, g[h][c:], 0.0))
    pw = each(lambda h: top[h][:, :c])
    inv = each(lambda h: eye + pw[h])
    for _ in range(levels):
        pw = each(lambda h: _dot3(pw[h], pw[h]))
        inv = each(lambda h: inv[h] + _dot3(inv[h], pw[h]))
    vv = each(lambda h: jnp.concatenate([of(v_ref, h), of(v_ref, h)], axis=0))
    xv = each(lambda h: _dot3(jnp.where(right, top[h], 0.0), vv[h]))
    x = each(lambda h: _dot3(at[h], st_ref[h]) + xv[h])
    u = each(lambda h: _dot3(inv[h], x[h]))
    uv = each(lambda h: jnp.concatenate([u[h], of(v_ref, h)], axis=0))
    y = each(lambda h: _bdot(bot[h], uv[h]) + _bdot(rt[h], st_ref[h]))
    st_new = each(lambda h: (st_ref[h] + _dot3(bk[h], uv[h], _TN)) * jnp.exp(wsum[h]))
    for h in heads:
        st_ref[h] = st_new[h]
        mu = jnp.mean(y[h], axis=-1, keepdims=True)
        yc = y[h] - mu
        var = jnp.mean(yc * yc, axis=-1, keepdims=True)
        col0 = (h % H_B) * HEAD_DIM
        y_ref[h // H_B, :, col0:col0 + HEAD_DIM] = yc * lax.rsqrt(var + LNX_EPS)

    @pl.when(step == pl.num_programs(1) - 1)
    def _():
        for i in range(n_chain):
            s_out[i // H_B, i % H_B] = st_ref[i].T


def _wkv_scan(r, lw, k, v, a, b, s0, chunk, nseq):
    bsz, h, t, n = r.shape
    seq = pl.BlockSpec((nseq, h, chunk, n), lambda i, j: (i, 0, j, 0))
    state = pl.BlockSpec((nseq, h, n, n), lambda i, j: (i, 0, 0, 0))
    return pl.pallas_call(
        functools.partial(_wkv_kernel, chunk=chunk),
        grid=(bsz // nseq, t // chunk),
        in_specs=[seq] * 6 + [state],
        out_specs=[pl.BlockSpec((nseq, chunk, h * n), lambda i, j: (i, j, 0)), state],
        out_shape=[SDS((bsz, t, h * n), f32), SDS((bsz, h, n, n), f32)],
        scratch_shapes=[pltpu.VMEM((nseq * h, n, n), f32)],
        compiler_params=_cparams(("arbitrary", "arbitrary")),
        name="wkv_scan",
    )(r, lw, k, v, a, b, s0)


def _outproj_kernel(attn_ref, yn_ref, bonus_ref, g_ref, x_ref, gt1_ref, sh2_ref, sc2_ref,
                    gattn_ref, lnxg_ref, lnxb_ref, wout_ref, gpost_ref, gpre_ref, rw_ref, rb_ref,
                    x1_ref, h2_ref, logit_ref):
    nb, tt, d = x_ref.shape
    o_a = _rms(attn_ref[...], gattn_ref[...])
    o_b = (yn_ref[...] * lnxg_ref[...] + lnxb_ref[...] + bonus_ref[...]) * g_ref[...]
    cat = jnp.concatenate([o_a, o_b], axis=-1).reshape(nb * tt, d)
    catb = cat.astype(bf16)
    half = d // 2
    mixed = jnp.concatenate([jnp.dot(catb, wout_ref[:, :half], preferred_element_type=f32),
                             jnp.dot(catb, wout_ref[:, half:], preferred_element_type=f32)],
                            axis=1).reshape(nb, tt, d)
    x1 = x_ref[...] + gt1_ref[...] * _rms(mixed, gpost_ref[...])
    x1_ref[...] = x1
    h2 = _rms(x1, gpre_ref[...]) * (1.0 + sc2_ref[...]) + sh2_ref[...]
    _store_token_tiles(h2_ref, 0, h2.reshape(nb * tt, d))
    logits = _dot3(h2.reshape(nb * tt, d), rw_ref[...]) + rb_ref[...]
    logit_ref[...] = logits.reshape(nb, tt, N_EXPERTS)


def _out_proj(attn, yn, bonus, g, x, gt1, sh2, sc2, p, w_out_bf16, nb, tt):
    b, t, d = x.shape
    row = lambda w: pl.BlockSpec((nb, tt, w), lambda i, j: (i, j, 0))
    mod = pl.BlockSpec((nb, 1, d), lambda i, j: (i, 0, 0))
    vec = lambda a: a.reshape(1, -1)
    params = [vec(p['g_attn_out']), vec(p['lnx_g']), vec(p['lnx_b']), w_out_bf16,
              vec(p['g_mix_post']), vec(p['g_ffn_pre']), p['router_w'], vec(p['router_b'])]
    return pl.pallas_call(
        _outproj_kernel,
        grid=(b // nb, t // tt),
        in_specs=[row(W_A), row(W_B), row(W_B), row(W_B), row(d), mod, mod, mod]
                 + [pl.BlockSpec(a.shape, lambda i, j: (0, 0)) for a in params],
        out_specs=[row(d), pl.BlockSpec((nb, tt, V7X_SUBLANES, V7X_LANES), lambda i, j: (i, j, 0, 0)),
                   row(N_EXPERTS)],
        out_shape=[SDS((b, t, d), f32), SDS((b, t, V7X_SUBLANES, V7X_LANES), f32),
                   SDS((b, t, N_EXPERTS), f32)],
        compiler_params=_cparams(("arbitrary", "arbitrary")),
        name="out_proj",
    )(attn, yn, bonus, g, x, gt1, sh2, sc2, *params)


def _route_kernel(logit_ref, idx_ref, gate_ref, rank_ref, count_ref, base_ref):
    tile = logit_ref.shape[0]

    @pl.when(pl.program_id(0) == 0)
    def _():
        base_ref[...] = jnp.zeros_like(base_ref)

    lane = lax.broadcasted_iota(i32, (tile, N_EXPERTS), 1).astype(f32)
    cur = logit_ref[...]
    hots, vals, idxs = [], [], []
    for _ in range(TOP_K):
        top = jnp.max(cur, axis=-1, keepdims=True)
        idx = jnp.min(jnp.where(cur == top, lane, float(N_EXPERTS)), axis=-1, keepdims=True)
        hot = lane == idx
        hots.append(hot)
        vals.append(top)
        idxs.append(idx)
        cur = jnp.where(hot, -jnp.inf, cur)
    es = [jnp.exp(v - vals[0]) for v in vals]
    total = sum(es)
    chosen = sum(h.astype(f32) for h in hots)
    ti = lax.broadcasted_iota(i32, (tile, tile), 0)
    si = lax.broadcasted_iota(i32, (tile, tile), 1)
    earlier = jnp.dot((ti > si).astype(bf16), chosen.astype(bf16), preferred_element_type=f32) + base_ref[...]
    ranks = [jnp.sum(jnp.where(h, earlier, 0.0), axis=-1, keepdims=True) for h in hots]
    base_ref[...] += jnp.sum(chosen, axis=0, keepdims=True)
    count_ref[...] = base_ref[...]
    col = lax.broadcasted_iota(i32, (tile, TOP_K), 1)
    pick = lambda parts: sum(jnp.where(col == k, parts[k], 0.0) for k in range(TOP_K))
    idx_ref[...] = pick(idxs).astype(i32)
    gate_ref[...] = pick([e / total for e in es])
    rank_ref[...] = pick(ranks).astype(i32)


def _route(logits, tile):
    m = logits.shape[0]
    tok = lambda dt: SDS((m, TOP_K), dt)
    blk = pl.BlockSpec((tile, TOP_K), lambda i: (i, 0))
    return pl.pallas_call(
        _route_kernel,
        grid=(m // tile,),
        in_specs=[pl.BlockSpec((tile, N_EXPERTS), lambda i: (i, 0))],
        out_specs=[blk, blk, blk, pl.BlockSpec((1, N_EXPERTS), lambda i: (0, 0))],
        out_shape=[tok(i32), tok(f32), tok(i32), SDS((1, N_EXPERTS), f32)],
        scratch_shapes=[pltpu.VMEM((1, N_EXPERTS), f32)],
        compiler_params=_cparams(("arbitrary",)),
        name="moe_route",
    )(logits)


def _row_copy(src, src_row, dst, dst_row, sem):
    return pltpu.make_async_copy(src.at[pl.ds(src_row, 1)], dst.at[pl.ds(dst_row, 1)], sem)


def _dispatch_kernel(dest_ref, xa_ref, xb_ref, xs_in, xs_out, sem, *, n_first):
    del xs_in
    tile = xa_ref.shape[0]

    def scatter_rows(x_ref):
        def start(r, _):
            for k in range(TOP_K):
                _row_copy(x_ref, r, xs_out, dest_ref[k, r], sem).start(priority=k % 2)
            return 0

        lax.fori_loop(0, tile, start, 0, unroll=DMA_ISSUE_UNROLL)
        for k in range(TOP_K):
            pltpu.make_async_copy(x_ref, xs_out.at[pl.ds(0, tile)], sem).wait()

    @pl.when(pl.program_id(0) < n_first)
    def _():
        scatter_rows(xa_ref)

    @pl.when(pl.program_id(0) >= n_first)
    def _():
        scatter_rows(xb_ref)


def _dispatch(xa, xb, dest_t, n_slots, tile):
    row_tile = xa.shape[1:]
    n_first = xa.shape[0] // tile
    return pl.pallas_call(
        functools.partial(_dispatch_kernel, n_first=n_first),
        grid=(n_first + xb.shape[0] // tile,),
        in_specs=[pl.BlockSpec((TOP_K, tile), lambda i: (0, i), memory_space=pltpu.SMEM),
                  pl.BlockSpec((tile,) + row_tile, lambda i: (jnp.minimum(i, n_first - 1), 0, 0)),
                  pl.BlockSpec((tile,) + row_tile, lambda i: (jnp.maximum(i - n_first, 0), 0, 0)),
                  pl.BlockSpec(memory_space=pl.ANY)],
        out_specs=pl.BlockSpec(memory_space=pl.ANY),
        out_shape=SDS((n_slots,) + row_tile, xa.dtype),
        scratch_shapes=[pltpu.SemaphoreType.DMA(())],
        input_output_aliases={3: 0},
        compiler_params=_cparams(("arbitrary",)),
        name="moe_dispatch",
    )(dest_t, xa, xb, jnp.zeros((n_slots,) + row_tile, xa.dtype))


def _combine_kernel(dest_ref, dest_next_ref, gate_ref, ys_hbm, x1a_ref, gt2a_ref, x1b_ref, gt2b_ref, g_ref,
                    oa_ref, ob_ref, buf_ref, sems, *, n_first):
    i = pl.program_id(0)
    n = pl.num_programs(0)
    tile = oa_ref.shape[0]

    def issue(dref, slot):
        def start(r, _):
            for k in range(TOP_K):
                _row_copy(ys_hbm, dref[k, r], buf_ref.at[slot, k], r, sems.at[slot]).start(priority=k % 2)
            return 0

        lax.fori_loop(0, tile, start, 0, unroll=DMA_ISSUE_UNROLL)

    def finish(slot):
        @pl.when(i + 1 < n)
        def _():
            issue(dest_next_ref, 1 - slot)

        for k in range(TOP_K):
            pltpu.make_async_copy(ys_hbm.at[pl.ds(0, tile)], buf_ref.at[slot, k], sems.at[slot]).wait()
        gate = gate_ref[...]
        f = sum(gate[:, k:k + 1] * _load_token_tiles(buf_ref, (slot * TOP_K + k) * tile, tile)
                for k in range(TOP_K))
        nf = _rms(f, g_ref[...])

        @pl.when(i < n_first)
        def _():
            oa_ref[...] = x1a_ref[...] + gt2a_ref[0] * nf

        @pl.when(i >= n_first)
        def _():
            ob_ref[...] = x1b_ref[...] + gt2b_ref[...] * nf

    @pl.when(i == 0)
    def _():
        issue(dest_ref, 0)

    @pl.when(i % 2 == 0)
    def _():
        finish(0)

    @pl.when(i % 2 == 1)
    def _():
        finish(1)


def _combine(ys, dest_t, gate, x1_a, gt2_a, x1_b, gt2_b, g_post, tile):
    m = gate.shape[0]
    m_a, d = x1_a.shape
    n_first = m_a // tile
    n_tiles = m // tile
    tiles_per_seq = m_a // gt2_a.shape[0] // tile
    first = lambda i: jnp.minimum(i, n_first - 1)
    second = lambda i: jnp.maximum(i - n_first, 0)
    return pl.pallas_call(
        functools.partial(_combine_kernel, n_first=n_first),
        grid=(n_tiles,),
        in_specs=[pl.BlockSpec((TOP_K, tile), lambda i: (0, i), memory_space=pltpu.SMEM),
                  pl.BlockSpec((TOP_K, tile), lambda i: (0, jnp.minimum(i + 1, n_tiles - 1)),
                               memory_space=pltpu.SMEM),
                  pl.BlockSpec((tile, TOP_K), lambda i: (i, 0)),
                  pl.BlockSpec(memory_space=pl.ANY),
                  pl.BlockSpec((tile, d), lambda i: (first(i), 0)),
                  pl.BlockSpec((1, 1, d), lambda i: (first(i) // tiles_per_seq, 0, 0)),
                  pl.BlockSpec((tile, d), lambda i: (second(i), 0)),
                  pl.BlockSpec((tile, d), lambda i: (second(i), 0)),
                  pl.BlockSpec((1, d), lambda i: (0, 0))],
        out_specs=[pl.BlockSpec((tile, d), lambda i: (first(i), 0)),
                   pl.BlockSpec((tile, d), lambda i: (second(i), 0))],
        out_shape=[SDS((m_a, d), f32), SDS((m - m_a, d), f32)],
        scratch_shapes=[pltpu.VMEM((2, TOP_K, tile) + ys.shape[1:], f32), pltpu.SemaphoreType.DMA((2,))],
        compiler_params=_cparams(("arbitrary",)),
        name="moe_combine",
    )(dest_t, dest_t, gate, ys, x1_a, gt2_a, x1_b, gt2_b, g_post.reshape(1, d))


def _expert_kernel(be_ref, nused_ref, x_ref, w1_ref, b1_ref, w2_ref, b2_ref, y_ref, w1b_ref, w2b_ref):
    i = pl.program_id(0)
    prev = be_ref[jnp.maximum(i - 1, 0)]

    @pl.when((i == 0) | (be_ref[i] != prev))
    def _():
        w1b_ref[...] = w1_ref[0].astype(bf16)
        w2b_ref[...] = w2_ref[0].astype(bf16)

    @pl.when(i < nused_ref[0])
    def _():
        x = _load_token_tiles(x_ref, 0, x_ref.shape[0]).astype(bf16)
        u = jnp.dot(x, w1b_ref[...], preferred_element_type=f32) + b1_ref[0]
        u_glu = jnp.minimum(u[:, :D_FF], SWIGLU_LIMIT)
        u_lin = jnp.clip(u[:, D_FF:], -SWIGLU_LIMIT, SWIGLU_LIMIT)
        act = u_glu * _sigmoid(SWIGLU_ALPHA * u_glu) * (u_lin + 1.0)
        y = jnp.dot(act.astype(bf16), w2b_ref[...], preferred_element_type=f32) + b2_ref[0]
        _store_token_tiles(y_ref, 0, y)

    @pl.when(i >= nused_ref[0])
    def _():
        y_ref[...] = jnp.zeros_like(y_ref)


def _expert_ffn(xs, blk_expert, n_used, e_w1, e_b1, e_w2, e_b2, tm):
    n_slots = xs.shape[0]
    row_tile = xs.shape[1:]
    d = math.prod(row_tile)
    n_blocks = n_slots // tm
    rows = pl.BlockSpec((tm,) + row_tile, lambda i, be, nu: (i, 0, 0))
    ex = lambda i, be, nu: (be[i], 0, 0)
    grid_spec = pltpu.PrefetchScalarGridSpec(
        num_scalar_prefetch=2,
        grid=(n_blocks,),
        in_specs=[rows,
                  pl.BlockSpec((1, d, 2 * D_FF), ex),
                  pl.BlockSpec((1, 1, 2 * D_FF), ex),
                  pl.BlockSpec((1, D_FF, d), ex),
                  pl.BlockSpec((1, 1, d), ex)],
        out_specs=rows,
        scratch_shapes=[pltpu.VMEM((d, 2 * D_FF), bf16), pltpu.VMEM((D_FF, d), bf16)],
    )
    return pl.pallas_call(
        _expert_kernel,
        grid_spec=grid_spec,
        out_shape=SDS((n_slots,) + row_tile, f32),
        compiler_params=_cparams(("arbitrary",)),
        name="expert_ffn",
    )(blk_expert, n_used, xs, e_w1, e_b1.reshape(N_EXPERTS, 1, -1), e_w2, e_b2.reshape(N_EXPERTS, 1, -1))


def _moe_residual(h2_a, logits_a, x1_a, gt2_a, h2_b, logits_b, x1_b, gt2_b, g_post, e_w1, e_b1, e_w2, e_b2, tm):
    tile = MOE_TOKEN_TILE
    m_a, m_b = h2_a.shape[0], h2_b.shape[0]
    assert m_a % tile == 0
    extra = -m_b % tile
    h2_b = jnp.pad(h2_b, ((0, extra), (0, 0), (0, 0)))
    x1_b = jnp.pad(x1_b, ((0, extra), (0, 0)))
    gt2_b = jnp.pad(gt2_b, ((0, extra), (0, 0)))
    logits = jnp.concatenate([logits_a, logits_b, jnp.zeros((extra, N_EXPERTS), f32)], axis=0)
    m = m_a + m_b + extra
    top_idx, gate, rank, counts = _route(logits, tile)
    counts = counts.reshape(N_EXPERTS).astype(i32)
    padded = (counts + tm - 1) // tm * tm
    pad_end = jnp.cumsum(padded)
    pad_start = pad_end - padded
    n_blocks = (m * TOP_K + N_EXPERTS * (tm - 1) + tm - 1) // tm
    blk_row = (jnp.arange(n_blocks) * tm)[:, None]
    blk_expert = jnp.minimum(jnp.sum(pad_end[None, :] <= blk_row, axis=1), N_EXPERTS - 1).astype(i32)
    n_used = (pad_end[-1] // tm).astype(i32).reshape(1)
    group_start = jnp.sum(jnp.where(top_idx[:, :, None] == jnp.arange(N_EXPERTS), pad_start, 0), axis=-1)
    dest_t = (group_start + rank).astype(i32).T
    xs = _dispatch(h2_a, h2_b, dest_t, n_blocks * tm, tile)
    ys = _expert_ffn(xs, blk_expert, n_used, e_w1, e_b1, e_w2, e_b2, tm)
    out_a, out_b = _combine(ys, dest_t, gate, x1_a, gt2_a, x1_b, gt2_b, g_post, tile)
    return out_a, out_b[:m_b]


def _mix_layer(x, mod, attend, shift_prev, wkv_prev, p, w_in_bf16, w_out_bf16, nb, tt, t_real, chunk, wkv_nseq):
    b, t, d = x.shape
    sh1, sc1, gt1, sh2, sc2, gt2 = [mod[:, i:i + 1, :] for i in range(6)]
    q, k, v, pb, last = _in_proj(x, sh1, sc1, p['g_mix_pre'], w_in_bf16, nb, tt, t_real)
    attn = attend(q, k, v)
    prev_first = jnp.concatenate([shift_prev[:, None, None, :], last[:, :-1]], axis=1)
    r, lw, k2, v2, a, kb, g, bonus = _rwkv_prep(pb, prev_first, p, nb, tt, t_real)
    yn, wkv_new = _wkv_scan(r, lw, k2, v2, a, kb, wkv_prev, chunk, wkv_nseq)
    x1, h2, logits = _out_proj(attn, yn, bonus, g, x, gt1, sh2, sc2, p, w_out_bf16, nb, tt)
    return x1, h2, logits, gt2, k, v, wkv_new, last[:, -1, 0]


def kernel(x_prompt, x_sample, cache_k, cache_v, state_wkv, state_shift, page_table, c_prompt, c_sample,
           w_ada, b_ada, g_mix_pre, g_mix_post, g_ffn_pre, g_ffn_post, w_in, mu_shift,
           decay_base, decay_up, iclr_base, iclr_up, gate_up, k_k, k_a, r_k, lnx_g, lnx_b,
           g_attn_out, sb_bias, w_out, router_w, router_b, e_w1, e_b1, e_w2, e_b2):
    weights = dict(w_ada=w_ada, b_ada=b_ada, g_mix_pre=g_mix_pre, g_mix_post=g_mix_post,
                   g_ffn_pre=g_ffn_pre, g_ffn_post=g_ffn_post, w_in=w_in, mu_shift=mu_shift,
                   decay_base=decay_base, decay_up=decay_up, iclr_base=iclr_base, iclr_up=iclr_up,
                   gate_up=gate_up, k_k=k_k, k_a=k_a, r_k=r_k, lnx_g=lnx_g, lnx_b=lnx_b,
                   g_attn_out=g_attn_out, sb_bias=sb_bias, w_out=w_out, router_w=router_w,
                   router_b=router_b, e_w1=e_w1, e_b1=e_b1, e_w2=e_w2, e_b2=e_b2)
    depth = w_ada.shape[0]
    bp, tp, d = x_prompt.shape
    bs, ts, _ = x_sample.shape
    ts_pad = -(-ts // V7X_SUBLANES) * V7X_SUBLANES
    hp = x_prompt
    hs = jnp.pad(x_sample, ((0, 0), (0, ts_pad - ts), (0, 0)))
    outs = [[] for _ in range(8)]
    for l in range(depth):
        p = {name: w[l] for name, w in weights.items()}
        w_in_bf16 = p['w_in'].astype(bf16)
        w_out_bf16 = p['w_out'].astype(bf16)
        mod = _ada_mod(jnp.concatenate([c_prompt, c_sample], axis=0), p['w_ada'], p['b_ada'])
        mod = mod.reshape(bp + bs, 6, d)

        attend_p = lambda q, k, v: _attn_prompt(q, k, v, p['sb_bias'])
        x1p, h2p, lgp, gt2p, kp, vp, wp, shp = _mix_layer(
            hp, mod[:bp], attend_p, jnp.zeros((bp, SHIFT_W), f32),
            jnp.zeros((bp, H_B, HEAD_DIM, HEAD_DIM), f32), p, w_in_bf16, w_out_bf16,
            nb=1, tt=ROW_TILE, t_real=tp, chunk=WKV_CHUNK, wkv_nseq=WKV_NSEQ)

        attend_s = lambda q, k, v: jnp.pad(
            _attn_sample(q, k, v, cache_k, cache_v, l, page_table, p['sb_bias'], ts),
            ((0, 0), (0, ts_pad - ts), (0, 0)))
        x1s, h2s, lgs, gt2s, ks, vs, ws, shs = _mix_layer(
            hs, mod[bp:], attend_s, state_shift[l], state_wkv[l], p, w_in_bf16, w_out_bf16,
            nb=bs, tt=ts_pad, t_real=ts, chunk=ts_pad, wkv_nseq=WKV_NSEQ)

        mp = bp * tp
        ms = bs * ts
        yp, ys = _moe_residual(
            h2p.reshape((mp,) + h2p.shape[2:]), lgp.reshape(mp, N_EXPERTS), x1p.reshape(mp, d), gt2p,
            h2s[:, :ts].reshape((ms,) + h2s.shape[2:]), lgs[:, :ts].reshape(ms, N_EXPERTS),
            x1s[:, :ts].reshape(ms, d), jnp.broadcast_to(gt2s, (bs, ts, d)).reshape(ms, d),
            p['g_ffn_post'], p['e_w1'], p['e_b1'], p['e_w2'], p['e_b2'], MOE_TILE)
        hp = yp.reshape(bp, tp, d)
        hs = jnp.pad(ys.reshape(bs, ts, d), ((0, 0), (0, ts_pad - ts), (0, 0)))

        for lst, val in zip(outs, (kp.reshape(bp, tp, H_A, HEAD_DIM), vp.reshape(bp, tp, H_A, HEAD_DIM),
                                   ks[:, :ts].reshape(bs, ts, H_A, HEAD_DIM),
                                   vs[:, :ts].reshape(bs, ts, H_A, HEAD_DIM), wp, ws, shp, shs)):
            lst.append(val)
    return (hp, hs[:, :ts]) + tuple(jnp.stack(lst) for lst in outs)
```

```python
import functools
import math

import jax
import jax.numpy as jnp
from jax import lax
from jax.experimental import pallas as pl
from jax.experimental.pallas import tpu as pltpu

f32 = jnp.float32
bf16 = jnp.bfloat16
i32 = jnp.int32
SDS = jax.ShapeDtypeStruct

D_MODEL = 1024
HEAD_DIM = 64
W_A = 512
W_B = 512
H_A = W_A // HEAD_DIM
H_B = W_B // HEAD_DIM
DECAY_LORA = 64
ICLR_LORA = 64
GATE_LORA = 128
SHIFT_W = 3 * W_B + DECAY_LORA + ICLR_LORA + GATE_LORA
N_EXPERTS = 32
TOP_K = 4
D_FF = D_MODEL
SWIGLU_ALPHA = 1.702
SWIGLU_LIMIT = 7.0
PAGE_SIZE = 128
RMS_EPS = 1e-6
LNX_EPS = 64e-5
LOG2E = 1.4426950408889634

V7X_SUBLANES = 8
V7X_LANES = 128
VMEM_LIMIT = 56 * 1024 * 1024

ROW_TILE = 256
WKV_CHUNK = 64
WKV_NSEQ = 4
ATTN_TQ = 512
ATTN_TK = 128
ATTN_NSUB = 4
PAGES_PER_STEP = 16
MOE_TILE = 256
MOE_TOKEN_TILE = 128
MOE_ROUTE_TILE = 256
DMA_ISSUE_UNROLL = 8


def _cparams(sem):
    return pltpu.CompilerParams(dimension_semantics=sem, vmem_limit_bytes=VMEM_LIMIT)


def _bdot(a, b):
    return jnp.dot(a.astype(bf16), b.astype(bf16), preferred_element_type=f32)


def _fdot(a, b, dims=(((1,), (0,)), ((), ()))):
    return lax.dot_general(a, b, dims, precision=lax.Precision.HIGHEST, preferred_element_type=f32)


_NT = (((1,), (1,)), ((), ()))
_TN = (((0,), (0,)), ((), ()))


def _split_dot(x, m_bf16):
    hi = x.astype(bf16)
    lo = (x - hi.astype(f32)).astype(bf16)
    return (jnp.dot(hi, m_bf16, preferred_element_type=f32)
            + jnp.dot(lo, m_bf16, preferred_element_type=f32))


def _sigmoid(x):
    return 1.0 / (1.0 + jnp.exp(-x))


def _softplus(x):
    return jnp.maximum(x, 0.0) + jnp.log(1.0 + jnp.exp(-jnp.abs(x)))


def _softplus2(z):
    return jnp.maximum(z, 0.0) + jnp.log2(1.0 + jnp.exp2(-jnp.abs(z)))


def _split2(x):
    hi = x.astype(bf16)
    return hi, (x - hi.astype(f32)).astype(bf16)


def _split3(x):
    hi = x.astype(bf16)
    r = x - hi.astype(f32)
    mid = r.astype(bf16)
    return hi, mid, (r - mid.astype(f32)).astype(bf16)


def _dg(a, b, dims):
    return lax.dot_general(a, b, dims, preferred_element_type=f32)


_NN = (((1,), (0,)), ((), ()))


def _dot3(a, b, dims=_NN):
    ah, al = _split2(a)
    bh, bl = _split2(b)
    return _dg(ah, bh, dims) + _dg(ah, bl, dims) + _dg(al, bh, dims)


def _dot_exact_rhs(x3, m_bf16, dims=_NN):
    return _dg(x3[0], m_bf16, dims) + _dg(x3[1], m_bf16, dims) + _dg(x3[2], m_bf16, dims)


def _rms(x, g):
    return x * lax.rsqrt(jnp.mean(x * x, axis=-1, keepdims=True) + RMS_EPS) * g


def _store_token_tiles(ref, first_row, x2d):
    rows = x2d.shape[0]
    flat = ref.reshape(math.prod(ref.shape) // V7X_LANES, V7X_LANES)
    for c in range(V7X_SUBLANES):
        flat[pl.ds(first_row * V7X_SUBLANES + c, rows, stride=V7X_SUBLANES), :] = (
            x2d[:, c * V7X_LANES:(c + 1) * V7X_LANES])


def _load_token_tiles(ref, first_row, rows):
    flat = ref.reshape(math.prod(ref.shape) // V7X_LANES, V7X_LANES)
    return jnp.concatenate([flat[pl.ds(first_row * V7X_SUBLANES + c, rows, stride=V7X_SUBLANES), :]
                            for c in range(V7X_SUBLANES)], axis=1)


def _ada_kernel(c_ref, w_ref, b_ref, o_ref):
    c = c_ref[...]
    o_ref[...] = _bdot(c * _sigmoid(c), w_ref[...]) + b_ref[...]


def _ada_mod(c, w_ada, b_ada):
    n, d = c.shape
    nout = w_ada.shape[1]
    tn = 1536
    return pl.pallas_call(
        _ada_kernel,
        grid=(nout // tn,),
        in_specs=[pl.BlockSpec((n, d), lambda j: (0, 0)),
                  pl.BlockSpec((d, tn), lambda j: (0, j)),
                  pl.BlockSpec((1, tn), lambda j: (0, j))],
        out_specs=pl.BlockSpec((n, tn), lambda j: (0, j)),
        out_shape=SDS((n, nout), f32),
        compiler_params=_cparams(("arbitrary",)),
        name="ada_mod",
    )(c, w_ada, b_ada.reshape(1, nout))


def _inproj_kernel(x_ref, sh_ref, sc_ref, g_ref, w_ref, q_ref, k_ref, v_ref, pb_ref, last_ref, *, t_last):
    nb, tt, d = x_ref.shape
    h = _rms(x_ref[...], g_ref[...]) * (1.0 + sc_ref[...]) + sh_ref[...]
    hb = h.reshape(nb * tt, d).astype(bf16)
    q_ref[...] = jnp.dot(hb, w_ref[:, 0:W_A], preferred_element_type=f32).reshape(nb, tt, W_A)
    k_ref[...] = jnp.dot(hb, w_ref[:, W_A:2 * W_A], preferred_element_type=f32).reshape(nb, tt, W_A)
    v_ref[...] = jnp.dot(hb, w_ref[:, 2 * W_A:3 * W_A], preferred_element_type=f32).reshape(nb, tt, W_A)
    pb = jnp.dot(hb, w_ref[:, 3 * W_A:], preferred_element_type=f32).reshape(nb, tt, SHIFT_W)
    pb_ref[...] = pb
    last_ref[:, 0] = pb[:, t_last:t_last + 1, :]


def _in_proj(x, sh1, sc1, g, w_in_bf16, nb, tt, t_real):
    b, t, d = x.shape
    row = lambda w: pl.BlockSpec((nb, tt, w), lambda i, j: (i, j, 0))
    mod = pl.BlockSpec((nb, 1, d), lambda i, j: (i, 0, 0))
    return pl.pallas_call(
        functools.partial(_inproj_kernel, t_last=min(tt, t_real) - 1),
        grid=(b // nb, t // tt),
        in_specs=[row(d), mod, mod,
                  pl.BlockSpec((1, d), lambda i, j: (0, 0)),
                  pl.BlockSpec(w_in_bf16.shape, lambda i, j: (0, 0))],
        out_specs=[row(W_A), row(W_A), row(W_A), row(SHIFT_W),
                   pl.BlockSpec((nb, 1, 1, SHIFT_W), lambda i, j: (i, j, 0, 0))],
        out_shape=[SDS((b, t, W_A), f32)] * 3 + [SDS((b, t, SHIFT_W), f32),
                                                  SDS((b, t // tt, 1, SHIFT_W), f32)],
        compiler_params=_cparams(("arbitrary", "arbitrary")),
        name="in_proj",
    )(x, sh1, sc1, g.reshape(1, d), w_in_bf16)


def _attn_prompt_kernel(bias_ref, q_ref, k_ref, v_ref, u2_ref, o_ref, kbd_ref, vbd_ref, acc_ref,
                        *, tq, tk, nsub):
    hp = pl.program_id(1)
    qi = pl.program_id(2)
    two = 2 * tk
    big = tk * nsub
    ratio = tq // big
    n_tiles = k_ref.shape[1] // tk

    @pl.when(qi == 0)
    def _():
        first = lax.broadcasted_iota(i32, (tk, 2 * HEAD_DIM), 1) < HEAD_DIM

        def build(j, _):
            start = pl.multiple_of(j * tk, tk)
            kt = k_ref[0, pl.ds(start, tk), :]
            vt = v_ref[0, pl.ds(start, tk), :]
            kbd_ref[j, 0:tk, :] = jnp.where(first, kt, 0.0).astype(bf16)
            kbd_ref[j, tk:two, :] = jnp.where(first, 0.0, kt).astype(bf16)
            vbd_ref[j, 0:tk, :] = jnp.where(first, vt, 0.0).astype(bf16)
            vbd_ref[j, tk:two, :] = jnp.where(first, 0.0, vt).astype(bf16)
            return 0

        lax.fori_loop(0, n_tiles, build, 0)

    qb = (q_ref[0] * (LOG2E * HEAD_DIM ** -0.5)).astype(bf16)
    lane2 = lax.broadcasted_iota(i32, (1, two), 1)
    bias2 = jnp.where(lane2 < tk, bias_ref[2 * hp], bias_ref[2 * hp + 1]) * LOG2E
    bias_row = jnp.concatenate([bias2] * nsub, axis=1)
    u2 = u2_ref[...]
    acc_ref[...] = jnp.zeros_like(acc_ref)
    row = lax.broadcasted_iota(i32, (tq, nsub * two), 0)
    col = lax.broadcasted_iota(i32, (tq, nsub * two), 1)
    key_off = (col // two) * tk + col % tk

    def chunk(tile0, mask, carry):
        kb = kbd_ref[pl.ds(tile0, nsub)].reshape(nsub * two, 2 * HEAD_DIM)
        vb = vbd_ref[pl.ds(tile0, nsub)].reshape(nsub * two, 2 * HEAD_DIM)
        z = _dg(qb, kb, _NT) + bias_row
        sp = _softplus2(z)
        if mask is not None:
            sp = jnp.where(mask, sp, 0.0)
        spb = sp.astype(bf16)
        ws = [None] * nsub
        for j in range(nsub - 1, -1, -1):
            incl = jnp.dot(spb[:, j * two:(j + 1) * two], u2, preferred_element_type=f32)
            ws[j] = jnp.exp2(z[:, j * two:(j + 1) * two] - incl - carry)
            carry = carry + jnp.concatenate([jnp.broadcast_to(incl[:, 0:1], (tq, tk)),
                                             jnp.broadcast_to(incl[:, tk:tk + 1], (tq, tk))], axis=1)
        w = jnp.concatenate(ws, axis=1)
        if mask is not None:
            w = jnp.where(mask, w, 0.0)
        acc_ref[...] += jnp.dot(w.astype(bf16), vb, preferred_element_type=f32)
        return carry

    carry = jnp.zeros((tq, two), f32)
    for dgl in range(ratio - 1, -1, -1):
        carry = chunk(qi * (tq // tk) + dgl * nsub, (key_off + dgl * big) < row, carry)

    def body(n, carry):
        return chunk((qi * ratio - 1 - n) * nsub, None, carry)

    lax.fori_loop(0, qi * ratio, body, carry)
    o_ref[0] = acc_ref[...]


def _attn_prompt(q, k, v, sb_bias):
    b, t, _ = q.shape
    tq, tk, nsub = ATTN_TQ, ATTN_TK, ATTN_NSUB
    j = lax.broadcasted_iota(i32, (2 * tk, 2 * tk), 0)
    s = lax.broadcasted_iota(i32, (2 * tk, 2 * tk), 1)
    u2 = ((j >= s) & ((j // tk) == (s // tk))).astype(bf16)
    seq = pl.BlockSpec((1, t, 2 * HEAD_DIM), lambda bi, hp, qi: (bi, 0, hp))
    return pl.pallas_call(
        functools.partial(_attn_prompt_kernel, tq=tq, tk=tk, nsub=nsub),
        grid=(b, H_A // 2, t // tq),
        in_specs=[pl.BlockSpec(memory_space=pltpu.SMEM),
                  pl.BlockSpec((1, tq, 2 * HEAD_DIM), lambda bi, hp, qi: (bi, qi, hp)),
                  seq, seq,
                  pl.BlockSpec((2 * tk, 2 * tk), lambda bi, hp, qi: (0, 0))],
        out_specs=pl.BlockSpec((1, tq, 2 * HEAD_DIM), lambda bi, hp, qi: (bi, qi, hp)),
        out_shape=SDS((b, t, W_A), f32),
        scratch_shapes=[pltpu.VMEM((t // tk, 2 * tk, 2 * HEAD_DIM), bf16),
                        pltpu.VMEM((t // tk, 2 * tk, 2 * HEAD_DIM), bf16),
                        pltpu.VMEM((tq, 2 * HEAD_DIM), f32)],
        compiler_params=_cparams(("arbitrary", "arbitrary", "arbitrary")),
        name="attn_prompt",
    )(sb_bias, q, k, v, u2)


def _attn_sample_kernel(pt_ref, bias_ref, qrow_ref, knew_ref, vnew_ref, uo_ref, *rest, n_new, pages):
    kp_refs = rest[:pages]
    vp_refs = rest[pages:2 * pages]
    o_ref, acc_ref, carry_ref = rest[2 * pages:]
    step = pl.program_id(1)
    n_row = H_A * n_new
    row_head = lax.broadcasted_iota(i32, (n_row, PAGE_SIZE), 0) // n_new
    bias = jnp.broadcast_to(bias_ref[...], (n_row, PAGE_SIZE))

    def add_values(w, value_of_head, dims):
        for h in range(H_A):
            acc_ref[...] += _dg(jnp.where(row_head[:, :w.shape[1]] == h, w, 0.0).astype(bf16),
                                value_of_head(h), dims)

    @pl.when(step == 0)
    def _():
        pad = knew_ref.shape[1]
        knew = knew_ref[0].astype(bf16)
        vnew = vnew_ref[0].astype(bf16)
        head_cols = lambda x, h: x[:, h * HEAD_DIM:(h + 1) * HEAD_DIM]
        z = sum(_dg(qrow_ref[0, h], head_cols(knew, h), _NT) for h in range(H_A)) + bias[:, :pad]
        s_idx = lax.broadcasted_iota(i32, (n_row, pad), 1)
        t_idx = lax.broadcasted_iota(i32, (n_row, pad), 0) % n_new
        mask = s_idx < t_idx
        sp = jnp.where(mask, _softplus2(z), 0.0)
        incl = jnp.zeros_like(sp)
        for j in range(n_new):
            incl = incl + jnp.where(s_idx <= j, sp[:, j:j + 1], 0.0)
        w = jnp.where(mask, jnp.exp2(z - incl), 0.0)
        acc_ref[...] = jnp.zeros_like(acc_ref)
        add_values(w, lambda h: head_cols(vnew, h), _NN)
        carry_ref[...] = jnp.broadcast_to(incl[:, 0:1], carry_ref.shape)

    uo = uo_ref[...]
    head_t = lambda ref, h: ref[0, 0, h].astype(bf16)
    z = [sum(_dg(qrow_ref[0, h], head_t(kp_refs[i], h), _NN) for h in range(H_A)) + bias
         for i in range(pages)]
    sp = [_softplus2(z[i]) for i in range(pages)]
    cr = [jnp.dot(sp[i].astype(bf16), uo, preferred_element_type=f32) for i in range(pages)]
    carry = carry_ref[...]
    for i in range(pages):
        w = jnp.exp2(z[i] - cr[i][:, :PAGE_SIZE] - carry)
        add_values(w, lambda h, i=i: head_t(vp_refs[i], h), _NT)
        carry = carry + cr[i][:, PAGE_SIZE:]
    carry_ref[...] = carry

    @pl.when(step == pl.num_programs(1) - 1)
    def _():
        o_ref[0] = acc_ref[...]


def _attn_sample(q, k_new, v_new, cache_k, cache_v, layer, page_table, sb_bias, n_new):
    b, pad, _ = q.shape
    n_pages = page_table.shape[1]
    pages = PAGES_PER_STEP
    n_row = H_A * n_new
    scale = LOG2E * HEAD_DIM ** -0.5
    qh = jnp.transpose(q[:, :n_new].reshape(b, n_new, H_A, HEAD_DIM) * scale, (0, 2, 1, 3))
    eye = jnp.eye(H_A, dtype=f32)
    qrow = (qh[:, :, None, :, :] * eye[None, :, :, None, None]).reshape(b, H_A, n_row, HEAD_DIM).astype(bf16)
    bias = jnp.repeat(sb_bias * LOG2E, n_new).reshape(n_row, 1)
    j = lax.broadcasted_iota(i32, (PAGE_SIZE, PAGE_SIZE), 0)
    s = lax.broadcasted_iota(i32, (PAGE_SIZE, PAGE_SIZE), 1)
    uo = jnp.concatenate([(j >= s).astype(bf16), jnp.ones((PAGE_SIZE, PAGE_SIZE), bf16)], axis=1)
    cache_k = jnp.transpose(cache_k, (0, 1, 3, 4, 2))
    cache_v = jnp.transpose(cache_v, (0, 1, 3, 4, 2))

    def page_spec(i):
        return pl.BlockSpec(
            (1, 1, H_A, HEAD_DIM, PAGE_SIZE),
            lambda bi, st, pt, i=i: (layer, pt[bi, n_pages - 1 - (st * pages + i)], 0, 0, 0))

    new_spec = pl.BlockSpec((1, pad, W_A), lambda bi, st, pt: (bi, 0, 0))
    grid_spec = pltpu.PrefetchScalarGridSpec(
        num_scalar_prefetch=1,
        grid=(b, n_pages // pages),
        in_specs=[pl.BlockSpec((n_row, 1), lambda bi, st, pt: (0, 0)),
                  pl.BlockSpec((1, H_A, n_row, HEAD_DIM), lambda bi, st, pt: (bi, 0, 0, 0)),
                  new_spec, new_spec,
                  pl.BlockSpec((PAGE_SIZE, 2 * PAGE_SIZE), lambda bi, st, pt: (0, 0))]
                 + [page_spec(i) for i in range(pages)] * 2,
        out_specs=pl.BlockSpec((1, n_row, HEAD_DIM), lambda bi, st, pt: (bi, 0, 0)),
        scratch_shapes=[pltpu.VMEM((n_row, HEAD_DIM), f32), pltpu.VMEM((n_row, PAGE_SIZE), f32)],
    )
    out = pl.pallas_call(
        functools.partial(_attn_sample_kernel, n_new=n_new, pages=pages),
        grid_spec=grid_spec,
        out_shape=SDS((b, n_row, HEAD_DIM), f32),
        compiler_params=_cparams(("arbitrary", "arbitrary")),
        name="attn_sample",
    )(page_table, bias, qrow, k_new, v_new, uo, *([cache_k] * pages), *([cache_v] * pages))
    return jnp.transpose(out.reshape(b, H_A, n_new, HEAD_DIM), (0, 2, 1, 3)).reshape(b, n_new, W_A)


def _prep_kernel(pb_ref, pf_ref, mu_ref, dbase_ref, dup_ref, ibase_ref, iup_ref, gup_ref,
                 kk_ref, ka_ref, rk_ref, hsum_ref,
                 r_out, lw_out, k_out, v_out, a_out, b_out, g_out, bonus_out, *, t_real):
    nb, tt, w = pb_ref.shape
    pb = pb_ref[...]
    tpos = lax.broadcasted_iota(i32, pb.shape, 1)
    prev = jnp.where(tpos == 0, pf_ref[:, 0], pltpu.roll(pb, 1, axis=1))
    x = (pb + (prev - pb) * mu_ref[...]).reshape(nb * tt, w)
    r = x[:, 0:W_B]
    k = x[:, W_B:2 * W_B]
    v = x[:, 2 * W_B:3 * W_B]
    o = 3 * W_B
    xw = x[:, o:o + DECAY_LORA]
    xa = x[:, o + DECAY_LORA:o + DECAY_LORA + ICLR_LORA]
    xg = x[:, o + DECAY_LORA + ICLR_LORA:]
    log_w = -_softplus(-(dbase_ref[...] + _bdot(jnp.tanh(xw), dup_ref[...]))) - 0.5
    lw = -jnp.exp(log_w)
    a = _sigmoid(ibase_ref[...] + _bdot(xa, iup_ref[...]))
    g = _bdot(_sigmoid(xg), gup_ref[...])
    hsum = hsum_ref[...]
    kkf = k * kk_ref[...]
    kk = kkf / jnp.maximum(jnp.sqrt(_split_dot(kkf * kkf, hsum)), 1e-12)
    k2 = k * (1.0 + (a - 1.0) * ka_ref[...])
    bonus = _split_dot(r * k2 * rk_ref[...], hsum) * v
    na = -kk
    kb = kk * a
    if t_real < tt:
        valid = (lax.broadcasted_iota(i32, (nb, tt, W_B), 1) < t_real).reshape(nb * tt, W_B)
        zero = lambda u: jnp.where(valid, u, 0.0)
        r, lw, k2, v, na, kb = zero(r), zero(lw), zero(k2), zero(v), zero(na), zero(kb)
    g_out[...] = g.reshape(nb, tt, W_B)
    bonus_out[...] = bonus.reshape(nb, tt, W_B)
    for val, ref in ((r, r_out), (lw, lw_out), (k2, k_out), (v, v_out), (na, a_out), (kb, b_out)):
        val = val.reshape(nb, tt, W_B)
        for h in range(H_B):
            ref[:, h, :, :] = val[:, :, h * HEAD_DIM:(h + 1) * HEAD_DIM]


def _rwkv_prep(pb, prev_first, p, nb, tt, t_real):
    b, t, w = pb.shape
    row = lambda i, j: (i, j, 0)
    const = lambda i, j: (0, 0)
    vec = lambda a: a.reshape(1, -1)
    head_of = jnp.arange(W_B) // HEAD_DIM
    hsum = (head_of[:, None] == head_of[None, :]).astype(bf16)
    params = [vec(p['mu_shift']), vec(p['decay_base']), p['decay_up'], vec(p['iclr_base']), p['iclr_up'],
              p['gate_up'], vec(p['k_k']), vec(p['k_a']), vec(p['r_k']), hsum]
    heads = pl.BlockSpec((nb, H_B, tt, HEAD_DIM), lambda i, j: (i, 0, j, 0))
    return pl.pallas_call(
        functools.partial(_prep_kernel, t_real=t_real),
        grid=(b // nb, t // tt),
        in_specs=[pl.BlockSpec((nb, tt, w), row),
                  pl.BlockSpec((nb, 1, 1, w), lambda i, j: (i, j, 0, 0))]
                 + [pl.BlockSpec(a.shape, const) for a in params],
        out_specs=[heads] * 6 + [pl.BlockSpec((nb, tt, W_B), row)] * 2,
        out_shape=[SDS((b, H_B, t, HEAD_DIM), f32)] * 6 + [SDS((b, t, W_B), f32)] * 2,
        compiler_params=_cparams(("arbitrary", "arbitrary")),
        name="rwkv_prep",
    )(pb, prev_first, *params)


def _wkv_kernel(r_ref, lw_ref, k_ref, v_ref, a_ref, b_ref, s0_ref, y_ref, s_out, st_ref, *, chunk):
    c = chunk
    n = HEAD_DIM
    step = pl.program_id(1)

    n_chain = lw_ref.shape[0] * H_B
    of = lambda ref, i: ref[i // H_B, i % H_B]

    @pl.when(step == 0)
    def _():
        for i in range(n_chain):
            st_ref[i] = of(s0_ref, i).T

    ti = lax.broadcasted_iota(i32, (c, c), 0)
    si = lax.broadcasted_iota(i32, (c, c), 1)
    tri = (ti >= si).astype(bf16)
    eye = (ti == si).astype(f32)
    ones = jnp.ones((c, n), bf16)
    row2 = lax.broadcasted_iota(i32, (c, 2 * c), 0)
    col2 = lax.broadcasted_iota(i32, (c, 2 * c), 1)
    strict2 = (col2 % c) < row2
    incl2 = (col2 % c) <= row2
    right = col2 >= c
    levels = max(c.bit_length() - 2, 0)
    heads = range(n_chain)
    each = lambda f: [f(h) for h in heads]
    lw = each(lambda h: of(lw_ref, h))
    lw3 = each(lambda h: _split3(lw[h]))
    cum = each(lambda h: sum(_dg(tri, t, _NN) for t in lw3[h]))
    wsum = each(lambda h: sum(_dg(t, ones, _TN) for t in lw3[h]))
    w_in = each(lambda h: jnp.exp(cum[h]))
    w_out = each(lambda h: jnp.exp(-cum[h]))
    at = each(lambda h: of(a_ref, h) * jnp.exp(cum[h] - lw[h]))
    rt = each(lambda h: of(r_ref, h) * w_in[h])
    bk = each(lambda h: jnp.concatenate([of(b_ref, h) * w_out[h], of(k_ref, h) * w_out[h]], axis=0))
    g = each(lambda h: _dot3(jnp.concatenate([at[h], rt[h]], axis=0), bk[h], _NT))
    top = each(lambda h: jnp.where(strict2, g[h][:c], 0.0))
    bot = each(lambda h: jnp.where(incl2, g[h][c:], 0.0))
    pw = each(lambda h: top[h][:, :c])
    inv = each(lambda h: eye + pw[h])
    for _ in range(levels):
        pw = each(lambda h: _dot3(pw[h], pw[h]))
        inv = each(lambda h: inv[h] + _dot3(inv[h], pw[h]))
    vv = each(lambda h: jnp.concatenate([of(v_ref, h), of(v_ref, h)], axis=0))
    xv = each(lambda h: _dot3(jnp.where(right, top[h], 0.0), vv[h]))
    x = each(lambda h: _dot3(at[h], st_ref[h]) + xv[h])
    u = each(lambda h: _dot3(inv[h], x[h]))
    uv = each(lambda h: jnp.concatenate([u[h], of(v_ref, h)], axis=0))
    y = each(lambda h: _bdot(bot[h], uv[h]) + _bdot(rt[h], st_ref[h]))
    st_new = each(lambda h: (st_ref[h] + _dot3(bk[h], uv[h], _TN)) * jnp.exp(wsum[h]))
    for h in heads:
        st_ref[h] = st_new[h]
        mu = jnp.mean(y[h], axis=-1, keepdims=True)
        yc = y[h] - mu
        var = jnp.mean(yc * yc, axis=-1, keepdims=True)
        col0 = (h % H_B) * HEAD_DIM
        y_ref[h // H_B, :, col0:col0 + HEAD_DIM] = yc * lax.rsqrt(var + LNX_EPS)

    @pl.when(step == pl.num_programs(1) - 1)
    def _():
        for i in range(n_chain):
            s_out[i // H_B, i % H_B] = st_ref[i].T


def _wkv_scan(r, lw, k, v, a, b, s0, chunk, nseq):
    bsz, h, t, n = r.shape
    nseq = math.gcd(bsz, nseq)
    seq = pl.BlockSpec((nseq, h, chunk, n), lambda i, j: (i, 0, j, 0))
    state = pl.BlockSpec((nseq, h, n, n), lambda i, j: (i, 0, 0, 0))
    return pl.pallas_call(
        functools.partial(_wkv_kernel, chunk=chunk),
        grid=(bsz // nseq, t // chunk),
        in_specs=[seq] * 6 + [state],
        out_specs=[pl.BlockSpec((nseq, chunk, h * n), lambda i, j: (i, j, 0)), state],
        out_shape=[SDS((bsz, t, h * n), f32), SDS((bsz, h, n, n), f32)],
        scratch_shapes=[pltpu.VMEM((nseq * h, n, n), f32)],
        compiler_params=_cparams(("arbitrary", "arbitrary")),
        name="wkv_scan",
    )(r, lw, k, v, a, b, s0)


def _outproj_kernel(attn_ref, yn_ref, bonus_ref, g_ref, x_ref, gt1_ref, sh2_ref, sc2_ref,
                    gattn_ref, lnxg_ref, lnxb_ref, wout_ref, gpost_ref, gpre_ref, rw_ref, rb_ref,
                    x1_ref, h2_ref, logit_ref):
    nb, tt, d = x_ref.shape
    o_a = _rms(attn_ref[...], gattn_ref[...])
    o_b = (yn_ref[...] * lnxg_ref[...] + lnxb_ref[...] + bonus_ref[...]) * g_ref[...]
    cat = jnp.concatenate([o_a, o_b], axis=-1).reshape(nb * tt, d)
    catb = cat.astype(bf16)
    half = d // 2
    mixed = jnp.concatenate([jnp.dot(catb, wout_ref[:, :half], preferred_element_type=f32),
                             jnp.dot(catb, wout_ref[:, half:], preferred_element_type=f32)],
                            axis=1).reshape(nb, tt, d)
    x1 = x_ref[...] + gt1_ref[...] * _rms(mixed, gpost_ref[...])
    x1_ref[...] = x1
    h2 = _rms(x1, gpre_ref[...]) * (1.0 + sc2_ref[...]) + sh2_ref[...]
    _store_token_tiles(h2_ref, 0, h2.reshape(nb * tt, d))
    logits = _dot3(h2.reshape(nb * tt, d), rw_ref[...]) + rb_ref[...]
    logit_ref[...] = logits.reshape(nb, tt, N_EXPERTS)


def _out_proj(attn, yn, bonus, g, x, gt1, sh2, sc2, p, w_out_bf16, nb, tt):
    b, t, d = x.shape
    row = lambda w: pl.BlockSpec((nb, tt, w), lambda i, j: (i, j, 0))
    mod = pl.BlockSpec((nb, 1, d), lambda i, j: (i, 0, 0))
    vec = lambda a: a.reshape(1, -1)
    params = [vec(p['g_attn_out']), vec(p['lnx_g']), vec(p['lnx_b']), w_out_bf16,
              vec(p['g_mix_post']), vec(p['g_ffn_pre']), p['router_w'], vec(p['router_b'])]
    return pl.pallas_call(
        _outproj_kernel,
        grid=(b // nb, t // tt),
        in_specs=[row(W_A), row(W_B), row(W_B), row(W_B), row(d), mod, mod, mod]
                 + [pl.BlockSpec(a.shape, lambda i, j: (0, 0)) for a in params],
        out_specs=[row(d), pl.BlockSpec((nb, tt, V7X_SUBLANES, V7X_LANES), lambda i, j: (i, j, 0, 0)),
                   row(N_EXPERTS)],
        out_shape=[SDS((b, t, d), f32), SDS((b, t, V7X_SUBLANES, V7X_LANES), f32),
                   SDS((b, t, N_EXPERTS), f32)],
        compiler_params=_cparams(("arbitrary", "arbitrary")),
        name="out_proj",
    )(attn, yn, bonus, g, x, gt1, sh2, sc2, *params)


def _route_kernel(logit_ref, idx_ref, gate_ref, rank_ref, count_ref, base_ref):
    tile = logit_ref.shape[0]

    @pl.when(pl.program_id(0) == 0)
    def _():
        base_ref[...] = jnp.zeros_like(base_ref)

    lane = lax.broadcasted_iota(i32, (tile, N_EXPERTS), 1).astype(f32)
    cur = logit_ref[...]
    hots, vals, idxs = [], [], []
    for _ in range(TOP_K):
        top = jnp.max(cur, axis=-1, keepdims=True)
        idx = jnp.min(jnp.where(cur == top, lane, float(N_EXPERTS)), axis=-1, keepdims=True)
        hot = lane == idx
        hots.append(hot)
        vals.append(top)
        idxs.append(idx)
        cur = jnp.where(hot, -jnp.inf, cur)
    es = [jnp.exp(v - vals[0]) for v in vals]
    total = sum(es)
    chosen = sum(h.astype(f32) for h in hots)
    ti = lax.broadcasted_iota(i32, (tile, tile), 0)
    si = lax.broadcasted_iota(i32, (tile, tile), 1)
    earlier = jnp.dot((ti > si).astype(bf16), chosen.astype(bf16), preferred_element_type=f32) + base_ref[...]
    ranks = [jnp.sum(jnp.where(h, earlier, 0.0), axis=-1, keepdims=True) for h in hots]
    base_ref[...] += jnp.sum(chosen, axis=0, keepdims=True)
    count_ref[...] = base_ref[...]
    col = lax.broadcasted_iota(i32, (tile, TOP_K), 1)
    pick = lambda parts: sum(jnp.where(col == k, parts[k], 0.0) for k in range(TOP_K))
    idx_ref[...] = pick(idxs).astype(i32)
    gate_ref[...] = pick([e / total for e in es])
    rank_ref[...] = pick(ranks).astype(i32)


def _route(logits, tile):
    m = logits.shape[0]
    tok = lambda dt: SDS((m, TOP_K), dt)
    blk = pl.BlockSpec((tile, TOP_K), lambda i: (i, 0))
    return pl.pallas_call(
        _route_kernel,
        grid=(m // tile,),
        in_specs=[pl.BlockSpec((tile, N_EXPERTS), lambda i: (i, 0))],
        out_specs=[blk, blk, blk, pl.BlockSpec((1, N_EXPERTS), lambda i: (0, 0))],
        out_shape=[tok(i32), tok(f32), tok(i32), SDS((1, N_EXPERTS), f32)],
        scratch_shapes=[pltpu.VMEM((1, N_EXPERTS), f32)],
        compiler_params=_cparams(("arbitrary",)),
        name="moe_route",
    )(logits)


def _row_copy(src, src_row, dst, dst_row, sem):
    return pltpu.make_async_copy(src.at[pl.ds(src_row, 1)], dst.at[pl.ds(dst_row, 1)], sem)


def _dispatch_kernel(dest_ref, xa_ref, xb_ref, xs_in, xs_out, sem, *, n_first):
    del xs_in
    tile = xa_ref.shape[0]

    def scatter_rows(x_ref):
        def start(r, _):
            for k in range(TOP_K):
                _row_copy(x_ref, r, xs_out, dest_ref[k, r], sem).start(priority=k % 2)
            return 0

        lax.fori_loop(0, tile, start, 0, unroll=DMA_ISSUE_UNROLL)
        for k in range(TOP_K):
            pltpu.make_async_copy(x_ref, xs_out.at[pl.ds(0, tile)], sem).wait()

    @pl.when(pl.program_id(0) < n_first)
    def _():
        scatter_rows(xa_ref)

    @pl.when(pl.program_id(0) >= n_first)
    def _():
        scatter_rows(xb_ref)


def _dispatch(xa, xb, dest_t, n_slots, tile):
    row_tile = xa.shape[1:]
    n_first = xa.shape[0] // tile
    return pl.pallas_call(
        functools.partial(_dispatch_kernel, n_first=n_first),
        grid=(n_first + xb.shape[0] // tile,),
        in_specs=[pl.BlockSpec((TOP_K, tile), lambda i: (0, i), memory_space=pltpu.SMEM),
                  pl.BlockSpec((tile,) + row_tile, lambda i: (jnp.minimum(i, n_first - 1), 0, 0)),
                  pl.BlockSpec((tile,) + row_tile, lambda i: (jnp.maximum(i - n_first, 0), 0, 0)),
                  pl.BlockSpec(memory_space=pl.ANY)],
        out_specs=pl.BlockSpec(memory_space=pl.ANY),
        out_shape=SDS((n_slots,) + row_tile, xa.dtype),
        scratch_shapes=[pltpu.SemaphoreType.DMA(())],
        input_output_aliases={3: 0},
        compiler_params=_cparams(("arbitrary",)),
        name="moe_dispatch",
    )(dest_t, xa, xb, jnp.zeros((n_slots,) + row_tile, xa.dtype))


def _combine_kernel(dest_ref, dest_next_ref, gate_ref, ys_hbm, x1a_ref, gt2a_ref, x1b_ref, gt2b_ref, g_ref,
                    oa_ref, ob_ref, buf_ref, sems, *, n_first):
    i = pl.program_id(0)
    n = pl.num_programs(0)
    tile = oa_ref.shape[0]

    def issue(dref, slot):
        def start(r, _):
            for k in range(TOP_K):
                _row_copy(ys_hbm, dref[k, r], buf_ref.at[slot, k], r, sems.at[slot]).start(priority=k % 2)
            return 0

        lax.fori_loop(0, tile, start, 0, unroll=DMA_ISSUE_UNROLL)

    def finish(slot):
        @pl.when(i + 1 < n)
        def _():
            issue(dest_next_ref, 1 - slot)

        for k in range(TOP_K):
            pltpu.make_async_copy(ys_hbm.at[pl.ds(0, tile)], buf_ref.at[slot, k], sems.at[slot]).wait()
        gate = gate_ref[...]
        f = sum(gate[:, k:k + 1] * _load_token_tiles(buf_ref, (slot * TOP_K + k) * tile, tile)
                for k in range(TOP_K))
        nf = _rms(f, g_ref[...])

        @pl.when(i < n_first)
        def _():
            oa_ref[...] = x1a_ref[...] + gt2a_ref[0] * nf

        @pl.when(i >= n_first)
        def _():
            ob_ref[...] = x1b_ref[...] + gt2b_ref[...] * nf

    @pl.when(i == 0)
    def _():
        issue(dest_ref, 0)

    @pl.when(i % 2 == 0)
    def _():
        finish(0)

    @pl.when(i % 2 == 1)
    def _():
        finish(1)


def _combine(ys, dest_t, gate, x1_a, gt2_a, x1_b, gt2_b, g_post, tile):
    m = gate.shape[0]
    m_a, d = x1_a.shape
    n_first = m_a // tile
    n_tiles = m // tile
    tiles_per_seq = m_a // gt2_a.shape[0] // tile
    first = lambda i: jnp.minimum(i, n_first - 1)
    second = lambda i: jnp.maximum(i - n_first, 0)
    return pl.pallas_call(
        functools.partial(_combine_kernel, n_first=n_first),
        grid=(n_tiles,),
        in_specs=[pl.BlockSpec((TOP_K, tile), lambda i: (0, i), memory_space=pltpu.SMEM),
                  pl.BlockSpec((TOP_K, tile), lambda i: (0, jnp.minimum(i + 1, n_tiles - 1)),
                               memory_space=pltpu.SMEM),
                  pl.BlockSpec((tile, TOP_K), lambda i: (i, 0)),
                  pl.BlockSpec(memory_space=pl.ANY),
                  pl.BlockSpec((tile, d), lambda i: (first(i), 0)),
                  pl.BlockSpec((1, 1, d), lambda i: (first(i) // tiles_per_seq, 0, 0)),
                  pl.BlockSpec((tile, d), lambda i: (second(i), 0)),
                  pl.BlockSpec((tile, d), lambda i: (second(i), 0)),
                  pl.BlockSpec((1, d), lambda i: (0, 0))],
        out_specs=[pl.BlockSpec((tile, d), lambda i: (first(i), 0)),
                   pl.BlockSpec((tile, d), lambda i: (second(i), 0))],
        out_shape=[SDS((m_a, d), f32), SDS((m - m_a, d), f32)],
        scratch_shapes=[pltpu.VMEM((2, TOP_K, tile) + ys.shape[1:], f32), pltpu.SemaphoreType.DMA((2,))],
        compiler_params=_cparams(("arbitrary",)),
        name="moe_combine",
    )(dest_t, dest_t, gate, ys, x1_a, gt2_a, x1_b, gt2_b, g_post.reshape(1, d))


def _expert_kernel(be_ref, nused_ref, x_ref, w1_ref, b1_ref, w2_ref, b2_ref, y_ref, w1b_ref, w2b_ref):
    i = pl.program_id(0)
    prev = be_ref[jnp.maximum(i - 1, 0)]

    @pl.when((i == 0) | (be_ref[i] != prev))
    def _():
        w1b_ref[...] = w1_ref[0].astype(bf16)
        w2b_ref[...] = w2_ref[0].astype(bf16)

    @pl.when(i < nused_ref[0])
    def _():
        x = _load_token_tiles(x_ref, 0, x_ref.shape[0]).astype(bf16)
        u = jnp.dot(x, w1b_ref[...], preferred_element_type=f32) + b1_ref[0]
        u_glu = jnp.minimum(u[:, :D_FF], SWIGLU_LIMIT)
        u_lin = jnp.clip(u[:, D_FF:], -SWIGLU_LIMIT, SWIGLU_LIMIT)
        act = u_glu * _sigmoid(SWIGLU_ALPHA * u_glu) * (u_lin + 1.0)
        y = jnp.dot(act.astype(bf16), w2b_ref[...], preferred_element_type=f32) + b2_ref[0]
        _store_token_tiles(y_ref, 0, y)

    @pl.when(i >= nused_ref[0])
    def _():
        y_ref[...] = jnp.zeros_like(y_ref)


def _expert_ffn(xs, blk_expert, n_used, e_w1, e_b1, e_w2, e_b2, tm):
    n_slots = xs.shape[0]
    row_tile = xs.shape[1:]
    d = math.prod(row_tile)
    n_blocks = n_slots // tm
    rows = pl.BlockSpec((tm,) + row_tile, lambda i, be, nu: (i, 0, 0))
    ex = lambda i, be, nu: (be[i], 0, 0)
    grid_spec = pltpu.PrefetchScalarGridSpec(
        num_scalar_prefetch=2,
        grid=(n_blocks,),
        in_specs=[rows,
                  pl.BlockSpec((1, d, 2 * D_FF), ex),
                  pl.BlockSpec((1, 1, 2 * D_FF), ex),
                  pl.BlockSpec((1, D_FF, d), ex),
                  pl.BlockSpec((1, 1, d), ex)],
        out_specs=rows,
        scratch_shapes=[pltpu.VMEM((d, 2 * D_FF), bf16), pltpu.VMEM((D_FF, d), bf16)],
    )
    return pl.pallas_call(
        _expert_kernel,
        grid_spec=grid_spec,
        out_shape=SDS((n_slots,) + row_tile, f32),
        compiler_params=_cparams(("arbitrary",)),
        name="expert_ffn",
    )(blk_expert, n_used, xs, e_w1, e_b1.reshape(N_EXPERTS, 1, -1), e_w2, e_b2.reshape(N_EXPERTS, 1, -1))


def _moe_residual(h2_a, logits_a, x1_a, gt2_a, h2_b, logits_b, x1_b, gt2_b, g_post, e_w1, e_b1, e_w2, e_b2, tm):
    tile = MOE_TOKEN_TILE
    m_a, m_b = h2_a.shape[0], h2_b.shape[0]
    assert m_a % tile == 0
    extra = -m_b % tile
    h2_b = jnp.pad(h2_b, ((0, extra), (0, 0), (0, 0)))
    x1_b = jnp.pad(x1_b, ((0, extra), (0, 0)))
    gt2_b = jnp.pad(gt2_b, ((0, extra), (0, 0)))
    logits = jnp.concatenate([logits_a, logits_b, jnp.zeros((extra, N_EXPERTS), f32)], axis=0)
    m = m_a + m_b + extra
    route_extra = -m % MOE_ROUTE_TILE
    top_idx, gate, rank, counts = _route(jnp.pad(logits, ((0, route_extra), (0, 0))), MOE_ROUTE_TILE)
    top_idx, gate, rank = top_idx[:m], gate[:m], rank[:m]
    counts = counts.reshape(N_EXPERTS).astype(i32)
    padded = (counts + tm - 1) // tm * tm
    pad_end = jnp.cumsum(padded)
    pad_start = pad_end - padded
    n_blocks = ((m + route_extra) * TOP_K + N_EXPERTS * (tm - 1) + tm - 1) // tm
    blk_row = (jnp.arange(n_blocks) * tm)[:, None]
    blk_expert = jnp.minimum(jnp.sum(pad_end[None, :] <= blk_row, axis=1), N_EXPERTS - 1).astype(i32)
    n_used = (pad_end[-1] // tm).astype(i32).reshape(1)
    group_start = jnp.sum(jnp.where(top_idx[:, :, None] == jnp.arange(N_EXPERTS), pad_start, 0), axis=-1)
    dest_t = (group_start + rank).astype(i32).T
    xs = _dispatch(h2_a, h2_b, dest_t, n_blocks * tm, tile)
    ys = _expert_ffn(xs, blk_expert, n_used, e_w1, e_b1, e_w2, e_b2, tm)
    out_a, out_b = _combine(ys, dest_t, gate, x1_a, gt2_a, x1_b, gt2_b, g_post, tile)
    return out_a, out_b[:m_b]


def _mix_layer(x, mod, attend, shift_prev, wkv_prev, p, w_in_bf16, w_out_bf16, nb, tt, t_real, chunk, wkv_nseq):
    b, t, d = x.shape
    sh1, sc1, gt1, sh2, sc2, gt2 = [mod[:, i:i + 1, :] for i in range(6)]
    q, k, v, pb, last = _in_proj(x, sh1, sc1, p['g_mix_pre'], w_in_bf16, nb, tt, t_real)
    attn = attend(q, k, v)
    prev_first = jnp.concatenate([shift_prev[:, None, None, :], last[:, :-1]], axis=1)
    r, lw, k2, v2, a, kb, g, bonus = _rwkv_prep(pb, prev_first, p, nb, tt, t_real)
    yn, wkv_new = _wkv_scan(r, lw, k2, v2, a, kb, wkv_prev, chunk, wkv_nseq)
    x1, h2, logits = _out_proj(attn, yn, bonus, g, x, gt1, sh2, sc2, p, w_out_bf16, nb, tt)
    return x1, h2, logits, gt2, k, v, wkv_new, last[:, -1, 0]


def kernel(x_prompt, x_sample, cache_k, cache_v, state_wkv, state_shift, page_table, c_prompt, c_sample,
           w_ada, b_ada, g_mix_pre, g_mix_post, g_ffn_pre, g_ffn_post, w_in, mu_shift,
           decay_base, decay_up, iclr_base, iclr_up, gate_up, k_k, k_a, r_k, lnx_g, lnx_b,
           g_attn_out, sb_bias, w_out, router_w, router_b, e_w1, e_b1, e_w2, e_b2):
    weights = dict(w_ada=w_ada, b_ada=b_ada, g_mix_pre=g_mix_pre, g_mix_post=g_mix_post,
                   g_ffn_pre=g_ffn_pre, g_ffn_post=g_ffn_post, w_in=w_in, mu_shift=mu_shift,
                   decay_base=decay_base, decay_up=decay_up, iclr_base=iclr_base, iclr_up=iclr_up,
                   gate_up=gate_up, k_k=k_k, k_a=k_a, r_k=r_k, lnx_g=lnx_g, lnx_b=lnx_b,
                   g_attn_out=g_attn_out, sb_bias=sb_bias, w_out=w_out, router_w=router_w,
                   router_b=router_b, e_w1=e_w1, e_b1=e_b1, e_w2=e_w2, e_b2=e_b2)
    depth = w_ada.shape[0]
    bp, tp, d = x_prompt.shape
    bs, ts, _ = x_sample.shape
    ts_pad = -(-ts // V7X_SUBLANES) * V7X_SUBLANES
    hp = x_prompt
    hs = jnp.pad(x_sample, ((0, 0), (0, ts_pad - ts), (0, 0)))
    outs = [[] for _ in range(8)]
    for l in range(depth):
        p = {name: w[l] for name, w in weights.items()}
        w_in_bf16 = p['w_in'].astype(bf16)
        w_out_bf16 = p['w_out'].astype(bf16)
        mod = _ada_mod(jnp.concatenate([c_prompt, c_sample], axis=0), p['w_ada'], p['b_ada'])
        mod = mod.reshape(bp + bs, 6, d)

        attend_p = lambda q, k, v: _attn_prompt(q, k, v, p['sb_bias'])
        x1p, h2p, lgp, gt2p, kp, vp, wp, shp = _mix_layer(
            hp, mod[:bp], attend_p, jnp.zeros((bp, SHIFT_W), f32),
            jnp.zeros((bp, H_B, HEAD_DIM, HEAD_DIM), f32), p, w_in_bf16, w_out_bf16,
            nb=1, tt=ROW_TILE, t_real=tp, chunk=WKV_CHUNK, wkv_nseq=WKV_NSEQ)

        attend_s = lambda q, k, v: jnp.pad(
            _attn_sample(q, k, v, cache_k, cache_v, l, page_table, p['sb_bias'], ts),
            ((0, 0), (0, ts_pad - ts), (0, 0)))
        x1s, h2s, lgs, gt2s, ks, vs, ws, shs = _mix_layer(
            hs, mod[bp:], attend_s, state_shift[l], state_wkv[l], p, w_in_bf16, w_out_bf16,
            nb=bs, tt=ts_pad, t_real=ts, chunk=ts_pad, wkv_nseq=WKV_NSEQ)

        mp = bp * tp
        ms = bs * ts
        yp, ys = _moe_residual(
            h2p.reshape((mp,) + h2p.shape[2:]), lgp.reshape(mp, N_EXPERTS), x1p.reshape(mp, d), gt2p,
            h2s[:, :ts].reshape((ms,) + h2s.shape[2:]), lgs[:, :ts].reshape(ms, N_EXPERTS),
            x1s[:, :ts].reshape(ms, d), jnp.broadcast_to(gt2s, (bs, ts, d)).reshape(ms, d),
            p['g_ffn_post'], p['e_w1'], p['e_b1'], p['e_w2'], p['e_b2'], MOE_TILE)
        hp = yp.reshape(bp, tp, d)
        hs = jnp.pad(ys.reshape(bs, ts, d), ((0, 0), (0, ts_pad - ts), (0, 0)))

        for lst, val in zip(outs, (kp.reshape(bp, tp, H_A, HEAD_DIM), vp.reshape(bp, tp, H_A, HEAD_DIM),
                                   ks[:, :ts].reshape(bs, ts, H_A, HEAD_DIM),
                                   vs[:, :ts].reshape(bs, ts, H_A, HEAD_DIM), wp, ws, shp, shs)):
            lst.append(val)
    return (hp, hs[:, :ts]) + tuple(jnp.stack(lst) for lst in outs)
```

```python
import functools
import math

import jax
import jax.numpy as jnp
from jax import lax
from jax.experimental import pallas as pl
from jax.experimental.pallas import tpu as pltpu

f32 = jnp.float32
bf16 = jnp.bfloat16
i32 = jnp.int32
SDS = jax.ShapeDtypeStruct

D_MODEL = 1024
HEAD_DIM = 64
W_A = 512
W_B = 512
H_A = W_A // HEAD_DIM
H_B = W_B // HEAD_DIM
DECAY_LORA = 64
ICLR_LORA = 64
GATE_LORA = 128
SHIFT_W = 3 * W_B + DECAY_LORA + ICLR_LORA + GATE_LORA
N_EXPERTS = 32
TOP_K = 4
D_FF = D_MODEL
SWIGLU_ALPHA = 1.702
SWIGLU_LIMIT = 7.0
PAGE_SIZE = 128
RMS_EPS = 1e-6
LNX_EPS = 64e-5
LOG2E = 1.4426950408889634

V7X_SUBLANES = 8
V7X_LANES = 128
VMEM_LIMIT = 56 * 1024 * 1024

ROW_TILE = 256
WKV_CHUNK = 64
WKV_NSEQ = 4
ATTN_TQ = 512
ATTN_TK = 128
ATTN_NSUB = 4
ATTN_DIAG_BLOCKS = 2
PAGES_PER_STEP = 16
MOE_TILE = 256
MOE_TOKEN_TILE = 128
MOE_ROUTE_TILE = 256
DMA_ISSUE_UNROLL = 8


def _cparams(sem):
    return pltpu.CompilerParams(dimension_semantics=sem, vmem_limit_bytes=VMEM_LIMIT)


def _bdot(a, b):
    return jnp.dot(a.astype(bf16), b.astype(bf16), preferred_element_type=f32)


def _fdot(a, b, dims=(((1,), (0,)), ((), ()))):
    return lax.dot_general(a, b, dims, precision=lax.Precision.HIGHEST, preferred_element_type=f32)


_NT = (((1,), (1,)), ((), ()))
_TN = (((0,), (0,)), ((), ()))


def _split_dot(x, m_bf16):
    hi = x.astype(bf16)
    lo = (x - hi.astype(f32)).astype(bf16)
    return (jnp.dot(hi, m_bf16, preferred_element_type=f32)
            + jnp.dot(lo, m_bf16, preferred_element_type=f32))


def _sigmoid(x):
    return 1.0 / (1.0 + jnp.exp(-x))


def _softplus(x):
    return jnp.maximum(x, 0.0) + jnp.log(1.0 + jnp.exp(-jnp.abs(x)))


def _softplus2(z):
    return jnp.maximum(z, 0.0) + jnp.log2(1.0 + jnp.exp2(-jnp.abs(z)))


def _split2(x):
    hi = x.astype(bf16)
    return hi, (x - hi.astype(f32)).astype(bf16)


def _split3(x):
    hi = x.astype(bf16)
    r = x - hi.astype(f32)
    mid = r.astype(bf16)
    return hi, mid, (r - mid.astype(f32)).astype(bf16)


def _dg(a, b, dims):
    return lax.dot_general(a, b, dims, preferred_element_type=f32)


_NN = (((1,), (0,)), ((), ()))


def _dot3(a, b, dims=_NN):
    ah, al = _split2(a)
    bh, bl = _split2(b)
    return _dg(ah, bh, dims) + _dg(ah, bl, dims) + _dg(al, bh, dims)


def _dot_exact_rhs(x3, m_bf16, dims=_NN):
    return _dg(x3[0], m_bf16, dims) + _dg(x3[1], m_bf16, dims) + _dg(x3[2], m_bf16, dims)


def _rms(x, g):
    return x * lax.rsqrt(jnp.mean(x * x, axis=-1, keepdims=True) + RMS_EPS) * g


def _store_token_tiles(ref, first_row, x2d):
    rows = x2d.shape[0]
    flat = ref.reshape(math.prod(ref.shape) // V7X_LANES, V7X_LANES)
    for c in range(V7X_SUBLANES):
        flat[pl.ds(first_row * V7X_SUBLANES + c, rows, stride=V7X_SUBLANES), :] = (
            x2d[:, c * V7X_LANES:(c + 1) * V7X_LANES])


def _load_token_tiles(ref, first_row, rows):
    flat = ref.reshape(math.prod(ref.shape) // V7X_LANES, V7X_LANES)
    return jnp.concatenate([flat[pl.ds(first_row * V7X_SUBLANES + c, rows, stride=V7X_SUBLANES), :]
                            for c in range(V7X_SUBLANES)], axis=1)


def _ada_kernel(c_ref, w_ref, b_ref, o_ref):
    c = c_ref[...]
    o_ref[...] = _bdot(c * _sigmoid(c), w_ref[...]) + b_ref[...]


def _ada_mod(c, w_ada, b_ada):
    n, d = c.shape
    nout = w_ada.shape[1]
    tn = 1536
    return pl.pallas_call(
        _ada_kernel,
        grid=(nout // tn,),
        in_specs=[pl.BlockSpec((n, d), lambda j: (0, 0)),
                  pl.BlockSpec((d, tn), lambda j: (0, j)),
                  pl.BlockSpec((1, tn), lambda j: (0, j))],
        out_specs=pl.BlockSpec((n, tn), lambda j: (0, j)),
        out_shape=SDS((n, nout), f32),
        compiler_params=_cparams(("arbitrary",)),
        name="ada_mod",
    )(c, w_ada, b_ada.reshape(1, nout))


def _inproj_kernel(x_ref, sh_ref, sc_ref, g_ref, w_ref, q_ref, k_ref, v_ref, pb_ref, last_ref, *, t_last):
    nb, tt, d = x_ref.shape
    h = _rms(x_ref[...], g_ref[...]) * (1.0 + sc_ref[...]) + sh_ref[...]
    hb = h.reshape(nb * tt, d).astype(bf16)
    q_ref[...] = jnp.dot(hb, w_ref[:, 0:W_A], preferred_element_type=f32).reshape(nb, tt, W_A)
    k_ref[...] = jnp.dot(hb, w_ref[:, W_A:2 * W_A], preferred_element_type=f32).reshape(nb, tt, W_A)
    v_ref[...] = jnp.dot(hb, w_ref[:, 2 * W_A:3 * W_A], preferred_element_type=f32).reshape(nb, tt, W_A)
    pb = jnp.dot(hb, w_ref[:, 3 * W_A:], preferred_element_type=f32).reshape(nb, tt, SHIFT_W)
    pb_ref[...] = pb
    last_ref[:, 0] = pb[:, t_last:t_last + 1, :]


def _in_proj(x, sh1, sc1, g, w_in_bf16, nb, tt, t_real):
    b, t, d = x.shape
    row = lambda w: pl.BlockSpec((nb, tt, w), lambda i, j: (i, j, 0))
    mod = pl.BlockSpec((nb, 1, d), lambda i, j: (i, 0, 0))
    return pl.pallas_call(
        functools.partial(_inproj_kernel, t_last=min(tt, t_real) - 1),
        grid=(b // nb, t // tt),
        in_specs=[row(d), mod, mod,
                  pl.BlockSpec((1, d), lambda i, j: (0, 0)),
                  pl.BlockSpec(w_in_bf16.shape, lambda i, j: (0, 0))],
        out_specs=[row(W_A), row(W_A), row(W_A), row(SHIFT_W),
                   pl.BlockSpec((nb, 1, 1, SHIFT_W), lambda i, j: (i, j, 0, 0))],
        out_shape=[SDS((b, t, W_A), f32)] * 3 + [SDS((b, t, SHIFT_W), f32),
                                                  SDS((b, t // tt, 1, SHIFT_W), f32)],
        compiler_params=_cparams(("arbitrary", "arbitrary")),
        name="in_proj",
    )(x, sh1, sc1, g.reshape(1, d), w_in_bf16)


def _attn_prompt_kernel(bias_ref, q_ref, k_ref, v_ref, u2_ref, o_ref, kbd_ref, vbd_ref, acc_ref,
                        *, tq, tk, nsub):
    hp = pl.program_id(1)
    qi = pl.program_id(2)
    two = 2 * tk
    big = tk * nsub
    ratio = tq // big
    n_tiles = k_ref.shape[1] // tk

    @pl.when(qi == 0)
    def _():
        first = lax.broadcasted_iota(i32, (tk, 2 * HEAD_DIM), 1) < HEAD_DIM

        def build(j, _):
            start = pl.multiple_of(j * tk, tk)
            kt = k_ref[0, pl.ds(start, tk), :]
            vt = v_ref[0, pl.ds(start, tk), :]
            kbd_ref[j, 0:tk, :] = jnp.where(first, kt, 0.0).astype(bf16)
            kbd_ref[j, tk:two, :] = jnp.where(first, 0.0, kt).astype(bf16)
            vbd_ref[j, 0:tk, :] = jnp.where(first, vt, 0.0).astype(bf16)
            vbd_ref[j, tk:two, :] = jnp.where(first, 0.0, vt).astype(bf16)
            return 0

        lax.fori_loop(0, n_tiles, build, 0)

    qb = (q_ref[0] * (LOG2E * HEAD_DIM ** -0.5)).astype(bf16)
    lane2 = lax.broadcasted_iota(i32, (1, two), 1)
    bias2 = jnp.where(lane2 < tk, bias_ref[2 * hp], bias_ref[2 * hp + 1]) * LOG2E
    bias_row = jnp.concatenate([bias2] * nsub, axis=1)
    u2 = u2_ref[...]
    acc_ref[...] = jnp.zeros_like(acc_ref)

    def chunk(tile0, ns, r0, nr, diagonal, carry):
        kb = kbd_ref[pl.ds(tile0, ns)].reshape(ns * two, 2 * HEAD_DIM)
        vb = vbd_ref[pl.ds(tile0, ns)].reshape(ns * two, 2 * HEAD_DIM)
        z = _dg(qb[r0:r0 + nr], kb, _NT) + bias_row[:, :ns * two]
        sp = _softplus2(z)
        if diagonal:
            col = lax.broadcasted_iota(i32, (nr, ns * two), 1)
            key_off = (col // two) * tk + col % tk
            mask = key_off < lax.broadcasted_iota(i32, (nr, ns * two), 0) + r0
            sp = jnp.where(mask, sp, 0.0)
        spb = sp.astype(bf16)
        ws = [None] * ns
        for j in range(ns - 1, -1, -1):
            incl = jnp.dot(spb[:, j * two:(j + 1) * two], u2, preferred_element_type=f32)
            ws[j] = jnp.exp2(z[:, j * two:(j + 1) * two] - incl - carry)
            carry = carry + jnp.concatenate([jnp.broadcast_to(incl[:, 0:1], (nr, tk)),
                                             jnp.broadcast_to(incl[:, tk:tk + 1], (nr, tk))], axis=1)
        w = jnp.concatenate(ws, axis=1)
        if diagonal:
            w = jnp.where(mask, w, 0.0)
        acc_ref[r0:r0 + nr, :] += jnp.dot(w.astype(bf16), vb, preferred_element_type=f32)
        return carry

    assert ratio == 1 and nsub % ATTN_DIAG_BLOCKS == 0
    rows_per = tq // ATTN_DIAG_BLOCKS
    tiles_per = nsub // ATTN_DIAG_BLOCKS
    carry = jnp.concatenate(
        [chunk(qi * nsub, (rb + 1) * tiles_per, rb * rows_per, rows_per, True, jnp.zeros((rows_per, two), f32))
         for rb in range(ATTN_DIAG_BLOCKS)], axis=0)

    def body(n, carry):
        return chunk((qi - 1 - n) * nsub, nsub, 0, tq, False, carry)

    lax.fori_loop(0, qi * ratio, body, carry)
    o_ref[0] = acc_ref[...]


def _attn_prompt(q, k, v, sb_bias):
    b, t, _ = q.shape
    tq, tk, nsub = ATTN_TQ, ATTN_TK, ATTN_NSUB
    j = lax.broadcasted_iota(i32, (2 * tk, 2 * tk), 0)
    s = lax.broadcasted_iota(i32, (2 * tk, 2 * tk), 1)
    u2 = ((j >= s) & ((j // tk) == (s // tk))).astype(bf16)
    seq = pl.BlockSpec((1, t, 2 * HEAD_DIM), lambda bi, hp, qi: (bi, 0, hp))
    return pl.pallas_call(
        functools.partial(_attn_prompt_kernel, tq=tq, tk=tk, nsub=nsub),
        grid=(b, H_A // 2, t // tq),
        in_specs=[pl.BlockSpec(memory_space=pltpu.SMEM),
                  pl.BlockSpec((1, tq, 2 * HEAD_DIM), lambda bi, hp, qi: (bi, qi, hp)),
                  seq, seq,
                  pl.BlockSpec((2 * tk, 2 * tk), lambda bi, hp, qi: (0, 0))],
        out_specs=pl.BlockSpec((1, tq, 2 * HEAD_DIM), lambda bi, hp, qi: (bi, qi, hp)),
        out_shape=SDS((b, t, W_A), f32),
        scratch_shapes=[pltpu.VMEM((t // tk, 2 * tk, 2 * HEAD_DIM), bf16),
                        pltpu.VMEM((t // tk, 2 * tk, 2 * HEAD_DIM), bf16),
                        pltpu.VMEM((tq, 2 * HEAD_DIM), f32)],
        compiler_params=_cparams(("arbitrary", "arbitrary", "arbitrary")),
        name="attn_prompt",
    )(sb_bias, q, k, v, u2)


def _attn_sample_kernel(pt_ref, bias_ref, qrow_ref, knew_ref, vnew_ref, uo_ref, *rest, n_new, pages):
    kp_refs = rest[:pages]
    vp_refs = rest[pages:2 * pages]
    o_ref, acc_ref, carry_ref = rest[2 * pages:]
    step = pl.program_id(1)
    n_row = H_A * n_new
    row_head = lax.broadcasted_iota(i32, (n_row, PAGE_SIZE), 0) // n_new
    bias = jnp.broadcast_to(bias_ref[...], (n_row, PAGE_SIZE))

    def add_values(w, value_of_head, dims):
        for h in range(H_A):
            acc_ref[...] += _dg(jnp.where(row_head[:, :w.shape[1]] == h, w, 0.0).astype(bf16),
                                value_of_head(h), dims)

    @pl.when(step == 0)
    def _():
        pad = knew_ref.shape[1]
        knew = knew_ref[0].astype(bf16)
        vnew = vnew_ref[0].astype(bf16)
        head_cols = lambda x, h: x[:, h * HEAD_DIM:(h + 1) * HEAD_DIM]
        z = sum(_dg(qrow_ref[0, h], head_cols(knew, h), _NT) for h in range(H_A)) + bias[:, :pad]
        s_idx = lax.broadcasted_iota(i32, (n_row, pad), 1)
        t_idx = lax.broadcasted_iota(i32, (n_row, pad), 0) % n_new
        mask = s_idx < t_idx
        sp = jnp.where(mask, _softplus2(z), 0.0)
        incl = jnp.zeros_like(sp)
        for j in range(n_new):
            incl = incl + jnp.where(s_idx <= j, sp[:, j:j + 1], 0.0)
        w = jnp.where(mask, jnp.exp2(z - incl), 0.0)
        acc_ref[...] = jnp.zeros_like(acc_ref)
        add_values(w, lambda h: head_cols(vnew, h), _NN)
        carry_ref[...] = jnp.broadcast_to(incl[:, 0:1], carry_ref.shape)

    uo = uo_ref[...]
    head_t = lambda ref, h: ref[0, 0, h].astype(bf16)
    z = [sum(_dg(qrow_ref[0, h], head_t(kp_refs[i], h), _NN) for h in range(H_A)) + bias
         for i in range(pages)]
    sp = [_softplus2(z[i]) for i in range(pages)]
    cr = [jnp.dot(sp[i].astype(bf16), uo, preferred_element_type=f32) for i in range(pages)]
    carry = carry_ref[...]
    for i in range(pages):
        w = jnp.exp2(z[i] - cr[i][:, :PAGE_SIZE] - carry)
        add_values(w, lambda h, i=i: head_t(vp_refs[i], h), _NT)
        carry = carry + cr[i][:, PAGE_SIZE:]
    carry_ref[...] = carry

    @pl.when(step == pl.num_programs(1) - 1)
    def _():
        o_ref[0] = acc_ref[...]


def _attn_sample(q, k_new, v_new, cache_k, cache_v, layer, page_table, sb_bias, n_new):
    b, pad, _ = q.shape
    n_pages = page_table.shape[1]
    pages = PAGES_PER_STEP
    n_row = H_A * n_new
    scale = LOG2E * HEAD_DIM ** -0.5
    qh = jnp.transpose(q[:, :n_new].reshape(b, n_new, H_A, HEAD_DIM) * scale, (0, 2, 1, 3))
    eye = jnp.eye(H_A, dtype=f32)
    qrow = (qh[:, :, None, :, :] * eye[None, :, :, None, None]).reshape(b, H_A, n_row, HEAD_DIM).astype(bf16)
    bias = jnp.repeat(sb_bias * LOG2E, n_new).reshape(n_row, 1)
    j = lax.broadcasted_iota(i32, (PAGE_SIZE, PAGE_SIZE), 0)
    s = lax.broadcasted_iota(i32, (PAGE_SIZE, PAGE_SIZE), 1)
    uo = jnp.concatenate([(j >= s).astype(bf16), jnp.ones((PAGE_SIZE, PAGE_SIZE), bf16)], axis=1)
    cache_k = jnp.transpose(cache_k, (0, 1, 3, 4, 2))
    cache_v = jnp.transpose(cache_v, (0, 1, 3, 4, 2))

    def page_spec(i):
        return pl.BlockSpec(
            (1, 1, H_A, HEAD_DIM, PAGE_SIZE),
            lambda bi, st, pt, i=i: (layer, pt[bi, n_pages - 1 - (st * pages + i)], 0, 0, 0))

    new_spec = pl.BlockSpec((1, pad, W_A), lambda bi, st, pt: (bi, 0, 0))
    grid_spec = pltpu.PrefetchScalarGridSpec(
        num_scalar_prefetch=1,
        grid=(b, n_pages // pages),
        in_specs=[pl.BlockSpec((n_row, 1), lambda bi, st, pt: (0, 0)),
                  pl.BlockSpec((1, H_A, n_row, HEAD_DIM), lambda bi, st, pt: (bi, 0, 0, 0)),
                  new_spec, new_spec,
                  pl.BlockSpec((PAGE_SIZE, 2 * PAGE_SIZE), lambda bi, st, pt: (0, 0))]
                 + [page_spec(i) for i in range(pages)] * 2,
        out_specs=pl.BlockSpec((1, n_row, HEAD_DIM), lambda bi, st, pt: (bi, 0, 0)),
        scratch_shapes=[pltpu.VMEM((n_row, HEAD_DIM), f32), pltpu.VMEM((n_row, PAGE_SIZE), f32)],
    )
    out = pl.pallas_call(
        functools.partial(_attn_sample_kernel, n_new=n_new, pages=pages),
        grid_spec=grid_spec,
        out_shape=SDS((b, n_row, HEAD_DIM), f32),
        compiler_params=_cparams(("arbitrary", "arbitrary")),
        name="attn_sample",
    )(page_table, bias, qrow, k_new, v_new, uo, *([cache_k] * pages), *([cache_v] * pages))
    return jnp.transpose(out.reshape(b, H_A, n_new, HEAD_DIM), (0, 2, 1, 3)).reshape(b, n_new, W_A)


def _prep_kernel(pb_ref, pf_ref, mu_ref, dbase_ref, dup_ref, ibase_ref, iup_ref, gup_ref,
                 kk_ref, ka_ref, rk_ref, hsum_ref,
                 r_out, lw_out, k_out, v_out, a_out, b_out, g_out, bonus_out, *, t_real):
    nb, tt, w = pb_ref.shape
    pb = pb_ref[...]
    tpos = lax.broadcasted_iota(i32, pb.shape, 1)
    prev = jnp.where(tpos == 0, pf_ref[:, 0], pltpu.roll(pb, 1, axis=1))
    x = (pb + (prev - pb) * mu_ref[...]).reshape(nb * tt, w)
    r = x[:, 0:W_B]
    k = x[:, W_B:2 * W_B]
    v = x[:, 2 * W_B:3 * W_B]
    o = 3 * W_B
    xw = x[:, o:o + DECAY_LORA]
    xa = x[:, o + DECAY_LORA:o + DECAY_LORA + ICLR_LORA]
    xg = x[:, o + DECAY_LORA + ICLR_LORA:]
    log_w = -_softplus(-(dbase_ref[...] + _bdot(jnp.tanh(xw), dup_ref[...]))) - 0.5
    lw = -jnp.exp(log_w)
    a = _sigmoid(ibase_ref[...] + _bdot(xa, iup_ref[...]))
    g = _bdot(_sigmoid(xg), gup_ref[...])
    hsum = hsum_ref[...]
    kkf = k * kk_ref[...]
    kk = kkf / jnp.maximum(jnp.sqrt(_split_dot(kkf * kkf, hsum)), 1e-12)
    k2 = k * (1.0 + (a - 1.0) * ka_ref[...])
    bonus = _split_dot(r * k2 * rk_ref[...], hsum) * v
    na = -kk
    kb = kk * a
    if t_real < tt:
        valid = (lax.broadcasted_iota(i32, (nb, tt, W_B), 1) < t_real).reshape(nb * tt, W_B)
        zero = lambda u: jnp.where(valid, u, 0.0)
        r, lw, k2, v, na, kb = zero(r), zero(lw), zero(k2), zero(v), zero(na), zero(kb)
    g_out[...] = g.reshape(nb, tt, W_B)
    bonus_out[...] = bonus.reshape(nb, tt, W_B)
    for val, ref in ((r, r_out), (lw, lw_out), (k2, k_out), (v, v_out), (na, a_out), (kb, b_out)):
        val = val.reshape(nb, tt, W_B)
        for h in range(H_B):
            ref[:, h, :, :] = val[:, :, h * HEAD_DIM:(h + 1) * HEAD_DIM]


def _rwkv_prep(pb, prev_first, p, nb, tt, t_real):
    b, t, w = pb.shape
    row = lambda i, j: (i, j, 0)
    const = lambda i, j: (0, 0)
    vec = lambda a: a.reshape(1, -1)
    head_of = jnp.arange(W_B) // HEAD_DIM
    hsum = (head_of[:, None] == head_of[None, :]).astype(bf16)
    params = [vec(p['mu_shift']), vec(p['decay_base']), p['decay_up'], vec(p['iclr_base']), p['iclr_up'],
              p['gate_up'], vec(p['k_k']), vec(p['k_a']), vec(p['r_k']), hsum]
    heads = pl.BlockSpec((nb, H_B, tt, HEAD_DIM), lambda i, j: (i, 0, j, 0))
    return pl.pallas_call(
        functools.partial(_prep_kernel, t_real=t_real),
        grid=(b // nb, t // tt),
        in_specs=[pl.BlockSpec((nb, tt, w), row),
                  pl.BlockSpec((nb, 1, 1, w), lambda i, j: (i, j, 0, 0))]
                 + [pl.BlockSpec(a.shape, const) for a in params],
        out_specs=[heads] * 6 + [pl.BlockSpec((nb, tt, W_B), row)] * 2,
        out_shape=[SDS((b, H_B, t, HEAD_DIM), f32)] * 6 + [SDS((b, t, W_B), f32)] * 2,
        compiler_params=_cparams(("arbitrary", "arbitrary")),
        name="rwkv_prep",
    )(pb, prev_first, *params)


def _wkv_kernel(r_ref, lw_ref, k_ref, v_ref, a_ref, b_ref, s0_ref, y_ref, s_out, st_ref, *, chunk):
    c = chunk
    n = HEAD_DIM
    step = pl.program_id(1)

    n_chain = lw_ref.shape[0] * H_B
    of = lambda ref, i: ref[i // H_B, i % H_B]

    @pl.when(step == 0)
    def _():
        for i in range(n_chain):
            st_ref[i] = of(s0_ref, i).T

    ti = lax.broadcasted_iota(i32, (c, c), 0)
    si = lax.broadcasted_iota(i32, (c, c), 1)
    tri = (ti >= si).astype(bf16)
    eye = (ti == si).astype(f32)
    ones = jnp.ones((c, n), bf16)
    row2 = lax.broadcasted_iota(i32, (c, 2 * c), 0)
    col2 = lax.broadcasted_iota(i32, (c, 2 * c), 1)
    strict2 = (col2 % c) < row2
    incl2 = (col2 % c) <= row2
    right = col2 >= c
    levels = max(c.bit_length() - 2, 0)
    heads = range(n_chain)
    each = lambda f: [f(h) for h in heads]
    lw = each(lambda h: of(lw_ref, h))
    lw3 = each(lambda h: _split3(lw[h]))
    cum = each(lambda h: sum(_dg(tri, t, _NN) for t in lw3[h]))
    wsum = each(lambda h: sum(_dg(t, ones, _TN) for t in lw3[h]))
    w_in = each(lambda h: jnp.exp(cum[h]))
    w_out = each(lambda h: jnp.exp(-cum[h]))
    at = each(lambda h: of(a_ref, h) * jnp.exp(cum[h] - lw[h]))
    rt = each(lambda h: of(r_ref, h) * w_in[h])
    bk = each(lambda h: jnp.concatenate([of(b_ref, h) * w_out[h], of(k_ref, h) * w_out[h]], axis=0))
    g = each(lambda h: _dot3(jnp.concatenate([at[h], rt[h]], axis=0), bk[h], _NT))
    top = each(lambda h: jnp.where(strict2, g[h][:c], 0.0))
    bot = each(lambda h: jnp.where(incl2, g[h][c:], 0.0))
    pw = each(lambda h: top[h][:, :c])
    inv = each(lambda h: eye + pw[h])
    for _ in range(levels):
        pw = each(lambda h: _dot3(pw[h], pw[h]))
        inv = each(lambda h: inv[h] + _dot3(inv[h], pw[h]))
    vv = each(lambda h: jnp.concatenate([of(v_ref, h), of(v_ref, h)], axis=0))
    xv = each(lambda h: _dot3(jnp.where(right, top[h], 0.0), vv[h]))
    x = each(lambda h: _dot3(at[h], st_ref[h]) + xv[h])
    u = each(lambda h: _dot3(inv[h], x[h]))
    uv = each(lambda h: jnp.concatenate([u[h], of(v_ref, h)], axis=0))
    y = each(lambda h: _bdot(bot[h], uv[h]) + _bdot(rt[h], st_ref[h]))
    st_new = each(lambda h: (st_ref[h] + _dot3(bk[h], uv[h], _TN)) * jnp.exp(wsum[h]))
    for h in heads:
        st_ref[h] = st_new[h]
        mu = jnp.mean(y[h], axis=-1, keepdims=True)
        yc = y[h] - mu
        var = jnp.mean(yc * yc, axis=-1, keepdims=True)
        col0 = (h % H_B) * HEAD_DIM
        y_ref[h // H_B, :, col0:col0 + HEAD_DIM] = yc * lax.rsqrt(var + LNX_EPS)

    @pl.when(step == pl.num_programs(1) - 1)
    def _():
        for i in range(n_chain):
            s_out[i // H_B, i % H_B] = st_ref[i].T


def _wkv_scan(r, lw, k, v, a, b, s0, chunk, nseq):
    bsz, h, t, n = r.shape
    nseq = math.gcd(bsz, nseq)
    seq = pl.BlockSpec((nseq, h, chunk, n), lambda i, j: (i, 0, j, 0))
    state = pl.BlockSpec((nseq, h, n, n), lambda i, j: (i, 0, 0, 0))
    return pl.pallas_call(
        functools.partial(_wkv_kernel, chunk=chunk),
        grid=(bsz // nseq, t // chunk),
        in_specs=[seq] * 6 + [state],
        out_specs=[pl.BlockSpec((nseq, chunk, h * n), lambda i, j: (i, j, 0)), state],
        out_shape=[SDS((bsz, t, h * n), f32), SDS((bsz, h, n, n), f32)],
        scratch_shapes=[pltpu.VMEM((nseq * h, n, n), f32)],
        compiler_params=_cparams(("arbitrary", "arbitrary")),
        name="wkv_scan",
    )(r, lw, k, v, a, b, s0)


def _outproj_kernel(attn_ref, yn_ref, bonus_ref, g_ref, x_ref, gt1_ref, sh2_ref, sc2_ref,
                    gattn_ref, lnxg_ref, lnxb_ref, wout_ref, gpost_ref, gpre_ref, rw_ref, rb_ref,
                    x1_ref, h2_ref, logit_ref):
    nb, tt, d = x_ref.shape
    o_a = _rms(attn_ref[...], gattn_ref[...])
    o_b = (yn_ref[...] * lnxg_ref[...] + lnxb_ref[...] + bonus_ref[...]) * g_ref[...]
    cat = jnp.concatenate([o_a, o_b], axis=-1).reshape(nb * tt, d)
    catb = cat.astype(bf16)
    half = d // 2
    mixed = jnp.concatenate([jnp.dot(catb, wout_ref[:, :half], preferred_element_type=f32),
                             jnp.dot(catb, wout_ref[:, half:], preferred_element_type=f32)],
                            axis=1).reshape(nb, tt, d)
    x1 = x_ref[...] + gt1_ref[...] * _rms(mixed, gpost_ref[...])
    x1_ref[...] = x1
    h2 = _rms(x1, gpre_ref[...]) * (1.0 + sc2_ref[...]) + sh2_ref[...]
    _store_token_tiles(h2_ref, 0, h2.reshape(nb * tt, d))
    logits = _dot3(h2.reshape(nb * tt, d), rw_ref[...]) + rb_ref[...]
    logit_ref[...] = logits.reshape(nb, tt, N_EXPERTS)


def _out_proj(attn, yn, bonus, g, x, gt1, sh2, sc2, p, w_out_bf16, nb, tt):
    b, t, d = x.shape
    row = lambda w: pl.BlockSpec((nb, tt, w), lambda i, j: (i, j, 0))
    mod = pl.BlockSpec((nb, 1, d), lambda i, j: (i, 0, 0))
    vec = lambda a: a.reshape(1, -1)
    params = [vec(p['g_attn_out']), vec(p['lnx_g']), vec(p['lnx_b']), w_out_bf16,
              vec(p['g_mix_post']), vec(p['g_ffn_pre']), p['router_w'], vec(p['router_b'])]
    return pl.pallas_call(
        _outproj_kernel,
        grid=(b // nb, t // tt),
        in_specs=[row(W_A), row(W_B), row(W_B), row(W_B), row(d), mod, mod, mod]
                 + [pl.BlockSpec(a.shape, lambda i, j: (0, 0)) for a in params],
        out_specs=[row(d), pl.BlockSpec((nb, tt, V7X_SUBLANES, V7X_LANES), lambda i, j: (i, j, 0, 0)),
                   row(N_EXPERTS)],
        out_shape=[SDS((b, t, d), f32), SDS((b, t, V7X_SUBLANES, V7X_LANES), f32),
                   SDS((b, t, N_EXPERTS), f32)],
        compiler_params=_cparams(("arbitrary", "arbitrary")),
        name="out_proj",
    )(attn, yn, bonus, g, x, gt1, sh2, sc2, *params)


def _route_kernel(logit_ref, idx_ref, gate_ref, rank_ref, count_ref, base_ref):
    tile = logit_ref.shape[0]

    @pl.when(pl.program_id(0) == 0)
    def _():
        base_ref[...] = jnp.zeros_like(base_ref)

    lane = lax.broadcasted_iota(i32, (tile, N_EXPERTS), 1).astype(f32)
    cur = logit_ref[...]
    hots, vals, idxs = [], [], []
    for _ in range(TOP_K):
        top = jnp.max(cur, axis=-1, keepdims=True)
        idx = jnp.min(jnp.where(cur == top, lane, float(N_EXPERTS)), axis=-1, keepdims=True)
        hot = lane == idx
        hots.append(hot)
        vals.append(top)
        idxs.append(idx)
        cur = jnp.where(hot, -jnp.inf, cur)
    es = [jnp.exp(v - vals[0]) for v in vals]
    total = sum(es)
    chosen = sum(h.astype(f32) for h in hots)
    ti = lax.broadcasted_iota(i32, (tile, tile), 0)
    si = lax.broadcasted_iota(i32, (tile, tile), 1)
    earlier = jnp.dot((ti > si).astype(bf16), chosen.astype(bf16), preferred_element_type=f32) + base_ref[...]
    ranks = [jnp.sum(jnp.where(h, earlier, 0.0), axis=-1, keepdims=True) for h in hots]
    base_ref[...] += jnp.sum(chosen, axis=0, keepdims=True)
    count_ref[...] = base_ref[...]
    col = lax.broadcasted_iota(i32, (tile, TOP_K), 1)
    pick = lambda parts: sum(jnp.where(col == k, parts[k], 0.0) for k in range(TOP_K))
    idx_ref[...] = pick(idxs).astype(i32)
    gate_ref[...] = pick([e / total for e in es])
    rank_ref[...] = pick(ranks).astype(i32)


def _route(logits, tile):
    m = logits.shape[0]
    tok = lambda dt: SDS((m, TOP_K), dt)
    blk = pl.BlockSpec((tile, TOP_K), lambda i: (i, 0))
    return pl.pallas_call(
        _route_kernel,
        grid=(m // tile,),
        in_specs=[pl.BlockSpec((tile, N_EXPERTS), lambda i: (i, 0))],
        out_specs=[blk, blk, blk, pl.BlockSpec((1, N_EXPERTS), lambda i: (0, 0))],
        out_shape=[tok(i32), tok(f32), tok(i32), SDS((1, N_EXPERTS), f32)],
        scratch_shapes=[pltpu.VMEM((1, N_EXPERTS), f32)],
        compiler_params=_cparams(("arbitrary",)),
        name="moe_route",
    )(logits)


def _row_copy(src, src_row, dst, dst_row, sem):
    return pltpu.make_async_copy(src.at[pl.ds(src_row, 1)], dst.at[pl.ds(dst_row, 1)], sem)


def _dispatch_kernel(dest_ref, xa_ref, xb_ref, xs_in, xs_out, sem, *, n_first):
    del xs_in
    tile = xa_ref.shape[0]

    def scatter_rows(x_ref):
        def start(r, _):
            for k in range(TOP_K):
                _row_copy(x_ref, r, xs_out, dest_ref[k, r], sem).start(priority=k % 2)
            return 0

        lax.fori_loop(0, tile, start, 0, unroll=DMA_ISSUE_UNROLL)
        for k in range(TOP_K):
            pltpu.make_async_copy(x_ref, xs_out.at[pl.ds(0, tile)], sem).wait()

    @pl.when(pl.program_id(0) < n_first)
    def _():
        scatter_rows(xa_ref)

    @pl.when(pl.program_id(0) >= n_first)
    def _():
        scatter_rows(xb_ref)


def _dispatch(xa, xb, dest_t, n_slots, tile):
    row_tile = xa.shape[1:]
    n_first = xa.shape[0] // tile
    return pl.pallas_call(
        functools.partial(_dispatch_kernel, n_first=n_first),
        grid=(n_first + xb.shape[0] // tile,),
        in_specs=[pl.BlockSpec((TOP_K, tile), lambda i: (0, i), memory_space=pltpu.SMEM),
                  pl.BlockSpec((tile,) + row_tile, lambda i: (jnp.minimum(i, n_first - 1), 0, 0)),
                  pl.BlockSpec((tile,) + row_tile, lambda i: (jnp.maximum(i - n_first, 0), 0, 0)),
                  pl.BlockSpec(memory_space=pl.ANY)],
        out_specs=pl.BlockSpec(memory_space=pl.ANY),
        out_shape=SDS((n_slots,) + row_tile, xa.dtype),
        scratch_shapes=[pltpu.SemaphoreType.DMA(())],
        input_output_aliases={3: 0},
        compiler_params=_cparams(("arbitrary",)),
        name="moe_dispatch",
    )(dest_t, xa, xb, jnp.zeros((n_slots,) + row_tile, xa.dtype))


def _combine_kernel(dest_ref, dest_next_ref, gate_ref, ys_hbm, x1a_ref, gt2a_ref, x1b_ref, gt2b_ref, g_ref,
                    oa_ref, ob_ref, buf_ref, sems, *, n_first):
    i = pl.program_id(0)
    n = pl.num_programs(0)
    tile = oa_ref.shape[0]

    def issue(dref, slot):
        def start(r, _):
            for k in range(TOP_K):
                _row_copy(ys_hbm, dref[k, r], buf_ref.at[slot, k], r, sems.at[slot]).start(priority=k % 2)
            return 0

        lax.fori_loop(0, tile, start, 0, unroll=DMA_ISSUE_UNROLL)

    def finish(slot):
        @pl.when(i + 1 < n)
        def _():
            issue(dest_next_ref, 1 - slot)

        for k in range(TOP_K):
            pltpu.make_async_copy(ys_hbm.at[pl.ds(0, tile)], buf_ref.at[slot, k], sems.at[slot]).wait()
        gate = gate_ref[...]
        f = sum(gate[:, k:k + 1] * _load_token_tiles(buf_ref, (slot * TOP_K + k) * tile, tile)
                for k in range(TOP_K))
        nf = _rms(f, g_ref[...])

        @pl.when(i < n_first)
        def _():
            oa_ref[...] = x1a_ref[...] + gt2a_ref[0] * nf

        @pl.when(i >= n_first)
        def _():
            ob_ref[...] = x1b_ref[...] + gt2b_ref[...] * nf

    @pl.when(i == 0)
    def _():
        issue(dest_ref, 0)

    @pl.when(i % 2 == 0)
    def _():
        finish(0)

    @pl.when(i % 2 == 1)
    def _():
        finish(1)


def _combine(ys, dest_t, gate, x1_a, gt2_a, x1_b, gt2_b, g_post, tile):
    m = gate.shape[0]
    m_a, d = x1_a.shape
    n_first = m_a // tile
    n_tiles = m // tile
    tiles_per_seq = m_a // gt2_a.shape[0] // tile
    first = lambda i: jnp.minimum(i, n_first - 1)
    second = lambda i: jnp.maximum(i - n_first, 0)
    return pl.pallas_call(
        functools.partial(_combine_kernel, n_first=n_first),
        grid=(n_tiles,),
        in_specs=[pl.BlockSpec((TOP_K, tile), lambda i: (0, i), memory_space=pltpu.SMEM),
                  pl.BlockSpec((TOP_K, tile), lambda i: (0, jnp.minimum(i + 1, n_tiles - 1)),
                               memory_space=pltpu.SMEM),
                  pl.BlockSpec((tile, TOP_K), lambda i: (i, 0)),
                  pl.BlockSpec(memory_space=pl.ANY),
                  pl.BlockSpec((tile, d), lambda i: (first(i), 0)),
                  pl.BlockSpec((1, 1, d), lambda i: (first(i) // tiles_per_seq, 0, 0)),
                  pl.BlockSpec((tile, d), lambda i: (second(i), 0)),
                  pl.BlockSpec((tile, d), lambda i: (second(i), 0)),
                  pl.BlockSpec((1, d), lambda i: (0, 0))],
        out_specs=[pl.BlockSpec((tile, d), lambda i: (first(i), 0)),
                   pl.BlockSpec((tile, d), lambda i: (second(i), 0))],
        out_shape=[SDS((m_a, d), f32), SDS((m - m_a, d), f32)],
        scratch_shapes=[pltpu.VMEM((2, TOP_K, tile) + ys.shape[1:], f32), pltpu.SemaphoreType.DMA((2,))],
        compiler_params=_cparams(("arbitrary",)),
        name="moe_combine",
    )(dest_t, dest_t, gate, ys, x1_a, gt2_a, x1_b, gt2_b, g_post.reshape(1, d))


def _expert_kernel(be_ref, nused_ref, x_ref, w1_ref, b1_ref, w2_ref, b2_ref, y_ref, w1b_ref, w2b_ref):
    i = pl.program_id(0)
    prev = be_ref[jnp.maximum(i - 1, 0)]

    @pl.when((i == 0) | (be_ref[i] != prev))
    def _():
        w1b_ref[...] = w1_ref[0].astype(bf16)
        w2b_ref[...] = w2_ref[0].astype(bf16)

    @pl.when(i < nused_ref[0])
    def _():
        x = _load_token_tiles(x_ref, 0, x_ref.shape[0]).astype(bf16)
        u = jnp.dot(x, w1b_ref[...], preferred_element_type=f32) + b1_ref[0]
        u_glu = jnp.minimum(u[:, :D_FF], SWIGLU_LIMIT)
        u_lin = jnp.clip(u[:, D_FF:], -SWIGLU_LIMIT, SWIGLU_LIMIT)
        act = u_glu * _sigmoid(SWIGLU_ALPHA * u_glu) * (u_lin + 1.0)
        y = jnp.dot(act.astype(bf16), w2b_ref[...], preferred_element_type=f32) + b2_ref[0]
        _store_token_tiles(y_ref, 0, y)

    @pl.when(i >= nused_ref[0])
    def _():
        y_ref[...] = jnp.zeros_like(y_ref)


def _expert_ffn(xs, blk_expert, n_used, e_w1, e_b1, e_w2, e_b2, tm):
    n_slots = xs.shape[0]
    row_tile = xs.shape[1:]
    d = math.prod(row_tile)
    n_blocks = n_slots // tm
    rows = pl.BlockSpec((tm,) + row_tile, lambda i, be, nu: (i, 0, 0))
    ex = lambda i, be, nu: (be[i], 0, 0)
    grid_spec = pltpu.PrefetchScalarGridSpec(
        num_scalar_prefetch=2,
        grid=(n_blocks,),
        in_specs=[rows,
                  pl.BlockSpec((1, d, 2 * D_FF), ex),
                  pl.BlockSpec((1, 1, 2 * D_FF), ex),
                  pl.BlockSpec((1, D_FF, d), ex),
                  pl.BlockSpec((1, 1, d), ex)],
        out_specs=rows,
        scratch_shapes=[pltpu.VMEM((d, 2 * D_FF), bf16), pltpu.VMEM((D_FF, d), bf16)],
    )
    return pl.pallas_call(
        _expert_kernel,
        grid_spec=grid_spec,
        out_shape=SDS((n_slots,) + row_tile, f32),
        compiler_params=_cparams(("arbitrary",)),
        name="expert_ffn",
    )(blk_expert, n_used, xs, e_w1, e_b1.reshape(N_EXPERTS, 1, -1), e_w2, e_b2.reshape(N_EXPERTS, 1, -1))


def _moe_residual(h2_a, logits_a, x1_a, gt2_a, h2_b, logits_b, x1_b, gt2_b, g_post, e_w1, e_b1, e_w2, e_b2, tm):
    tile = MOE_TOKEN_TILE
    m_a, m_b = h2_a.shape[0], h2_b.shape[0]
    assert m_a % tile == 0
    extra = -m_b % tile
    h2_b = jnp.pad(h2_b, ((0, extra), (0, 0), (0, 0)))
    x1_b = jnp.pad(x1_b, ((0, extra), (0, 0)))
    gt2_b = jnp.pad(gt2_b, ((0, extra), (0, 0)))
    logits = jnp.concatenate([logits_a, logits_b, jnp.zeros((extra, N_EXPERTS), f32)], axis=0)
    m = m_a + m_b + extra
    route_extra = -m % MOE_ROUTE_TILE
    top_idx, gate, rank, counts = _route(jnp.pad(logits, ((0, route_extra), (0, 0))), MOE_ROUTE_TILE)
    top_idx, gate, rank = top_idx[:m], gate[:m], rank[:m]
    counts = counts.reshape(N_EXPERTS).astype(i32)
    padded = (counts + tm - 1) // tm * tm
    pad_end = jnp.cumsum(padded)
    pad_start = pad_end - padded
    n_blocks = ((m + route_extra) * TOP_K + N_EXPERTS * (tm - 1) + tm - 1) // tm
    blk_row = (jnp.arange(n_blocks) * tm)[:, None]
    blk_expert = jnp.minimum(jnp.sum(pad_end[None, :] <= blk_row, axis=1), N_EXPERTS - 1).astype(i32)
    n_used = (pad_end[-1] // tm).astype(i32).reshape(1)
    group_start = jnp.sum(jnp.where(top_idx[:, :, None] == jnp.arange(N_EXPERTS), pad_start, 0), axis=-1)
    dest_t = (group_start + rank).astype(i32).T
    xs = _dispatch(h2_a, h2_b, dest_t, n_blocks * tm, tile)
    ys = _expert_ffn(xs, blk_expert, n_used, e_w1, e_b1, e_w2, e_b2, tm)
    out_a, out_b = _combine(ys, dest_t, gate, x1_a, gt2_a, x1_b, gt2_b, g_post, tile)
    return out_a, out_b[:m_b]


def _mix_layer(x, mod, attend, shift_prev, wkv_prev, p, w_in_bf16, w_out_bf16, nb, tt, t_real, chunk, wkv_nseq):
    b, t, d = x.shape
    sh1, sc1, gt1, sh2, sc2, gt2 = [mod[:, i:i + 1, :] for i in range(6)]
    q, k, v, pb, last = _in_proj(x, sh1, sc1, p['g_mix_pre'], w_in_bf16, nb, tt, t_real)
    attn = attend(q, k, v)
    prev_first = jnp.concatenate([shift_prev[:, None, None, :], last[:, :-1]], axis=1)
    r, lw, k2, v2, a, kb, g, bonus = _rwkv_prep(pb, prev_first, p, nb, tt, t_real)
    yn, wkv_new = _wkv_scan(r, lw, k2, v2, a, kb, wkv_prev, chunk, wkv_nseq)
    x1, h2, logits = _out_proj(attn, yn, bonus, g, x, gt1, sh2, sc2, p, w_out_bf16, nb, tt)
    return x1, h2, logits, gt2, k, v, wkv_new, last[:, -1, 0]


def kernel(x_prompt, x_sample, cache_k, cache_v, state_wkv, state_shift, page_table, c_prompt, c_sample,
           w_ada, b_ada, g_mix_pre, g_mix_post, g_ffn_pre, g_ffn_post, w_in, mu_shift,
           decay_base, decay_up, iclr_base, iclr_up, gate_up, k_k, k_a, r_k, lnx_g, lnx_b,
           g_attn_out, sb_bias, w_out, router_w, router_b, e_w1, e_b1, e_w2, e_b2):
    weights = dict(w_ada=w_ada, b_ada=b_ada, g_mix_pre=g_mix_pre, g_mix_post=g_mix_post,
                   g_ffn_pre=g_ffn_pre, g_ffn_post=g_ffn_post, w_in=w_in, mu_shift=mu_shift,
                   decay_base=decay_base, decay_up=decay_up, iclr_base=iclr_base, iclr_up=iclr_up,
                   gate_up=gate_up, k_k=k_k, k_a=k_a, r_k=r_k, lnx_g=lnx_g, lnx_b=lnx_b,
                   g_attn_out=g_attn_out, sb_bias=sb_bias, w_out=w_out, router_w=router_w,
                   router_b=router_b, e_w1=e_w1, e_b1=e_b1, e_w2=e_w2, e_b2=e_b2)
    depth = w_ada.shape[0]
    bp, tp, d = x_prompt.shape
    bs, ts, _ = x_sample.shape
    ts_pad = -(-ts // V7X_SUBLANES) * V7X_SUBLANES
    hp = x_prompt
    hs = jnp.pad(x_sample, ((0, 0), (0, ts_pad - ts), (0, 0)))
    outs = [[] for _ in range(8)]
    for l in range(depth):
        p = {name: w[l] for name, w in weights.items()}
        w_in_bf16 = p['w_in'].astype(bf16)
        w_out_bf16 = p['w_out'].astype(bf16)
        mod = _ada_mod(jnp.concatenate([c_prompt, c_sample], axis=0), p['w_ada'], p['b_ada'])
        mod = mod.reshape(bp + bs, 6, d)

        attend_p = lambda q, k, v: _attn_prompt(q, k, v, p['sb_bias'])
        x1p, h2p, lgp, gt2p, kp, vp, wp, shp = _mix_layer(
            hp, mod[:bp], attend_p, jnp.zeros((bp, SHIFT_W), f32),
            jnp.zeros((bp, H_B, HEAD_DIM, HEAD_DIM), f32), p, w_in_bf16, w_out_bf16,
            nb=1, tt=ROW_TILE, t_real=tp, chunk=WKV_CHUNK, wkv_nseq=WKV_NSEQ)

        attend_s = lambda q, k, v: jnp.pad(
            _attn_sample(q, k, v, cache_k, cache_v, l, page_table, p['sb_bias'], ts),
            ((0, 0), (0, ts_pad - ts), (0, 0)))
        x1s, h2s, lgs, gt2s, ks, vs, ws, shs = _mix_layer(
            hs, mod[bp:], attend_s, state_shift[l], state_wkv[l], p, w_in_bf16, w_out_bf16,
            nb=bs, tt=ts_pad, t_real=ts, chunk=ts_pad, wkv_nseq=WKV_NSEQ)

        mp = bp * tp
        ms = bs * ts
        yp, ys = _moe_residual(
            h2p.reshape((mp,) + h2p.shape[2:]), lgp.reshape(mp, N_EXPERTS), x1p.reshape(mp, d), gt2p,
            h2s[:, :ts].reshape((ms,) + h2s.shape[2:]), lgs[:, :ts].reshape(ms, N_EXPERTS),
            x1s[:, :ts].reshape(ms, d), jnp.broadcast_to(gt2s, (bs, ts, d)).reshape(ms, d),
            p['g_ffn_post'], p['e_w1'], p['e_b1'], p['e_w2'], p['e_b2'], MOE_TILE)
        hp = yp.reshape(bp, tp, d)
        hs = jnp.pad(ys.reshape(bs, ts, d), ((0, 0), (0, ts_pad - ts), (0, 0)))

        for lst, val in zip(outs, (kp.reshape(bp, tp, H_A, HEAD_DIM), vp.reshape(bp, tp, H_A, HEAD_DIM),
                                   ks[:, :ts].reshape(bs, ts, H_A, HEAD_DIM),
                                   vs[:, :ts].reshape(bs, ts, H_A, HEAD_DIM), wp, ws, shp, shs)):
            lst.append(val)
    return (hp, hs[:, :ts]) + tuple(jnp.stack(lst) for lst in outs)
```

```python
import functools
import math

import jax
import jax.numpy as jnp
from jax import lax
from jax.experimental import pallas as pl
from jax.experimental.pallas import tpu as pltpu

f32 = jnp.float32
bf16 = jnp.bfloat16
i32 = jnp.int32
SDS = jax.ShapeDtypeStruct

D_MODEL = 1024
HEAD_DIM = 64
W_A = 512
W_B = 512
H_A = W_A // HEAD_DIM
H_B = W_B // HEAD_DIM
DECAY_LORA = 64
ICLR_LORA = 64
GATE_LORA = 128
SHIFT_W = 3 * W_B + DECAY_LORA + ICLR_LORA + GATE_LORA
N_EXPERTS = 32
TOP_K = 4
D_FF = D_MODEL
SWIGLU_ALPHA = 1.702
SWIGLU_LIMIT = 7.0
PAGE_SIZE = 128
RMS_EPS = 1e-6
LNX_EPS = 64e-5
LOG2E = 1.4426950408889634

V7X_SUBLANES = 8
V7X_LANES = 128
VMEM_LIMIT = 56 * 1024 * 1024

ROW_TILE = 256
WKV_CHUNK = 64
WKV_NSEQ = 4
ATTN_TQ = 512
ATTN_TK = 128
ATTN_NSUB = 4
ATTN_DIAG_BLOCKS = 2
PAGES_PER_STEP = 16
MOE_TILE = 256
MOE_TOKEN_TILE = 256
MOE_ROUTE_TILE = 256
DMA_ISSUE_UNROLL = 8


def _cparams(sem):
    return pltpu.CompilerParams(dimension_semantics=sem, vmem_limit_bytes=VMEM_LIMIT)


def _bdot(a, b):
    return jnp.dot(a.astype(bf16), b.astype(bf16), preferred_element_type=f32)


def _fdot(a, b, dims=(((1,), (0,)), ((), ()))):
    return lax.dot_general(a, b, dims, precision=lax.Precision.HIGHEST, preferred_element_type=f32)


_NT = (((1,), (1,)), ((), ()))
_TN = (((0,), (0,)), ((), ()))


def _split_dot(x, m_bf16):
    hi = x.astype(bf16)
    lo = (x - hi.astype(f32)).astype(bf16)
    return (jnp.dot(hi, m_bf16, preferred_element_type=f32)
            + jnp.dot(lo, m_bf16, preferred_element_type=f32))


def _sigmoid(x):
    return 1.0 / (1.0 + jnp.exp(-x))


def _softplus(x):
    return jnp.maximum(x, 0.0) + jnp.log(1.0 + jnp.exp(-jnp.abs(x)))


def _softplus2(z):
    return jnp.maximum(z, 0.0) + jnp.log2(1.0 + jnp.exp2(-jnp.abs(z)))


def _split2(x):
    hi = x.astype(bf16)
    return hi, (x - hi.astype(f32)).astype(bf16)


def _split3(x):
    hi = x.astype(bf16)
    r = x - hi.astype(f32)
    mid = r.astype(bf16)
    return hi, mid, (r - mid.astype(f32)).astype(bf16)


def _dg(a, b, dims):
    return lax.dot_general(a, b, dims, preferred_element_type=f32)


_NN = (((1,), (0,)), ((), ()))


def _dot3(a, b, dims=_NN):
    ah, al = _split2(a)
    bh, bl = _split2(b)
    return _dg(ah, bh, dims) + _dg(ah, bl, dims) + _dg(al, bh, dims)


def _dot_exact_rhs(x3, m_bf16, dims=_NN):
    return _dg(x3[0], m_bf16, dims) + _dg(x3[1], m_bf16, dims) + _dg(x3[2], m_bf16, dims)


def _rms(x, g):
    return x * lax.rsqrt(jnp.mean(x * x, axis=-1, keepdims=True) + RMS_EPS) * g


def _store_token_tiles(ref, first_row, x2d):
    rows = x2d.shape[0]
    flat = ref.reshape(math.prod(ref.shape) // V7X_LANES, V7X_LANES)
    for c in range(V7X_SUBLANES):
        flat[pl.ds(first_row * V7X_SUBLANES + c, rows, stride=V7X_SUBLANES), :] = (
            x2d[:, c * V7X_LANES:(c + 1) * V7X_LANES])


def _load_token_tiles(ref, first_row, rows):
    flat = ref.reshape(math.prod(ref.shape) // V7X_LANES, V7X_LANES)
    return jnp.concatenate([flat[pl.ds(first_row * V7X_SUBLANES + c, rows, stride=V7X_SUBLANES), :]
                            for c in range(V7X_SUBLANES)], axis=1)


def _ada_kernel(c_ref, w_ref, b_ref, o_ref):
    c = c_ref[...]
    o_ref[...] = _bdot(c * _sigmoid(c), w_ref[...]) + b_ref[...]


def _ada_mod(c, w_ada, b_ada):
    n, d = c.shape
    nout = w_ada.shape[1]
    tn = 1536
    return pl.pallas_call(
        _ada_kernel,
        grid=(nout // tn,),
        in_specs=[pl.BlockSpec((n, d), lambda j: (0, 0)),
                  pl.BlockSpec((d, tn), lambda j: (0, j)),
                  pl.BlockSpec((1, tn), lambda j: (0, j))],
        out_specs=pl.BlockSpec((n, tn), lambda j: (0, j)),
        out_shape=SDS((n, nout), f32),
        compiler_params=_cparams(("arbitrary",)),
        name="ada_mod",
    )(c, w_ada, b_ada.reshape(1, nout))


def _inproj_kernel(x_ref, sh_ref, sc_ref, g_ref, w_ref, q_ref, k_ref, v_ref, pb_ref, last_ref, *, t_last):
    nb, tt, d = x_ref.shape
    h = _rms(x_ref[...], g_ref[...]) * (1.0 + sc_ref[...]) + sh_ref[...]
    hb = h.reshape(nb * tt, d).astype(bf16)
    q_ref[...] = jnp.dot(hb, w_ref[:, 0:W_A], preferred_element_type=f32).reshape(nb, tt, W_A)
    k_ref[...] = jnp.dot(hb, w_ref[:, W_A:2 * W_A], preferred_element_type=f32).reshape(nb, tt, W_A)
    v_ref[...] = jnp.dot(hb, w_ref[:, 2 * W_A:3 * W_A], preferred_element_type=f32).reshape(nb, tt, W_A)
    pb = jnp.dot(hb, w_ref[:, 3 * W_A:], preferred_element_type=f32).reshape(nb, tt, SHIFT_W)
    pb_ref[...] = pb
    last_ref[:, 0] = pb[:, t_last:t_last + 1, :]


def _in_proj(x, sh1, sc1, g, w_in_bf16, nb, tt, t_real):
    b, t, d = x.shape
    row = lambda w: pl.BlockSpec((nb, tt, w), lambda i, j: (i, j, 0))
    mod = pl.BlockSpec((nb, 1, d), lambda i, j: (i, 0, 0))
    return pl.pallas_call(
        functools.partial(_inproj_kernel, t_last=min(tt, t_real) - 1),
        grid=(b // nb, t // tt),
        in_specs=[row(d), mod, mod,
                  pl.BlockSpec((1, d), lambda i, j: (0, 0)),
                  pl.BlockSpec(w_in_bf16.shape, lambda i, j: (0, 0))],
        out_specs=[row(W_A), row(W_A), row(W_A), row(SHIFT_W),
                   pl.BlockSpec((nb, 1, 1, SHIFT_W), lambda i, j: (i, j, 0, 0))],
        out_shape=[SDS((b, t, W_A), f32)] * 3 + [SDS((b, t, SHIFT_W), f32),
                                                  SDS((b, t // tt, 1, SHIFT_W), f32)],
        compiler_params=_cparams(("arbitrary", "arbitrary")),
        name="in_proj",
    )(x, sh1, sc1, g.reshape(1, d), w_in_bf16)


def _attn_prompt_kernel(bias_ref, q_ref, k_ref, v_ref, u2_ref, o_ref, kbd_ref, vbd_ref, acc_ref,
                        *, tq, tk, nsub):
    hp = pl.program_id(1)
    qi = pl.program_id(2)
    two = 2 * tk
    big = tk * nsub
    ratio = tq // big
    n_tiles = k_ref.shape[1] // tk

    @pl.when(qi == 0)
    def _():
        first = lax.broadcasted_iota(i32, (tk, 2 * HEAD_DIM), 1) < HEAD_DIM

        def build(j, _):
            start = pl.multiple_of(j * tk, tk)
            kt = k_ref[0, pl.ds(start, tk), :]
            vt = v_ref[0, pl.ds(start, tk), :]
            kbd_ref[j, 0:tk, :] = jnp.where(first, kt, 0.0).astype(bf16)
            kbd_ref[j, tk:two, :] = jnp.where(first, 0.0, kt).astype(bf16)
            vbd_ref[j, 0:tk, :] = jnp.where(first, vt, 0.0).astype(bf16)
            vbd_ref[j, tk:two, :] = jnp.where(first, 0.0, vt).astype(bf16)
            return 0

        lax.fori_loop(0, n_tiles, build, 0)

    qb = (q_ref[0] * (LOG2E * HEAD_DIM ** -0.5)).astype(bf16)
    lane2 = lax.broadcasted_iota(i32, (1, two), 1)
    bias2 = jnp.where(lane2 < tk, bias_ref[2 * hp], bias_ref[2 * hp + 1]) * LOG2E
    bias_row = jnp.concatenate([bias2] * nsub, axis=1)
    u2 = u2_ref[...]
    acc_ref[...] = jnp.zeros_like(acc_ref)

    def chunk(tile0, ns, r0, nr, diagonal, carry):
        kb = kbd_ref[pl.ds(tile0, ns)].reshape(ns * two, 2 * HEAD_DIM)
        vb = vbd_ref[pl.ds(tile0, ns)].reshape(ns * two, 2 * HEAD_DIM)
        z = _dg(qb[r0:r0 + nr], kb, _NT) + bias_row[:, :ns * two]
        sp = _softplus2(z)
        if diagonal:
            col = lax.broadcasted_iota(i32, (nr, ns * two), 1)
            key_off = (col // two) * tk + col % tk
            mask = key_off < lax.broadcasted_iota(i32, (nr, ns * two), 0) + r0
            sp = jnp.where(mask, sp, 0.0)
        spb = sp.astype(bf16)
        ws = [None] * ns
        for j in range(ns - 1, -1, -1):
            incl = jnp.dot(spb[:, j * two:(j + 1) * two], u2, preferred_element_type=f32)
            ws[j] = jnp.exp2(z[:, j * two:(j + 1) * two] - incl - carry)
            carry = carry + jnp.concatenate([jnp.broadcast_to(incl[:, 0:1], (nr, tk)),
                                             jnp.broadcast_to(incl[:, tk:tk + 1], (nr, tk))], axis=1)
        w = jnp.concatenate(ws, axis=1)
        if diagonal:
            w = jnp.where(mask, w, 0.0)
        acc_ref[r0:r0 + nr, :] += jnp.dot(w.astype(bf16), vb, preferred_element_type=f32)
        return carry

    assert ratio == 1 and nsub % ATTN_DIAG_BLOCKS == 0
    rows_per = tq // ATTN_DIAG_BLOCKS
    tiles_per = nsub // ATTN_DIAG_BLOCKS
    carry = jnp.concatenate(
        [chunk(qi * nsub, (rb + 1) * tiles_per, rb * rows_per, rows_per, True, jnp.zeros((rows_per, two), f32))
         for rb in range(ATTN_DIAG_BLOCKS)], axis=0)

    def body(n, carry):
        return chunk((qi - 1 - n) * nsub, nsub, 0, tq, False, carry)

    lax.fori_loop(0, qi * ratio, body, carry)
    o_ref[0] = acc_ref[...]


def _attn_prompt(q, k, v, sb_bias):
    b, t, _ = q.shape
    tq, tk, nsub = ATTN_TQ, ATTN_TK, ATTN_NSUB
    j = lax.broadcasted_iota(i32, (2 * tk, 2 * tk), 0)
    s = lax.broadcasted_iota(i32, (2 * tk, 2 * tk), 1)
    u2 = ((j >= s) & ((j // tk) == (s // tk))).astype(bf16)
    seq = pl.BlockSpec((1, t, 2 * HEAD_DIM), lambda bi, hp, qi: (bi, 0, hp))
    return pl.pallas_call(
        functools.partial(_attn_prompt_kernel, tq=tq, tk=tk, nsub=nsub),
        grid=(b, H_A // 2, t // tq),
        in_specs=[pl.BlockSpec(memory_space=pltpu.SMEM),
                  pl.BlockSpec((1, tq, 2 * HEAD_DIM), lambda bi, hp, qi: (bi, qi, hp)),
                  seq, seq,
                  pl.BlockSpec((2 * tk, 2 * tk), lambda bi, hp, qi: (0, 0))],
        out_specs=pl.BlockSpec((1, tq, 2 * HEAD_DIM), lambda bi, hp, qi: (bi, qi, hp)),
        out_shape=SDS((b, t, W_A), f32),
        scratch_shapes=[pltpu.VMEM((t // tk, 2 * tk, 2 * HEAD_DIM), bf16),
                        pltpu.VMEM((t // tk, 2 * tk, 2 * HEAD_DIM), bf16),
                        pltpu.VMEM((tq, 2 * HEAD_DIM), f32)],
        compiler_params=_cparams(("arbitrary", "arbitrary", "arbitrary")),
        name="attn_prompt",
    )(sb_bias, q, k, v, u2)


def _attn_sample_kernel(pt_ref, bias_ref, qrow_ref, knew_ref, vnew_ref, uo_ref, *rest, n_new, pages):
    kp_refs = rest[:pages]
    vp_refs = rest[pages:2 * pages]
    o_ref, acc_ref, carry_ref = rest[2 * pages:]
    step = pl.program_id(1)
    n_row = H_A * n_new
    row_head = lax.broadcasted_iota(i32, (n_row, PAGE_SIZE), 0) // n_new
    bias = jnp.broadcast_to(bias_ref[...], (n_row, PAGE_SIZE))

    def add_values(w, value_of_head, dims):
        for h in range(H_A):
            acc_ref[...] += _dg(jnp.where(row_head[:, :w.shape[1]] == h, w, 0.0).astype(bf16),
                                value_of_head(h), dims)

    @pl.when(step == 0)
    def _():
        pad = knew_ref.shape[1]
        knew = knew_ref[0].astype(bf16)
        vnew = vnew_ref[0].astype(bf16)
        head_cols = lambda x, h: x[:, h * HEAD_DIM:(h + 1) * HEAD_DIM]
        z = sum(_dg(qrow_ref[0, h], head_cols(knew, h), _NT) for h in range(H_A)) + bias[:, :pad]
        s_idx = lax.broadcasted_iota(i32, (n_row, pad), 1)
        t_idx = lax.broadcasted_iota(i32, (n_row, pad), 0) % n_new
        mask = s_idx < t_idx
        sp = jnp.where(mask, _softplus2(z), 0.0)
        incl = jnp.zeros_like(sp)
        for j in range(n_new):
            incl = incl + jnp.where(s_idx <= j, sp[:, j:j + 1], 0.0)
        w = jnp.where(mask, jnp.exp2(z - incl), 0.0)
        acc_ref[...] = jnp.zeros_like(acc_ref)
        add_values(w, lambda h: head_cols(vnew, h), _NN)
        carry_ref[...] = jnp.broadcast_to(incl[:, 0:1], carry_ref.shape)

    uo = uo_ref[...]
    head_t = lambda ref, h: ref[0, 0, h].astype(bf16)
    z = [sum(_dg(qrow_ref[0, h], head_t(kp_refs[i], h), _NN) for h in range(H_A)) + bias
         for i in range(pages)]
    sp = [_softplus2(z[i]) for i in range(pages)]
    cr = [jnp.dot(sp[i].astype(bf16), uo, preferred_element_type=f32) for i in range(pages)]
    carry = carry_ref[...]
    for i in range(pages):
        w = jnp.exp2(z[i] - cr[i][:, :PAGE_SIZE] - carry)
        add_values(w, lambda h, i=i: head_t(vp_refs[i], h), _NT)
        carry = carry + cr[i][:, PAGE_SIZE:]
    carry_ref[...] = carry

    @pl.when(step == pl.num_programs(1) - 1)
    def _():
        o_ref[0] = acc_ref[...]


def _attn_sample(q, k_new, v_new, cache_k, cache_v, layer, page_table, sb_bias, n_new):
    b, pad, _ = q.shape
    n_pages = page_table.shape[1]
    pages = PAGES_PER_STEP
    n_row = H_A * n_new
    scale = LOG2E * HEAD_DIM ** -0.5
    qh = jnp.transpose(q[:, :n_new].reshape(b, n_new, H_A, HEAD_DIM) * scale, (0, 2, 1, 3))
    eye = jnp.eye(H_A, dtype=f32)
    qrow = (qh[:, :, None, :, :] * eye[None, :, :, None, None]).reshape(b, H_A, n_row, HEAD_DIM).astype(bf16)
    bias = jnp.repeat(sb_bias * LOG2E, n_new).reshape(n_row, 1)
    j = lax.broadcasted_iota(i32, (PAGE_SIZE, PAGE_SIZE), 0)
    s = lax.broadcasted_iota(i32, (PAGE_SIZE, PAGE_SIZE), 1)
    uo = jnp.concatenate([(j >= s).astype(bf16), jnp.ones((PAGE_SIZE, PAGE_SIZE), bf16)], axis=1)
    cache_k = jnp.transpose(cache_k, (0, 1, 3, 4, 2))
    cache_v = jnp.transpose(cache_v, (0, 1, 3, 4, 2))

    def page_spec(i):
        return pl.BlockSpec(
            (1, 1, H_A, HEAD_DIM, PAGE_SIZE),
            lambda bi, st, pt, i=i: (layer, pt[bi, n_pages - 1 - (st * pages + i)], 0, 0, 0))

    new_spec = pl.BlockSpec((1, pad, W_A), lambda bi, st, pt: (bi, 0, 0))
    grid_spec = pltpu.PrefetchScalarGridSpec(
        num_scalar_prefetch=1,
        grid=(b, n_pages // pages),
        in_specs=[pl.BlockSpec((n_row, 1), lambda bi, st, pt: (0, 0)),
                  pl.BlockSpec((1, H_A, n_row, HEAD_DIM), lambda bi, st, pt: (bi, 0, 0, 0)),
                  new_spec, new_spec,
                  pl.BlockSpec((PAGE_SIZE, 2 * PAGE_SIZE), lambda bi, st, pt: (0, 0))]
                 + [page_spec(i) for i in range(pages)] * 2,
        out_specs=pl.BlockSpec((1, n_row, HEAD_DIM), lambda bi, st, pt: (bi, 0, 0)),
        scratch_shapes=[pltpu.VMEM((n_row, HEAD_DIM), f32), pltpu.VMEM((n_row, PAGE_SIZE), f32)],
    )
    out = pl.pallas_call(
        functools.partial(_attn_sample_kernel, n_new=n_new, pages=pages),
        grid_spec=grid_spec,
        out_shape=SDS((b, n_row, HEAD_DIM), f32),
        compiler_params=_cparams(("arbitrary", "arbitrary")),
        name="attn_sample",
    )(page_table, bias, qrow, k_new, v_new, uo, *([cache_k] * pages), *([cache_v] * pages))
    return jnp.transpose(out.reshape(b, H_A, n_new, HEAD_DIM), (0, 2, 1, 3)).reshape(b, n_new, W_A)


def _prep_kernel(pb_ref, pf_ref, mu_ref, dbase_ref, dup_ref, ibase_ref, iup_ref, gup_ref,
                 kk_ref, ka_ref, rk_ref, hsum_ref,
                 r_out, lw_out, k_out, v_out, a_out, b_out, g_out, bonus_out, *, t_real):
    nb, tt, w = pb_ref.shape
    pb = pb_ref[...]
    tpos = lax.broadcasted_iota(i32, pb.shape, 1)
    prev = jnp.where(tpos == 0, pf_ref[:, 0], pltpu.roll(pb, 1, axis=1))
    x = (pb + (prev - pb) * mu_ref[...]).reshape(nb * tt, w)
    r = x[:, 0:W_B]
    k = x[:, W_B:2 * W_B]
    v = x[:, 2 * W_B:3 * W_B]
    o = 3 * W_B
    xw = x[:, o:o + DECAY_LORA]
    xa = x[:, o + DECAY_LORA:o + DECAY_LORA + ICLR_LORA]
    xg = x[:, o + DECAY_LORA + ICLR_LORA:]
    log_w = -_softplus(-(dbase_ref[...] + _bdot(jnp.tanh(xw), dup_ref[...]))) - 0.5
    lw = -jnp.exp(log_w)
    a = _sigmoid(ibase_ref[...] + _bdot(xa, iup_ref[...]))
    g = _bdot(_sigmoid(xg), gup_ref[...])
    hsum = hsum_ref[...]
    kkf = k * kk_ref[...]
    kk = kkf / jnp.maximum(jnp.sqrt(_split_dot(kkf * kkf, hsum)), 1e-12)
    k2 = k * (1.0 + (a - 1.0) * ka_ref[...])
    bonus = _split_dot(r * k2 * rk_ref[...], hsum) * v
    na = -kk
    kb = kk * a
    if t_real < tt:
        valid = (lax.broadcasted_iota(i32, (nb, tt, W_B), 1) < t_real).reshape(nb * tt, W_B)
        zero = lambda u: jnp.where(valid, u, 0.0)
        r, lw, k2, v, na, kb = zero(r), zero(lw), zero(k2), zero(v), zero(na), zero(kb)
    g_out[...] = g.reshape(nb, tt, W_B)
    bonus_out[...] = bonus.reshape(nb, tt, W_B)
    for val, ref in ((r, r_out), (lw, lw_out), (k2, k_out), (v, v_out), (na, a_out), (kb, b_out)):
        val = val.reshape(nb, tt, W_B)
        for h in range(H_B):
            ref[:, h, :, :] = val[:, :, h * HEAD_DIM:(h + 1) * HEAD_DIM]


def _rwkv_prep(pb, prev_first, p, nb, tt, t_real):
    b, t, w = pb.shape
    row = lambda i, j: (i, j, 0)
    const = lambda i, j: (0, 0)
    vec = lambda a: a.reshape(1, -1)
    head_of = jnp.arange(W_B) // HEAD_DIM
    hsum = (head_of[:, None] == head_of[None, :]).astype(bf16)
    params = [vec(p['mu_shift']), vec(p['decay_base']), p['decay_up'], vec(p['iclr_base']), p['iclr_up'],
              p['gate_up'], vec(p['k_k']), vec(p['k_a']), vec(p['r_k']), hsum]
    heads = pl.BlockSpec((nb, H_B, tt, HEAD_DIM), lambda i, j: (i, 0, j, 0))
    return pl.pallas_call(
        functools.partial(_prep_kernel, t_real=t_real),
        grid=(b // nb, t // tt),
        in_specs=[pl.BlockSpec((nb, tt, w), row),
                  pl.BlockSpec((nb, 1, 1, w), lambda i, j: (i, j, 0, 0))]
                 + [pl.BlockSpec(a.shape, const) for a in params],
        out_specs=[heads] * 6 + [pl.BlockSpec((nb, tt, W_B), row)] * 2,
        out_shape=[SDS((b, H_B, t, HEAD_DIM), f32)] * 6 + [SDS((b, t, W_B), f32)] * 2,
        compiler_params=_cparams(("arbitrary", "arbitrary")),
        name="rwkv_prep",
    )(pb, prev_first, *params)


def _wkv_kernel(r_ref, lw_ref, k_ref, v_ref, a_ref, b_ref, s0_ref, y_ref, s_out, st_ref, *, chunk):
    c = chunk
    n = HEAD_DIM
    step = pl.program_id(1)

    n_chain = lw_ref.shape[0] * H_B
    of = lambda ref, i: ref[i // H_B, i % H_B]

    @pl.when(step == 0)
    def _():
        for i in range(n_chain):
            st_ref[i] = of(s0_ref, i).T

    ti = lax.broadcasted_iota(i32, (c, c), 0)
    si = lax.broadcasted_iota(i32, (c, c), 1)
    tri = (ti >= si).astype(bf16)
    eye = (ti == si).astype(f32)
    ones = jnp.ones((c, n), bf16)
    row2 = lax.broadcasted_iota(i32, (c, 2 * c), 0)
    col2 = lax.broadcasted_iota(i32, (c, 2 * c), 1)
    strict2 = (col2 % c) < row2
    incl2 = (col2 % c) <= row2
    right = col2 >= c
    levels = max(c.bit_length() - 2, 0)
    heads = range(n_chain)
    each = lambda f: [f(h) for h in heads]
    lw = each(lambda h: of(lw_ref, h))
    lw3 = each(lambda h: _split3(lw[h]))
    cum = each(lambda h: sum(_dg(tri, t, _NN) for t in lw3[h]))
    wsum = each(lambda h: sum(_dg(t, ones, _TN) for t in lw3[h]))
    w_in = each(lambda h: jnp.exp(cum[h]))
    w_out = each(lambda h: jnp.exp(-cum[h]))
    at = each(lambda h: of(a_ref, h) * jnp.exp(cum[h] - lw[h]))
    rt = each(lambda h: of(r_ref, h) * w_in[h])
    bk = each(lambda h: jnp.concatenate([of(b_ref, h) * w_out[h], of(k_ref, h) * w_out[h]], axis=0))
    g = each(lambda h: _dot3(jnp.concatenate([at[h], rt[h]], axis=0), bk[h], _NT))
    top = each(lambda h: jnp.where(strict2, g[h][:c], 0.0))
    bot = each(lambda h: jnp.where(incl2, g[h][c:], 0.0))
    pw = each(lambda h: top[h][:, :c])
    inv = each(lambda h: eye + pw[h])
    for _ in range(levels):
        pw = each(lambda h: _dot3(pw[h], pw[h]))
        inv = each(lambda h: inv[h] + _dot3(inv[h], pw[h]))
    vv = each(lambda h: jnp.concatenate([of(v_ref, h), of(v_ref, h)], axis=0))
    xv = each(lambda h: _dot3(jnp.where(right, top[h], 0.0), vv[h]))
    x = each(lambda h: _dot3(at[h], st_ref[h]) + xv[h])
    u = each(lambda h: _dot3(inv[h], x[h]))
    uv = each(lambda h: jnp.concatenate([u[h], of(v_ref, h)], axis=0))
    y = each(lambda h: _bdot(bot[h], uv[h]) + _bdot(rt[h], st_ref[h]))
    st_new = each(lambda h: (st_ref[h] + _dot3(bk[h], uv[h], _TN)) * jnp.exp(wsum[h]))
    for h in heads:
        st_ref[h] = st_new[h]
        mu = jnp.mean(y[h], axis=-1, keepdims=True)
        yc = y[h] - mu
        var = jnp.mean(yc * yc, axis=-1, keepdims=True)
        col0 = (h % H_B) * HEAD_DIM
        y_ref[h // H_B, :, col0:col0 + HEAD_DIM] = yc * lax.rsqrt(var + LNX_EPS)

    @pl.when(step == pl.num_programs(1) - 1)
    def _():
        for i in range(n_chain):
            s_out[i // H_B, i % H_B] = st_ref[i].T


def _wkv_scan(r, lw, k, v, a, b, s0, chunk, nseq):
    bsz, h, t, n = r.shape
    nseq = math.gcd(bsz, nseq)
    seq = pl.BlockSpec((nseq, h, chunk, n), lambda i, j: (i, 0, j, 0))
    state = pl.BlockSpec((nseq, h, n, n), lambda i, j: (i, 0, 0, 0))
    return pl.pallas_call(
        functools.partial(_wkv_kernel, chunk=chunk),
        grid=(bsz // nseq, t // chunk),
        in_specs=[seq] * 6 + [state],
        out_specs=[pl.BlockSpec((nseq, chunk, h * n), lambda i, j: (i, j, 0)), state],
        out_shape=[SDS((bsz, t, h * n), f32), SDS((bsz, h, n, n), f32)],
        scratch_shapes=[pltpu.VMEM((nseq * h, n, n), f32)],
        compiler_params=_cparams(("arbitrary", "arbitrary")),
        name="wkv_scan",
    )(r, lw, k, v, a, b, s0)


def _outproj_kernel(attn_ref, yn_ref, bonus_ref, g_ref, x_ref, gt1_ref, sh2_ref, sc2_ref,
                    gattn_ref, lnxg_ref, lnxb_ref, wout_ref, gpost_ref, gpre_ref, rw_ref, rb_ref,
                    x1_ref, h2_ref, logit_ref):
    nb, tt, d = x_ref.shape
    o_a = _rms(attn_ref[...], gattn_ref[...])
    o_b = (yn_ref[...] * lnxg_ref[...] + lnxb_ref[...] + bonus_ref[...]) * g_ref[...]
    cat = jnp.concatenate([o_a, o_b], axis=-1).reshape(nb * tt, d)
    catb = cat.astype(bf16)
    half = d // 2
    mixed = jnp.concatenate([jnp.dot(catb, wout_ref[:, :half], preferred_element_type=f32),
                             jnp.dot(catb, wout_ref[:, half:], preferred_element_type=f32)],
                            axis=1).reshape(nb, tt, d)
    x1 = x_ref[...] + gt1_ref[...] * _rms(mixed, gpost_ref[...])
    x1_ref[...] = x1
    h2 = _rms(x1, gpre_ref[...]) * (1.0 + sc2_ref[...]) + sh2_ref[...]
    _store_token_tiles(h2_ref, 0, h2.reshape(nb * tt, d))
    logits = _dot3(h2.reshape(nb * tt, d), rw_ref[...]) + rb_ref[...]
    logit_ref[...] = logits.reshape(nb, tt, N_EXPERTS)


def _out_proj(attn, yn, bonus, g, x, gt1, sh2, sc2, p, w_out_bf16, nb, tt):
    b, t, d = x.shape
    row = lambda w: pl.BlockSpec((nb, tt, w), lambda i, j: (i, j, 0))
    mod = pl.BlockSpec((nb, 1, d), lambda i, j: (i, 0, 0))
    vec = lambda a: a.reshape(1, -1)
    params = [vec(p['g_attn_out']), vec(p['lnx_g']), vec(p['lnx_b']), w_out_bf16,
              vec(p['g_mix_post']), vec(p['g_ffn_pre']), p['router_w'], vec(p['router_b'])]
    return pl.pallas_call(
        _outproj_kernel,
        grid=(b // nb, t // tt),
        in_specs=[row(W_A), row(W_B), row(W_B), row(W_B), row(d), mod, mod, mod]
                 + [pl.BlockSpec(a.shape, lambda i, j: (0, 0)) for a in params],
        out_specs=[row(d), pl.BlockSpec((nb, tt, V7X_SUBLANES, V7X_LANES), lambda i, j: (i, j, 0, 0)),
                   row(N_EXPERTS)],
        out_shape=[SDS((b, t, d), f32), SDS((b, t, V7X_SUBLANES, V7X_LANES), f32),
                   SDS((b, t, N_EXPERTS), f32)],
        compiler_params=_cparams(("arbitrary", "arbitrary")),
        name="out_proj",
    )(attn, yn, bonus, g, x, gt1, sh2, sc2, *params)


def _route_kernel(logit_ref, idx_ref, gate_ref, rank_ref, count_ref, base_ref):
    tile = logit_ref.shape[0]

    @pl.when(pl.program_id(0) == 0)
    def _():
        base_ref[...] = jnp.zeros_like(base_ref)

    lane = lax.broadcasted_iota(i32, (tile, N_EXPERTS), 1).astype(f32)
    cur = logit_ref[...]
    hots, vals, idxs = [], [], []
    for _ in range(TOP_K):
        top = jnp.max(cur, axis=-1, keepdims=True)
        idx = jnp.min(jnp.where(cur == top, lane, float(N_EXPERTS)), axis=-1, keepdims=True)
        hot = lane == idx
        hots.append(hot)
        vals.append(top)
        idxs.append(idx)
        cur = jnp.where(hot, -jnp.inf, cur)
    es = [jnp.exp(v - vals[0]) for v in vals]
    total = sum(es)
    chosen = sum(h.astype(f32) for h in hots)
    ti = lax.broadcasted_iota(i32, (tile, tile), 0)
    si = lax.broadcasted_iota(i32, (tile, tile), 1)
    earlier = jnp.dot((ti > si).astype(bf16), chosen.astype(bf16), preferred_element_type=f32) + base_ref[...]
    ranks = [jnp.sum(jnp.where(h, earlier, 0.0), axis=-1, keepdims=True) for h in hots]
    base_ref[...] += jnp.sum(chosen, axis=0, keepdims=True)
    count_ref[...] = base_ref[...]
    col = lax.broadcasted_iota(i32, (tile, TOP_K), 1)
    pick = lambda parts: sum(jnp.where(col == k, parts[k], 0.0) for k in range(TOP_K))
    idx_ref[...] = pick(idxs).astype(i32)
    gate_ref[...] = pick([e / total for e in es])
    rank_ref[...] = pick(ranks).astype(i32)


def _route(logits, tile):
    m = logits.shape[0]
    tok = lambda dt: SDS((m, TOP_K), dt)
    blk = pl.BlockSpec((tile, TOP_K), lambda i: (i, 0))
    return pl.pallas_call(
        _route_kernel,
        grid=(m // tile,),
        in_specs=[pl.BlockSpec((tile, N_EXPERTS), lambda i: (i, 0))],
        out_specs=[blk, blk, blk, pl.BlockSpec((1, N_EXPERTS), lambda i: (0, 0))],
        out_shape=[tok(i32), tok(f32), tok(i32), SDS((1, N_EXPERTS), f32)],
        scratch_shapes=[pltpu.VMEM((1, N_EXPERTS), f32)],
        compiler_params=_cparams(("arbitrary",)),
        name="moe_route",
    )(logits)


def _row_copy(src, src_row, dst, dst_row, sem):
    return pltpu.make_async_copy(src.at[pl.ds(src_row, 1)], dst.at[pl.ds(dst_row, 1)], sem)


def _dispatch_kernel(dest_ref, xa_ref, xb_ref, xs_in, xs_out, sem, *, n_first):
    del xs_in
    tile = xa_ref.shape[0]

    def scatter_rows(x_ref):
        def start(r, _):
            for k in range(TOP_K):
                _row_copy(x_ref, r, xs_out, dest_ref[k, r], sem).start(priority=k % 2)
            return 0

        lax.fori_loop(0, tile, start, 0, unroll=DMA_ISSUE_UNROLL)
        for k in range(TOP_K):
            pltpu.make_async_copy(x_ref, xs_out.at[pl.ds(0, tile)], sem).wait()

    @pl.when(pl.program_id(0) < n_first)
    def _():
        scatter_rows(xa_ref)

    @pl.when(pl.program_id(0) >= n_first)
    def _():
        scatter_rows(xb_ref)


def _dispatch(xa, xb, dest_t, n_slots, tile):
    row_tile = xa.shape[1:]
    n_first = xa.shape[0] // tile
    return pl.pallas_call(
        functools.partial(_dispatch_kernel, n_first=n_first),
        grid=(n_first + xb.shape[0] // tile,),
        in_specs=[pl.BlockSpec((TOP_K, tile), lambda i: (0, i), memory_space=pltpu.SMEM),
                  pl.BlockSpec((tile,) + row_tile, lambda i: (jnp.minimum(i, n_first - 1), 0, 0)),
                  pl.BlockSpec((tile,) + row_tile, lambda i: (jnp.maximum(i - n_first, 0), 0, 0)),
                  pl.BlockSpec(memory_space=pl.ANY)],
        out_specs=pl.BlockSpec(memory_space=pl.ANY),
        out_shape=SDS((n_slots,) + row_tile, xa.dtype),
        scratch_shapes=[pltpu.SemaphoreType.DMA(())],
        input_output_aliases={3: 0},
        compiler_params=_cparams(("arbitrary",)),
        name="moe_dispatch",
    )(dest_t, xa, xb, jnp.zeros((n_slots,) + row_tile, xa.dtype))


def _combine_kernel(dest_ref, dest_next_ref, gate_ref, ys_hbm, x1a_ref, gt2a_ref, x1b_ref, gt2b_ref, g_ref,
                    oa_ref, ob_ref, buf_ref, sems, *, n_first):
    i = pl.program_id(0)
    n = pl.num_programs(0)
    tile = oa_ref.shape[0]

    def issue(dref, slot):
        def start(r, _):
            for k in range(TOP_K):
                _row_copy(ys_hbm, dref[k, r], buf_ref.at[slot, k], r, sems.at[slot]).start(priority=k % 2)
            return 0

        lax.fori_loop(0, tile, start, 0, unroll=DMA_ISSUE_UNROLL)

    def finish(slot):
        @pl.when(i + 1 < n)
        def _():
            issue(dest_next_ref, 1 - slot)

        for k in range(TOP_K):
            pltpu.make_async_copy(ys_hbm.at[pl.ds(0, tile)], buf_ref.at[slot, k], sems.at[slot]).wait()
        gate = gate_ref[...]
        f = sum(gate[:, k:k + 1] * _load_token_tiles(buf_ref, (slot * TOP_K + k) * tile, tile)
                for k in range(TOP_K))
        nf = _rms(f, g_ref[...])

        @pl.when(i < n_first)
        def _():
            oa_ref[...] = x1a_ref[...] + gt2a_ref[0] * nf

        @pl.when(i >= n_first)
        def _():
            ob_ref[...] = x1b_ref[...] + gt2b_ref[...] * nf

    @pl.when(i == 0)
    def _():
        issue(dest_ref, 0)

    @pl.when(i % 2 == 0)
    def _():
        finish(0)

    @pl.when(i % 2 == 1)
    def _():
        finish(1)


def _combine(ys, dest_t, gate, x1_a, gt2_a, x1_b, gt2_b, g_post, tile):
    m = gate.shape[0]
    m_a, d = x1_a.shape
    n_first = m_a // tile
    n_tiles = m // tile
    tiles_per_seq = m_a // gt2_a.shape[0] // tile
    first = lambda i: jnp.minimum(i, n_first - 1)
    second = lambda i: jnp.maximum(i - n_first, 0)
    return pl.pallas_call(
        functools.partial(_combine_kernel, n_first=n_first),
        grid=(n_tiles,),
        in_specs=[pl.BlockSpec((TOP_K, tile), lambda i: (0, i), memory_space=pltpu.SMEM),
                  pl.BlockSpec((TOP_K, tile), lambda i: (0, jnp.minimum(i + 1, n_tiles - 1)),
                               memory_space=pltpu.SMEM),
                  pl.BlockSpec((tile, TOP_K), lambda i: (i, 0)),
                  pl.BlockSpec(memory_space=pl.ANY),
                  pl.BlockSpec((tile, d), lambda i: (first(i), 0)),
                  pl.BlockSpec((1, 1, d), lambda i: (first(i) // tiles_per_seq, 0, 0)),
                  pl.BlockSpec((tile, d), lambda i: (second(i), 0)),
                  pl.BlockSpec((tile, d), lambda i: (second(i), 0)),
                  pl.BlockSpec((1, d), lambda i: (0, 0))],
        out_specs=[pl.BlockSpec((tile, d), lambda i: (first(i), 0)),
                   pl.BlockSpec((tile, d), lambda i: (second(i), 0))],
        out_shape=[SDS((m_a, d), f32), SDS((m - m_a, d), f32)],
        scratch_shapes=[pltpu.VMEM((2, TOP_K, tile) + ys.shape[1:], f32), pltpu.SemaphoreType.DMA((2,))],
        compiler_params=_cparams(("arbitrary",)),
        name="moe_combine",
    )(dest_t, dest_t, gate, ys, x1_a, gt2_a, x1_b, gt2_b, g_post.reshape(1, d))


def _expert_kernel(be_ref, nused_ref, x_ref, w1_ref, b1_ref, w2_ref, b2_ref, y_ref, w1b_ref, w2b_ref):
    i = pl.program_id(0)
    prev = be_ref[jnp.maximum(i - 1, 0)]

    @pl.when((i == 0) | (be_ref[i] != prev))
    def _():
        w1b_ref[...] = w1_ref[0].astype(bf16)
        w2b_ref[...] = w2_ref[0].astype(bf16)

    @pl.when(i < nused_ref[0])
    def _():
        x = _load_token_tiles(x_ref, 0, x_ref.shape[0]).astype(bf16)
        u = jnp.dot(x, w1b_ref[...], preferred_element_type=f32) + b1_ref[0]
        u_glu = jnp.minimum(u[:, :D_FF], SWIGLU_LIMIT)
        u_lin = jnp.clip(u[:, D_FF:], -SWIGLU_LIMIT, SWIGLU_LIMIT)
        act = u_glu * _sigmoid(SWIGLU_ALPHA * u_glu) * (u_lin + 1.0)
        y = jnp.dot(act.astype(bf16), w2b_ref[...], preferred_element_type=f32) + b2_ref[0]
        _store_token_tiles(y_ref, 0, y)

    @pl.when(i >= nused_ref[0])
    def _():
        y_ref[...] = jnp.zeros_like(y_ref)


def _expert_ffn(xs, blk_expert, n_used, e_w1, e_b1, e_w2, e_b2, tm):
    n_slots = xs.shape[0]
    row_tile = xs.shape[1:]
    d = math.prod(row_tile)
    n_blocks = n_slots // tm
    rows = pl.BlockSpec((tm,) + row_tile, lambda i, be, nu: (i, 0, 0))
    ex = lambda i, be, nu: (be[i], 0, 0)
    grid_spec = pltpu.PrefetchScalarGridSpec(
        num_scalar_prefetch=2,
        grid=(n_blocks,),
        in_specs=[rows,
                  pl.BlockSpec((1, d, 2 * D_FF), ex),
                  pl.BlockSpec((1, 1, 2 * D_FF), ex),
                  pl.BlockSpec((1, D_FF, d), ex),
                  pl.BlockSpec((1, 1, d), ex)],
        out_specs=rows,
        scratch_shapes=[pltpu.VMEM((d, 2 * D_FF), bf16), pltpu.VMEM((D_FF, d), bf16)],
    )
    return pl.pallas_call(
        _expert_kernel,
        grid_spec=grid_spec,
        out_shape=SDS((n_slots,) + row_tile, f32),
        compiler_params=_cparams(("arbitrary",)),
        name="expert_ffn",
    )(blk_expert, n_used, xs, e_w1, e_b1.reshape(N_EXPERTS, 1, -1), e_w2, e_b2.reshape(N_EXPERTS, 1, -1))


def _moe_residual(h2_a, logits_a, x1_a, gt2_a, h2_b, logits_b, x1_b, gt2_b, g_post, e_w1, e_b1, e_w2, e_b2, tm):
    tile = MOE_TOKEN_TILE
    m_a, m_b = h2_a.shape[0], h2_b.shape[0]
    assert m_a % tile == 0
    extra = -m_b % tile
    h2_b = jnp.pad(h2_b, ((0, extra), (0, 0), (0, 0)))
    x1_b = jnp.pad(x1_b, ((0, extra), (0, 0)))
    gt2_b = jnp.pad(gt2_b, ((0, extra), (0, 0)))
    logits = jnp.concatenate([logits_a, logits_b, jnp.zeros((extra, N_EXPERTS), f32)], axis=0)
    m = m_a + m_b + extra
    route_extra = -m % MOE_ROUTE_TILE
    top_idx, gate, rank, counts = _route(jnp.pad(logits, ((0, route_extra), (0, 0))), MOE_ROUTE_TILE)
    top_idx, gate, rank = top_idx[:m], gate[:m], rank[:m]
    counts = counts.reshape(N_EXPERTS).astype(i32)
    padded = (counts + tm - 1) // tm * tm
    pad_end = jnp.cumsum(padded)
    pad_start = pad_end - padded
    n_blocks = ((m + route_extra) * TOP_K + N_EXPERTS * (tm - 1) + tm - 1) // tm
    blk_row = (jnp.arange(n_blocks) * tm)[:, None]
    blk_expert = jnp.minimum(jnp.sum(pad_end[None, :] <= blk_row, axis=1), N_EXPERTS - 1).astype(i32)
    n_used = (pad_end[-1] // tm).astype(i32).reshape(1)
    group_start = jnp.sum(jnp.where(top_idx[:, :, None] == jnp.arange(N_EXPERTS), pad_start, 0), axis=-1)
    dest_t = (group_start + rank).astype(i32).T
    xs = _dispatch(h2_a, h2_b, dest_t, n_blocks * tm, tile)
    ys = _expert_ffn(xs, blk_expert, n_used, e_w1, e_b1, e_w2, e_b2, tm)
    out_a, out_b = _combine(ys, dest_t, gate, x1_a, gt2_a, x1_b, gt2_b, g_post, tile)
    return out_a, out_b[:m_b]


def _mix_layer(x, mod, attend, shift_prev, wkv_prev, p, w_in_bf16, w_out_bf16, nb, tt, t_real, chunk, wkv_nseq):
    b, t, d = x.shape
    sh1, sc1, gt1, sh2, sc2, gt2 = [mod[:, i:i + 1, :] for i in range(6)]
    q, k, v, pb, last = _in_proj(x, sh1, sc1, p['g_mix_pre'], w_in_bf16, nb, tt, t_real)
    attn = attend(q, k, v)
    prev_first = jnp.concatenate([shift_prev[:, None, None, :], last[:, :-1]], axis=1)
    r, lw, k2, v2, a, kb, g, bonus = _rwkv_prep(pb, prev_first, p, nb, tt, t_real)
    yn, wkv_new = _wkv_scan(r, lw, k2, v2, a, kb, wkv_prev, chunk, wkv_nseq)
    x1, h2, logits = _out_proj(attn, yn, bonus, g, x, gt1, sh2, sc2, p, w_out_bf16, nb, tt)
    return x1, h2, logits, gt2, k, v, wkv_new, last[:, -1, 0]


def kernel(x_prompt, x_sample, cache_k, cache_v, state_wkv, state_shift, page_table, c_prompt, c_sample,
           w_ada, b_ada, g_mix_pre, g_mix_post, g_ffn_pre, g_ffn_post, w_in, mu_shift,
           decay_base, decay_up, iclr_base, iclr_up, gate_up, k_k, k_a, r_k, lnx_g, lnx_b,
           g_attn_out, sb_bias, w_out, router_w, router_b, e_w1, e_b1, e_w2, e_b2):
    weights = dict(w_ada=w_ada, b_ada=b_ada, g_mix_pre=g_mix_pre, g_mix_post=g_mix_post,
                   g_ffn_pre=g_ffn_pre, g_ffn_post=g_ffn_post, w_in=w_in, mu_shift=mu_shift,
                   decay_base=decay_base, decay_up=decay_up, iclr_base=iclr_base, iclr_up=iclr_up,
                   gate_up=gate_up, k_k=k_k, k_a=k_a, r_k=r_k, lnx_g=lnx_g, lnx_b=lnx_b,
                   g_attn_out=g_attn_out, sb_bias=sb_bias, w_out=w_out, router_w=router_w,
                   router_b=router_b, e_w1=e_w1, e_b1=e_b1, e_w2=e_w2, e_b2=e_b2)
    depth = w_ada.shape[0]
    bp, tp, d = x_prompt.shape
    bs, ts, _ = x_sample.shape
    ts_pad = -(-ts // V7X_SUBLANES) * V7X_SUBLANES
    hp = x_prompt
    hs = jnp.pad(x_sample, ((0, 0), (0, ts_pad - ts), (0, 0)))
    outs = [[] for _ in range(8)]
    for l in range(depth):
        p = {name: w[l] for name, w in weights.items()}
        w_in_bf16 = p['w_in'].astype(bf16)
        w_out_bf16 = p['w_out'].astype(bf16)
        mod = _ada_mod(jnp.concatenate([c_prompt, c_sample], axis=0), p['w_ada'], p['b_ada'])
        mod = mod.reshape(bp + bs, 6, d)

        attend_p = lambda q, k, v: _attn_prompt(q, k, v, p['sb_bias'])
        x1p, h2p, lgp, gt2p, kp, vp, wp, shp = _mix_layer(
            hp, mod[:bp], attend_p, jnp.zeros((bp, SHIFT_W), f32),
            jnp.zeros((bp, H_B, HEAD_DIM, HEAD_DIM), f32), p, w_in_bf16, w_out_bf16,
            nb=1, tt=ROW_TILE, t_real=tp, chunk=WKV_CHUNK, wkv_nseq=WKV_NSEQ)

        attend_s = lambda q, k, v: jnp.pad(
            _attn_sample(q, k, v, cache_k, cache_v, l, page_table, p['sb_bias'], ts),
            ((0, 0), (0, ts_pad - ts), (0, 0)))
        x1s, h2s, lgs, gt2s, ks, vs, ws, shs = _mix_layer(
            hs, mod[bp:], attend_s, state_shift[l], state_wkv[l], p, w_in_bf16, w_out_bf16,
            nb=bs, tt=ts_pad, t_real=ts, chunk=ts_pad, wkv_nseq=WKV_NSEQ)

        mp = bp * tp
        ms = bs * ts
        yp, ys = _moe_residual(
            h2p.reshape((mp,) + h2p.shape[2:]), lgp.reshape(mp, N_EXPERTS), x1p.reshape(mp, d), gt2p,
            h2s[:, :ts].reshape((ms,) + h2s.shape[2:]), lgs[:, :ts].reshape(ms, N_EXPERTS),
            x1s[:, :ts].reshape(ms, d), jnp.broadcast_to(gt2s, (bs, ts, d)).reshape(ms, d),
            p['g_ffn_post'], p['e_w1'], p['e_b1'], p['e_w2'], p['e_b2'], MOE_TILE)
        hp = yp.reshape(bp, tp, d)
        hs = jnp.pad(ys.reshape(bs, ts, d), ((0, 0), (0, ts_pad - ts), (0, 0)))

        for lst, val in zip(outs, (kp.reshape(bp, tp, H_A, HEAD_DIM), vp.reshape(bp, tp, H_A, HEAD_DIM),
                                   ks[:, :ts].reshape(bs, ts, H_A, HEAD_DIM),
                                   vs[:, :ts].reshape(bs, ts, H_A, HEAD_DIM), wp, ws, shp, shs)):
            lst.append(val)
    return (hp, hs[:, :ts]) + tuple(jnp.stack(lst) for lst in outs)
```
